```python
import math
import jax, jax.numpy as jnp
from jax import lax
import numpy as np

D_MODEL = 1024
BATCH = 2
SEQ = 8192
DEPTH = 4
DEC_BATCH = 128
DEC_SEQ = 1
PAST_LEN = 8192
PAGE_SIZE = 128

N_MIXERS = 2
N_SSM_LAYERS = (DEPTH + N_MIXERS - 1) // N_MIXERS
N_SWA_LAYERS = DEPTH - N_SSM_LAYERS
SSM_GROUP = 16
SSM_GROUPS = D_MODEL // SSM_GROUP
SSM_STATE = 64
SSM_CHUNK = 128
DT_MIN = 1e-3
DT_MAX = 1e-1
HEAD_DIM = 64
N_HEADS = D_MODEL // HEAD_DIM
N_KV_HEADS = 4
GQA = N_HEADS // N_KV_HEADS
WINDOW = 128
QKV_WIDTH = (N_HEADS + 2 * N_KV_HEADS) * HEAD_DIM
N_MEM = 256
CA_HEADS = 4
CA_HEAD_DIM = D_MODEL // CA_HEADS
D_FF = ((8 * D_MODEL // 3 + 127) // 128) * 128
FFN_RES = 0.5
EPS = 1e-6
NEG = -1e30

kernel_name = "s5_swa_sink_macaron_memory_decoder_step"


def rmsnorm(x, g):
    xf = x.astype(jnp.float32)
    r = lax.rsqrt(jnp.mean(xf * xf, axis=-1, keepdims=True) + EPS)
    return (xf * r * g.astype(jnp.float32)).astype(x.dtype)


def swiglu(x, w_in, w_out):
    gate, up = jnp.split(x @ w_in, 2, axis=-1)
    return (jax.nn.silu(gate) * up) @ w_out


def _cplx_combine(e1, e2):
    a1r, a1i, b1r, b1i = e1
    a2r, a2i, b2r, b2i = e2
    return (a2r * a1r - a2i * a1i,
            a2r * a1i + a2i * a1r,
            a2r * b1r - a2i * b1i + b2r,
            a2r * b1i + a2i * b1r + b2i)


def s5_mixer(u, h0_re, h0_im, a_re, a_im, log_dt, b_re, b_im, c_re, c_im, d_skip, w_glu, b_glu):
    bsz, L, _ = u.shape
    f32 = jnp.float32
    uf = u.astype(f32)
    dt = jnp.exp(log_dt.astype(f32))[:, None]
    ar, ai = a_re.astype(f32), a_im.astype(f32)
    mag = jnp.exp(ar * dt)
    lr, li = mag * jnp.cos(ai * dt), mag * jnp.sin(ai * dt)
    den = ar * ar + ai * ai
    nr, ni = lr - 1.0, li
    zr = (nr * ar + ni * ai) / den
    zi = (ni * ar - nr * ai) / den
    br, bi = b_re.astype(f32), b_im.astype(f32)
    bbar_r = zr[..., None] * br - zi[..., None] * bi
    bbar_i = zr[..., None] * bi + zi[..., None] * br
    cr, ci = c_re.astype(f32), c_im.astype(f32)
    chunk = SSM_CHUNK if L % SSM_CHUNK == 0 else L
    nc = L // chunk
    uc = uf.reshape(bsz, nc, chunk, SSM_GROUPS, SSM_GROUP).transpose(1, 0, 2, 3, 4)

    def step(carry, u_blk):
        hr, hi = carry
        bur = jnp.einsum('bcgs,gps->bcgp', u_blk, bbar_r)
        bui = jnp.einsum('bcgs,gps->bcgp', u_blk, bbar_i)
        alr = jnp.broadcast_to(lr, bur.shape)
        ali = jnp.broadcast_to(li, bur.shape)
        pr, pi, xr, xi = lax.associative_scan(_cplx_combine, (alr, ali, bur, bui), axis=1)
        hr_t = xr + pr * hr[:, None] - pi * hi[:, None]
        hi_t = xi + pr * hi[:, None] + pi * hr[:, None]
        y = jnp.einsum('bcgp,gsp->bcgs', hr_t, cr) - jnp.einsum('bcgp,gsp->bcgs', hi_t, ci)
        return (hr_t[:, -1], hi_t[:, -1]), y

    (hr, hi), ys = lax.scan(step, (h0_re.astype(f32), h0_im.astype(f32)), uc)
    y = ys.transpose(1, 0, 2, 3, 4).reshape(bsz, L, D_MODEL) + d_skip.astype(f32) * uf
    y = jax.nn.gelu(y, approximate=False)
    a, g = jnp.split(y @ w_glu.astype(f32) + b_glu.astype(f32), 2, axis=-1)
    return (a * jax.nn.sigmoid(g)).astype(u.dtype), hr, hi


def window_attention(x, k_past, v_past, w_qkv, b_qkv, w_o, sinks, has_past):
    bsz, L, _ = x.shape
    f32 = jnp.float32
    W = WINDOW
    qkv = x @ w_qkv + b_qkv
    q, k, v = jnp.split(qkv, [N_HEADS * HEAD_DIM, (N_HEADS + N_KV_HEADS) * HEAD_DIM], axis=-1)
    q = q.reshape(bsz, L, N_KV_HEADS, GQA, HEAD_DIM)
    k = k.reshape(bsz, L, N_KV_HEADS, HEAD_DIM)
    v = v.reshape(bsz, L, N_KV_HEADS, HEAD_DIM)
    nb = -(-L // W)
    pad = nb * W - L
    qb = jnp.pad(q, ((0, 0), (0, pad), (0, 0), (0, 0), (0, 0))).reshape(
        bsz, nb, W, N_KV_HEADS, GQA, HEAD_DIM)

    def bands(t_past, t):
        t_all = jnp.concatenate(
            [t_past.astype(t.dtype), t, jnp.zeros((bsz, pad, N_KV_HEADS, HEAD_DIM), t.dtype)],
            axis=1).reshape(bsz, nb + 1, W, N_KV_HEADS, HEAD_DIM)
        return jnp.concatenate([t_all[:, :-1], t_all[:, 1:]], axis=2)

    kb, vb = bands(k_past, k), bands(v_past, v)
    s = jnp.einsum('bnikgd,bnjkd->bnkgij', qb.astype(f32), kb.astype(f32)) * (HEAD_DIM ** -0.5)
    i = jnp.arange(W)[:, None]
    j = jnp.arange(2 * W)[None, :]
    valid = jnp.broadcast_to((j >= i) & (j <= i + W), (nb, W, 2 * W))
    if not has_past:
        valid = valid & ((jnp.arange(nb)[:, None, None] * W + j[None]) >= W)
    s = jnp.where(valid[None, :, None, None], s, NEG)
    sink = sinks.astype(f32).reshape(N_KV_HEADS, GQA)[None, None, :, :, None, None]
    m = jnp.maximum(s.max(axis=-1, keepdims=True), sink)
    p = jnp.exp(s - m)
    p = p / (p.sum(axis=-1, keepdims=True) + jnp.exp(sink - m))
    o = jnp.einsum('bnkgij,bnjkd->bnikgd', p, vb.astype(f32)).reshape(
        bsz, nb * W, N_HEADS * HEAD_DIM)[:, :L]
    y = o.astype(x.dtype) @ w_o
    buf_k = jnp.concatenate([k_past.astype(k.dtype), k], axis=1)[:, -W:]
    buf_v = jnp.concatenate([v_past.astype(v.dtype), v], axis=1)[:, -W:]
    return y, buf_k, buf_v


def mem_kv(mem, g_mem, w_kv):
    k, v = jnp.split(rmsnorm(mem, g_mem) @ w_kv, 2, axis=-1)
    shp = (mem.shape[0], mem.shape[1], CA_HEADS, CA_HEAD_DIM)
    return k.reshape(shp), v.reshape(shp)


def cross_attention(x, mk, mv, w_q, w_o):
    bsz, L, _ = x.shape
    q = (x @ w_q).reshape(bsz, L, CA_HEADS, CA_HEAD_DIM)
    s = jnp.einsum('blhd,bmhd->bhlm', q.astype(jnp.float32), mk.astype(jnp.float32)) * (CA_HEAD_DIM ** -0.5)
    p = jax.nn.softmax(s, axis=-1)
    o = jnp.einsum('bhlm,bmhd->blhd', p, mv.astype(jnp.float32)).reshape(bsz, L, D_MODEL)
    return o.astype(x.dtype) @ w_o


def setup_inputs(seed: int = 0) -> dict:
    key = jax.random.key(seed)
    ks = iter(jax.random.split(key, 40))
    f32 = jnp.float32

    def nrm(shape, scale=1.0):
        return jax.random.normal(next(ks), shape, f32) * scale

    win = min(WINDOW, PAST_LEN)
    G, P, GS = SSM_GROUPS, SSM_STATE, SSM_GROUP
    return {
        "x_prompt": nrm((BATCH, SEQ, D_MODEL)),
        "x_sample": nrm((DEC_BATCH, DEC_SEQ, D_MODEL)),
        "mem_prompt": nrm((BATCH, N_MEM, D_MODEL)),
        "state_ssm_re": nrm((N_SSM_LAYERS, DEC_BATCH, G, P), 0.1),
        "state_ssm_im": nrm((N_SSM_LAYERS, DEC_BATCH, G, P), 0.1),
        "cache_win_k": nrm((N_SWA_LAYERS, DEC_BATCH, win, N_KV_HEADS, HEAD_DIM)),
        "cache_win_v": nrm((N_SWA_LAYERS, DEC_BATCH, win, N_KV_HEADS, HEAD_DIM)),
        "cache_mem_k": nrm((DEPTH, DEC_BATCH, N_MEM, CA_HEADS, CA_HEAD_DIM)),
        "cache_mem_v": nrm((DEPTH, DEC_BATCH, N_MEM, CA_HEADS, CA_HEAD_DIM)),
        "norm_g": 1.0 + nrm((DEPTH, 8, D_MODEL), 0.02),
        "mem_norm_g": 1.0 + nrm((DEPTH, D_MODEL), 0.02),
        "ffn_w_in": nrm((DEPTH, 2, D_MODEL, 2 * D_FF), D_MODEL ** -0.5),
        "ffn_w_out": nrm((DEPTH, 2, D_FF, D_MODEL), D_FF ** -0.5),
        "ssm_a_re": -0.5 + nrm((N_SSM_LAYERS, G, P), 0.01),
        "ssm_a_im": math.pi * jnp.arange(P, dtype=f32) + nrm((N_SSM_LAYERS, G, P), 0.01),
        "ssm_log_dt": jax.random.uniform(next(ks), (N_SSM_LAYERS, G), f32,
                                         math.log(DT_MIN), math.log(DT_MAX)),
        "ssm_b_re": nrm((N_SSM_LAYERS, G, P, GS), (2 * GS) ** -0.5),
        "ssm_b_im": nrm((N_SSM_LAYERS, G, P, GS), (2 * GS) ** -0.5),
        "ssm_c_re": nrm((N_SSM_LAYERS, G, GS, P), P ** -0.5),
        "ssm_c_im": nrm((N_SSM_LAYERS, G, GS, P), P ** -0.5),
        "ssm_d": nrm((N_SSM_LAYERS, D_MODEL)),
        "ssm_w_glu": nrm((N_SSM_LAYERS, D_MODEL, 2 * D_MODEL), D_MODEL ** -0.5),
        "ssm_b_glu": nrm((N_SSM_LAYERS, 2 * D_MODEL), 0.02),
        "attn_w_qkv": nrm((N_SWA_LAYERS, D_MODEL, QKV_WIDTH), D_MODEL ** -0.5),
        "attn_b_qkv": nrm((N_SWA_LAYERS, QKV_WIDTH), 0.02),
        "attn_w_o": nrm((N_SWA_LAYERS, N_HEADS * HEAD_DIM, D_MODEL), (N_HEADS * HEAD_DIM) ** -0.5),
        "attn_sinks": nrm((N_SWA_LAYERS, N_HEADS), 0.5),
        "ca_w_q": nrm((DEPTH, D_MODEL, D_MODEL), D_MODEL ** -0.5),
        "ca_w_kv": nrm((DEPTH, D_MODEL, 2 * D_MODEL), D_MODEL ** -0.5),
        "ca_w_o": nrm((DEPTH, D_MODEL, D_MODEL), D_MODEL ** -0.5),
    }


def reference(x_prompt, x_sample, mem_prompt, state_ssm_re, state_ssm_im, cache_win_k, cache_win_v,
              cache_mem_k, cache_mem_v, norm_g, mem_norm_g, ffn_w_in, ffn_w_out,
              ssm_a_re, ssm_a_im, ssm_log_dt, ssm_b_re, ssm_b_im, ssm_c_re, ssm_c_im, ssm_d,
              ssm_w_glu, ssm_b_glu, attn_w_qkv, attn_b_qkv, attn_w_o, attn_sinks,
              ca_w_q, ca_w_kv, ca_w_o):
    xp, xs = x_prompt, x_sample
    bp = xp.shape[0]
    ssm_re_p, ssm_im_p, ssm_re_s, ssm_im_s = [], [], [], []
    wk_p, wv_p, wk_s, wv_s = [], [], [], []
    mk_list, mv_list = [], []

    def half_ffn(x, i, j):
        g = norm_g[i]
        h = swiglu(rmsnorm(x, g[6 * j]), ffn_w_in[i, j], ffn_w_out[i, j])
        return x + FFN_RES * rmsnorm(h, g[6 * j + 1])

    for i in range(DEPTH):
        g = norm_g[i]
        xp = half_ffn(xp, i, 0)
        xs = half_ffn(xs, i, 0)
        li = i // N_MIXERS
        if i % N_MIXERS == 0:
            prm = (ssm_a_re[li], ssm_a_im[li], ssm_log_dt[li], ssm_b_re[li], ssm_b_im[li],
                   ssm_c_re[li], ssm_c_im[li], ssm_d[li], ssm_w_glu[li], ssm_b_glu[li])
            h0 = jnp.zeros((bp, SSM_GROUPS, SSM_STATE), jnp.float32)
            yp, hr_p, hi_p = s5_mixer(rmsnorm(xp, g[2]), h0, h0, *prm)
            ys, hr_s, hi_s = s5_mixer(rmsnorm(xs, g[2]), state_ssm_re[li], state_ssm_im[li], *prm)
            ssm_re_p.append(hr_p); ssm_im_p.append(hi_p)
            ssm_re_s.append(hr_s); ssm_im_s.append(hi_s)
        else:
            prm = (attn_w_qkv[li], attn_b_qkv[li], attn_w_o[li], attn_sinks[li])
            zpast = jnp.zeros((bp, WINDOW, N_KV_HEADS, HEAD_DIM), xp.dtype)
            yp, bk_p, bv_p = window_attention(rmsnorm(xp, g[2]), zpast, zpast, *prm, has_past=False)
            ys, bk_s, bv_s = window_attention(rmsnorm(xs, g[2]), cache_win_k[li], cache_win_v[li],
                                              *prm, has_past=True)
            wk_p.append(bk_p); wv_p.append(bv_p)
            wk_s.append(bk_s); wv_s.append(bv_s)
        xp = xp + rmsnorm(yp, g[3])
        xs = xs + rmsnorm(ys, g[3])
        mk, mv = mem_kv(mem_prompt, mem_norm_g[i], ca_w_kv[i])
        mk_list.append(mk); mv_list.append(mv)
        xp = xp + rmsnorm(cross_attention(rmsnorm(xp, g[4]), mk, mv, ca_w_q[i], ca_w_o[i]), g[5])
        xs = xs + rmsnorm(cross_attention(rmsnorm(xs, g[4]), cache_mem_k[i], cache_mem_v[i],
                                          ca_w_q[i], ca_w_o[i]), g[5])
        xp = half_ffn(xp, i, 1)
        xs = half_ffn(xs, i, 1)

    new_ssm_re_prompt = jnp.stack(ssm_re_p)
    new_ssm_im_prompt = jnp.stack(ssm_im_p)
    new_win_k_prompt = jnp.stack(wk_p)
    new_win_v_prompt = jnp.stack(wv_p)
    new_mem_k_prompt = jnp.stack(mk_list)
    new_mem_v_prompt = jnp.stack(mv_list)
    new_ssm_re_sample = jnp.stack(ssm_re_s)
    new_ssm_im_sample = jnp.stack(ssm_im_s)
    new_win_k_sample = jnp.stack(wk_s)
    new_win_v_sample = jnp.stack(wv_s)
    return (xp, xs, new_ssm_re_prompt, new_ssm_im_prompt, new_win_k_prompt, new_win_v_prompt,
            new_mem_k_prompt, new_mem_v_prompt, new_ssm_re_sample, new_ssm_im_sample,
            new_win_k_sample, new_win_v_sample)
```

```python
import functools
import math

import jax
import jax.numpy as jnp
from jax import lax
from jax.experimental import pallas as pl
from jax.experimental.pallas import tpu as pltpu

F32 = jnp.float32
BF16 = jnp.bfloat16

D_MODEL = 1024
DEPTH = 4
N_MIXERS = 2
SSM_GROUP = 16
SSM_GROUPS = D_MODEL // SSM_GROUP
SSM_STATE = 64
HEAD_DIM = 64
N_HEADS = D_MODEL // HEAD_DIM
N_KV_HEADS = 4
GQA = N_HEADS // N_KV_HEADS
WINDOW = 128
KV_WIDTH = N_KV_HEADS * HEAD_DIM
QKV_WIDTH = (N_HEADS + 2 * N_KV_HEADS) * HEAD_DIM
N_MEM = 256
CA_HEADS = 4
CA_HEAD_DIM = D_MODEL // CA_HEADS
D_FF = ((8 * D_MODEL // 3 + 127) // 128) * 128
FFN_RES = 0.5
EPS = 1e-6
NEG = -1e30

SUBLANES = 8
LANES = 128
VMEM_LIMIT_BYTES = 56 * 1024 * 1024

SSM_LAGS = SUBLANES
SSM_TILE_GROUPS = LANES // SSM_GROUP
SSM_TILES = SSM_GROUPS // SSM_TILE_GROUPS
SSM_TILE_STATE = SSM_TILE_GROUPS * SSM_STATE

FFN_CHUNK = 256
FFN_TOKENS = 1024
CA_TOKENS = 512
SWA_TOKENS = 512
SSM_TOKENS = 512
SAMPLE_CA_BLOCK = 4
SAMPLE_SWA_BLOCK = 16


def _params(*sem):
    return pltpu.CompilerParams(dimension_semantics=sem, vmem_limit_bytes=VMEM_LIMIT_BYTES)


def _rms(x, g):
    r = lax.rsqrt(jnp.mean(x * x, axis=-1, keepdims=True) + EPS)
    return x * r * g


def _dot(a, b):
    return jnp.dot(a, b, preferred_element_type=F32)


def _dot_nt(a, b):
    return lax.dot_general(a, b, (((1,), (1,)), ((), ())), preferred_element_type=F32)


def _const_spec(shape):
    zeros = (0,) * len(shape)
    return pl.BlockSpec(shape, lambda *_: zeros)


def _ffn_kernel(x_ref, gpre_ref, gpost_ref, wg_ref, wu_ref, wo_ref, o_ref, xn_ref, acc_ref):
    c = pl.program_id(1)

    @pl.when(c == 0)
    def _():
        xn_ref[...] = _rms(x_ref[...], gpre_ref[...]).astype(BF16)
        acc_ref[...] = jnp.zeros_like(acc_ref)

    xn = xn_ref[...]
    gate = _dot(xn, wg_ref[...])
    up = _dot(xn, wu_ref[...])
    h = (gate * jax.nn.sigmoid(gate) * up).astype(BF16)
    acc_ref[...] += _dot(h, wo_ref[...])

    @pl.when(c == pl.num_programs(1) - 1)
    def _():
        o_ref[...] = x_ref[...] + FFN_RES * _rms(acc_ref[...], gpost_ref[...])


def _half_ffn(x, g_pre, g_post, w_in, w_out, tokens):
    n = x.shape[0]
    tm = min(tokens, n)
    n_chunks = D_FF // FFN_CHUNK
    return pl.pallas_call(
        _ffn_kernel,
        grid=(n // tm, n_chunks),
        in_specs=[
            pl.BlockSpec((tm, D_MODEL), lambda i, c: (i, 0)),
            _const_spec((1, D_MODEL)),
            _const_spec((1, D_MODEL)),
            pl.BlockSpec((D_MODEL, FFN_CHUNK), lambda i, c: (0, c)),
            pl.BlockSpec((D_MODEL, FFN_CHUNK), lambda i, c: (0, c + n_chunks)),
            pl.BlockSpec((FFN_CHUNK, D_MODEL), lambda i, c: (c, 0)),
        ],
        out_specs=pl.BlockSpec((tm, D_MODEL), lambda i, c: (i, 0)),
        out_shape=jax.ShapeDtypeStruct((n, D_MODEL), F32),
        scratch_shapes=[pltpu.VMEM((tm, D_MODEL), BF16), pltpu.VMEM((tm, D_MODEL), F32)],
        compiler_params=_params("parallel", "arbitrary"),
        name="half_ffn",
    )(x, g_pre, g_post, w_in, w_in, w_out)


def _mem_kv_kernel(mem_ref, g_ref, w_ref, k_ref, v_ref):
    mn = _rms(mem_ref[...], g_ref[...]).astype(BF16)
    kv = _dot(mn, w_ref[...])
    k_ref[...] = kv[:, :D_MODEL]
    v_ref[...] = kv[:, D_MODEL:]


def _mem_kv(mem, g_mem, w_kv):
    bsz = mem.shape[0]
    out = jax.ShapeDtypeStruct((DEPTH, bsz, N_MEM, D_MODEL), F32)
    out_spec = pl.BlockSpec((None, None, N_MEM, D_MODEL), lambda l, b: (l, b, 0, 0))
    return pl.pallas_call(
        _mem_kv_kernel,
        grid=(DEPTH, bsz),
        in_specs=[
            pl.BlockSpec((None, N_MEM, D_MODEL), lambda l, b: (b, 0, 0)),
            pl.BlockSpec((None, 1, D_MODEL), lambda l, b: (l, 0, 0)),
            pl.BlockSpec((None, D_MODEL, 2 * D_MODEL), lambda l, b: (l, 0, 0)),
        ],
        out_specs=[out_spec, out_spec],
        out_shape=[out, out],
        compiler_params=_params("parallel", "parallel"),
        name="mem_kv",
    )(mem, g_mem.reshape(DEPTH, 1, D_MODEL), w_kv)


def _ca_kernel(x_ref, gpre_ref, gpost_ref, wq_ref, wo_ref, mk_ref, mv_ref, o_ref):
    x = x_ref[...]
    xn = _rms(x, gpre_ref[...]).astype(BF16)
    q = _dot(xn, wq_ref[...])
    heads = []
    for h in range(CA_HEADS):
        cols = slice(h * CA_HEAD_DIM, (h + 1) * CA_HEAD_DIM)
        s = _dot_nt(q[:, cols].astype(BF16), mk_ref[:, cols].astype(BF16)) * (CA_HEAD_DIM ** -0.5)
        p = jnp.exp(s - jnp.max(s, axis=-1, keepdims=True))
        den = jnp.sum(p, axis=-1, keepdims=True)
        heads.append(_dot(p.astype(BF16), mv_ref[:, cols].astype(BF16)) / den)
    o = jnp.concatenate(heads, axis=-1).astype(BF16)
    o_ref[...] = x + _rms(_dot(o, wo_ref[...]), gpost_ref[...])


def _cross_attention_prompt(x, g_pre, g_post, w_q, w_o, mk, mv, seq):
    n = x.shape[0]
    tm = CA_TOKENS
    per_seq = seq // tm
    mem_spec = pl.BlockSpec((None, N_MEM, D_MODEL), lambda i: (i // per_seq, 0, 0))
    return pl.pallas_call(
        _ca_kernel,
        grid=(n // tm,),
        in_specs=[
            pl.BlockSpec((tm, D_MODEL), lambda i: (i, 0)),
            _const_spec((1, D_MODEL)),
            _const_spec((1, D_MODEL)),
            _const_spec((D_MODEL, D_MODEL)),
            _const_spec((D_MODEL, D_MODEL)),
            mem_spec,
            mem_spec,
        ],
        out_specs=pl.BlockSpec((tm, D_MODEL), lambda i: (i, 0)),
        out_shape=jax.ShapeDtypeStruct((n, D_MODEL), F32),
        compiler_params=_params("parallel"),
        name="cross_attn_prompt",
    )(x, g_pre, g_post, w_q, w_o, mk, mv)


def _ca_sample_kernel(x_ref, gpre_ref, gpost_ref, wq_ref, wo_ref, mk_ref, mv_ref, o_ref, q_ref, att_ref):
    i = pl.program_id(0)

    @pl.when(i == 0)
    def _():
        xn = _rms(x_ref[...], gpre_ref[...]).astype(BF16)
        q_ref[...] = _dot(xn, wq_ref[...]) * (CA_HEAD_DIM ** -0.5)

    for b in range(SAMPLE_CA_BLOCK):
        row = i * SAMPLE_CA_BLOCK + b
        q = q_ref[pl.ds(row, 1), :]
        prod = mk_ref[b] * q
        heads = []
        for h in range(CA_HEADS):
            cols = slice(h * CA_HEAD_DIM, (h + 1) * CA_HEAD_DIM)
            s = jnp.sum(prod[:, cols], axis=-1, keepdims=True)
            p = jnp.exp(s - jnp.max(s, axis=0, keepdims=True))
            den = jnp.sum(p, axis=0, keepdims=True)
            heads.append(jnp.sum(p * mv_ref[b, :, cols], axis=0, keepdims=True) / den)
        att_ref[pl.ds(row, 1), :] = jnp.concatenate(heads, axis=-1)

    @pl.when(i == pl.num_programs(0) - 1)
    def _():
        y = _dot(att_ref[...].astype(BF16), wo_ref[...])
        o_ref[...] = x_ref[...] + _rms(y, gpost_ref[...])


def _cross_attention_sample(x, g_pre, g_post, w_q, w_o, mk, mv):
    n = x.shape[0]
    mem_spec = pl.BlockSpec((SAMPLE_CA_BLOCK, N_MEM, D_MODEL), lambda i: (i, 0, 0))
    return pl.pallas_call(
        _ca_sample_kernel,
        grid=(n // SAMPLE_CA_BLOCK,),
        in_specs=[
            _const_spec((n, D_MODEL)),
            _const_spec((1, D_MODEL)),
            _const_spec((1, D_MODEL)),
            _const_spec((D_MODEL, D_MODEL)),
            _const_spec((D_MODEL, D_MODEL)),
            mem_spec,
            mem_spec,
        ],
        out_specs=_const_spec((n, D_MODEL)),
        out_shape=jax.ShapeDtypeStruct((n, D_MODEL), F32),
        scratch_shapes=[pltpu.VMEM((n, D_MODEL), F32), pltpu.VMEM((n, D_MODEL), F32)],
        compiler_params=_params("arbitrary"),
        name="cross_attn_sample",
    )(x, g_pre, g_post, w_q, w_o, mk, mv)


def _swa_kernel(sink_ref, x_ref, gpre_ref, gpost_ref, wqkv_ref, bqkv_ref, wo_ref,
                o_ref, kout_ref, vout_ref, kprev_ref, vprev_ref, att_ref):
    t = pl.program_id(1)
    tm = x_ref.shape[0]
    x = x_ref[...]
    xn = _rms(x, gpre_ref[...]).astype(BF16)
    qkv = _dot(xn, wqkv_ref[...]) + bqkv_ref[...]
    k = qkv[:, N_HEADS * HEAD_DIM:N_HEADS * HEAD_DIM + KV_WIDTH]
    v = qkv[:, N_HEADS * HEAD_DIM + KV_WIDTH:]

    cur, nxt = t % 2, (t + 1) % 2

    @pl.when(t == 0)
    def _():
        kprev_ref[0] = jnp.zeros((WINDOW, KV_WIDTH), BF16)
        vprev_ref[0] = jnp.zeros((WINDOW, KV_WIDTH), BF16)

    kall = jnp.concatenate([kprev_ref[cur], k.astype(BF16)], axis=0)
    vall = jnp.concatenate([vprev_ref[cur], v.astype(BF16)], axis=0)

    row = lax.broadcasted_iota(jnp.int32, (GQA * WINDOW, 2 * WINDOW), 0)
    qi = row & (WINDOW - 1)
    kj = lax.broadcasted_iota(jnp.int32, (GQA * WINDOW, 2 * WINDOW), 1)
    band = (kj >= qi) & (kj <= qi + WINDOW)
    row_g = lax.broadcasted_iota(jnp.int32, (GQA * WINDOW, 1), 0) // WINDOW

    for n in range(tm // WINDOW):
        rows = slice(n * WINDOW, (n + 1) * WINDOW)
        kk = kall[n * WINDOW:(n + 2) * WINDOW, :]
        vv = vall[n * WINDOW:(n + 2) * WINDOW, :]
        valid = band & (kj >= WINDOW - (t * tm + n * WINDOW)) if n == 0 else band
        for kh in range(N_KV_HEADS):
            kv_cols = slice(kh * HEAD_DIM, (kh + 1) * HEAD_DIM)
            q4 = jnp.concatenate(
                [qkv[rows, (kh * GQA + g) * HEAD_DIM:(kh * GQA + g + 1) * HEAD_DIM] for g in range(GQA)],
                axis=0).astype(BF16)
            sink = jnp.full((GQA * WINDOW, 1), sink_ref[kh * GQA], F32)
            for g in range(1, GQA):
                sink = jnp.where(row_g == g, sink_ref[kh * GQA + g], sink)
            s = _dot_nt(q4, kk[:, kv_cols]) * (HEAD_DIM ** -0.5)
            s = jnp.where(valid, s, NEG)
            m = jnp.maximum(jnp.max(s, axis=-1, keepdims=True), sink)
            p = jnp.exp(s - m)
            den = jnp.sum(p, axis=-1, keepdims=True) + jnp.exp(sink - m)
            o4 = _dot(p.astype(BF16), vv[:, kv_cols]) / den
            for g in range(GQA):
                h = kh * GQA + g
                att_ref[rows, h * HEAD_DIM:(h + 1) * HEAD_DIM] = o4[g * WINDOW:(g + 1) * WINDOW]

    y = _dot(att_ref[...].astype(BF16), wo_ref[...])
    o_ref[...] = x + _rms(y, gpost_ref[...])

    kprev_ref[nxt] = k[tm - WINDOW:, :].astype(BF16)
    vprev_ref[nxt] = v[tm - WINDOW:, :].astype(BF16)

    kout_ref[...] = k[tm - WINDOW:, :]
    vout_ref[...] = v[tm - WINDOW:, :]


def _window_attention_prompt(x, g_pre, g_post, w_qkv, b_qkv, w_o, sinks, bsz, seq):
    tm = SWA_TOKENS
    per_seq = seq // tm
    win_spec = pl.BlockSpec((None, WINDOW, KV_WIDTH), lambda b, t, *_: (b, 0, 0))
    win_shape = jax.ShapeDtypeStruct((bsz, WINDOW, KV_WIDTH), F32)
    grid_spec = pltpu.PrefetchScalarGridSpec(
        num_scalar_prefetch=1,
        grid=(bsz, per_seq),
        in_specs=[
            pl.BlockSpec((tm, D_MODEL), lambda b, t, *_: (b * per_seq + t, 0)),
            _const_spec((1, D_MODEL)),
            _const_spec((1, D_MODEL)),
            _const_spec((D_MODEL, QKV_WIDTH)),
            _const_spec((1, QKV_WIDTH)),
            _const_spec((N_HEADS * HEAD_DIM, D_MODEL)),
        ],
        out_specs=[pl.BlockSpec((tm, D_MODEL), lambda b, t, *_: (b * per_seq + t, 0)), win_spec, win_spec],
        scratch_shapes=[
            pltpu.VMEM((2, WINDOW, KV_WIDTH), BF16),
            pltpu.VMEM((2, WINDOW, KV_WIDTH), BF16),
            pltpu.VMEM((tm, N_HEADS * HEAD_DIM), F32),
        ],
    )
    return pl.pallas_call(
        _swa_kernel,
        grid_spec=grid_spec,
        out_shape=[jax.ShapeDtypeStruct(x.shape, F32), win_shape, win_shape],
        compiler_params=_params("arbitrary", "arbitrary"),
        name="window_attn_prompt",
    )(sinks, x, g_pre, g_post, w_qkv, b_qkv, w_o)


def _swa_sample_kernel(sink_ref, x_ref, gpre_ref, gpost_ref, wqkv_ref, bqkv_ref, wo_ref, ck_ref, cv_ref,
                       o_ref, nk_ref, nv_ref, qkv_ref, att_ref):
    i = pl.program_id(0)
    k0 = N_HEADS * HEAD_DIM
    v0 = k0 + KV_WIDTH

    @pl.when(i == 0)
    def _():
        xn = _rms(x_ref[...], gpre_ref[...]).astype(BF16)
        qkv_ref[...] = _dot(xn, wqkv_ref[...]) + bqkv_ref[...]

    for b in range(SAMPLE_SWA_BLOCK):
        row = i * SAMPLE_SWA_BLOCK + b
        qkv = qkv_ref[pl.ds(row, 1), :]
        kc = ck_ref[b]
        vc = cv_ref[b]
        k_new = qkv[:, k0:v0]
        v_new = qkv[:, v0:]
        heads = []
        for h in range(N_HEADS):
            kh = h // GQA
            kv_cols = slice(kh * HEAD_DIM, (kh + 1) * HEAD_DIM)
            q = qkv[:, h * HEAD_DIM:(h + 1) * HEAD_DIM] * (HEAD_DIM ** -0.5)
            s = jnp.sum(kc[:, kv_cols] * q, axis=-1, keepdims=True)
            s_new = jnp.sum(k_new[:, kv_cols] * q, axis=-1, keepdims=True)
            sink = sink_ref[h]
            m = jnp.maximum(jnp.maximum(jnp.max(s, axis=0, keepdims=True), s_new), sink)
            p = jnp.exp(s - m)
            p_new = jnp.exp(s_new - m)
            den = jnp.sum(p, axis=0, keepdims=True) + p_new + jnp.exp(sink - m)
            o = jnp.sum(p * vc[:, kv_cols], axis=0, keepdims=True) + p_new * v_new[:, kv_cols]
            heads.append(o / den)
        att_ref[pl.ds(row, 1), :] = jnp.concatenate(heads, axis=-1)
        nk_ref[b, 0:WINDOW - 1, :] = kc[1:, :]
        nv_ref[b, 0:WINDOW - 1, :] = vc[1:, :]
        nk_ref[b, WINDOW - 1:WINDOW, :] = k_new
        nv_ref[b, WINDOW - 1:WINDOW, :] = v_new

    @pl.when(i == pl.num_programs(0) - 1)
    def _():
        y = _dot(att_ref[...].astype(BF16), wo_ref[...])
        o_ref[...] = x_ref[...] + _rms(y, gpost_ref[...])


def _window_attention_sample(x, g_pre, g_post, w_qkv, b_qkv, w_o, sinks, cache_k, cache_v):
    n = x.shape[0]
    blk = SAMPLE_SWA_BLOCK
    win_spec = pl.BlockSpec((blk, WINDOW, KV_WIDTH), lambda i, *_: (i, 0, 0))
    win_shape = jax.ShapeDtypeStruct((n, WINDOW, KV_WIDTH), F32)
    grid_spec = pltpu.PrefetchScalarGridSpec(
        num_scalar_prefetch=1,
        grid=(n // blk,),
        in_specs=[
            _const_spec((n, D_MODEL)),
            _const_spec((1, D_MODEL)),
            _const_spec((1, D_MODEL)),
            _const_spec((D_MODEL, QKV_WIDTH)),
            _const_spec((1, QKV_WIDTH)),
            _const_spec((N_HEADS * HEAD_DIM, D_MODEL)),
            win_spec,
            win_spec,
        ],
        out_specs=[_const_spec((n, D_MODEL)), win_spec, win_spec],
        scratch_shapes=[pltpu.VMEM((n, QKV_WIDTH), F32), pltpu.VMEM((n, N_HEADS * HEAD_DIM), F32)],
    )
    return pl.pallas_call(
        _swa_sample_kernel,
        grid_spec=grid_spec,
        out_shape=[jax.ShapeDtypeStruct(x.shape, F32), win_shape, win_shape],
        compiler_params=_params("arbitrary"),
        name="window_attn_sample",
    )(sinks, x, g_pre, g_post, w_qkv, b_qkv, w_o, cache_k, cache_v)


def _ssm_prep_kernel(ar_ref, ai_ref, dt_ref, br_ref, bi_ref, wr_ref, wi_ref, lam_ref):
    ar, ai, dt = ar_ref[...], ai_ref[...], jnp.exp(dt_ref[...])
    mag = jnp.exp(ar * dt)
    lr, li = mag * jnp.cos(ai * dt), mag * jnp.sin(ai * dt)
    den = ar * ar + ai * ai
    nr, ni = lr - 1.0, li
    zr = (nr * ar + ni * ai) / den
    zi = (ni * ar - nr * ai) / den
    br, bi = br_ref[...], bi_ref[...]
    wr = zr * br - zi * bi
    wi = zr * bi + zi * br
    pr, pi = lr, li
    for k in range(SSM_LAGS):
        wr_ref[k] = wr
        wi_ref[k] = wi
        wr, wi = lr * wr - li * wi, lr * wi + li * wr
        if k > 0:
            pr, pi = lr * pr - li * pi, lr * pi + li * pr
    lam_ref[0] = lr
    lam_ref[1] = li
    lam_ref[2] = pr
    lam_ref[3] = pi


def _ssm_tables(a_re, a_im, log_dt, b_re, b_im, c_re, c_im):
    G, P, GS, R, TG, NT = SSM_GROUPS, SSM_STATE, SSM_GROUP, SSM_LAGS, SSM_TILE_GROUPS, SSM_TILES
    rows = G * P
    dense = (rows * GS // LANES, LANES)
    spread = lambda a: jnp.broadcast_to(a.astype(F32)[:, :, None], (G, P, GS)).reshape(dense)
    log_dt = jnp.broadcast_to(log_dt.astype(F32)[:, None], (G, P))
    wr, wi, lam = pl.pallas_call(
        _ssm_prep_kernel,
        out_shape=[jax.ShapeDtypeStruct((R,) + dense, F32), jax.ShapeDtypeStruct((R,) + dense, F32),
                   jax.ShapeDtypeStruct((4,) + dense, F32)],
        name="ssm_prep",
    )(spread(a_re), spread(a_im), spread(log_dt), b_re.astype(F32).reshape(dense), b_im.astype(F32).reshape(dense))
    wr, wi = wr.reshape(R, rows, GS), wi.reshape(R, rows, GS)
    lam = lam.reshape(4, rows, GS)[:, :, 0]
    eye = jnp.eye(TG, dtype=F32)
    w = jnp.stack([wr, wi]).reshape(2, R, NT, TG, P, GS).transpose(2, 1, 3, 5, 0, 4)
    w_lag = (w[:, :, :, :, :, None, :] * eye[None, None, :, None, None, :, None]).reshape(
        NT, R * LANES, 2 * SSM_TILE_STATE).astype(BF16)

    def c_blocks(cm):
        ct = cm.astype(F32).reshape(NT, TG, GS, P).transpose(0, 1, 3, 2)
        return (ct[:, :, :, None, :] * eye[None, :, None, :, None]).reshape(NT, SSM_TILE_STATE, LANES).astype(BF16)

    def lam_rows(re, im):
        return jnp.concatenate([re.reshape(NT, 1, SSM_TILE_STATE), im.reshape(NT, 1, SSM_TILE_STATE)], axis=-1)

    return (w_lag, c_blocks(c_re), c_blocks(c_im), lam_rows(lam[0], lam[1]), lam_rows(lam[2], lam[3]))


def _glu_tail(x, u, y, d_ref, wglu_ref, bglu_ref, gpost_ref):
    y = y + d_ref[...] * u
    y = 0.5 * y * (1.0 + lax.erf(y * (2.0 ** -0.5)))
    z = _dot(y.astype(BF16), wglu_ref[...]) + bglu_ref[...]
    out = z[:, :D_MODEL] * jax.nn.sigmoid(z[:, D_MODEL:])
    return x + _rms(out, gpost_ref[...])


def _ssm_kernel(x_ref, gpre_ref, gpost_ref, wlag_ref, cr_ref, ci_ref, lamk_ref, d_ref, wglu_ref, bglu_ref,
                o_ref, hout_ref, ubuf_ref, uprev_ref, h2_ref, y_ref, carry_ref):
    t = pl.program_id(1)
    tm = x_ref.shape[0]
    S = SSM_TILE_STATE
    cur, nxt = t % 2, (t + 1) % 2

    @pl.when(t == 0)
    def _():
        uprev_ref[0] = jnp.zeros((SSM_LAGS, D_MODEL), F32)
        carry_ref[...] = jnp.zeros_like(carry_ref)

    x = x_ref[...]
    u = _rms(x, gpre_ref[...])
    ubuf_ref[0:SSM_LAGS, :] = uprev_ref[cur]
    ubuf_ref[SSM_LAGS:, :] = u

    for c in range(SSM_TILES):
        cols = slice(c * LANES, (c + 1) * LANES)
        h_ref = h2_ref.at[c % 2]
        lhs = jnp.concatenate(
            [ubuf_ref[SSM_LAGS - k:SSM_LAGS - k + tm, cols] for k in range(SSM_LAGS)], axis=1).astype(BF16)
        h_ref[...] = _dot(lhs, wlag_ref[c])
        lam = lamk_ref[c]
        lr = jnp.broadcast_to(lam[:, :S], (SUBLANES, S))
        li = jnp.broadcast_to(lam[:, S:], (SUBLANES, S))

        def slab(m, carry):
            r0 = pl.multiple_of(m * SUBLANES, SUBLANES)
            hw = h_ref[pl.ds(r0, SUBLANES), :]
            cr, ci = carry[:, :S], carry[:, S:]
            new = jnp.concatenate([hw[:, :S] + (lr * cr - li * ci), hw[:, S:] + (lr * ci + li * cr)], axis=1)
            h_ref[pl.ds(r0, SUBLANES), :] = new
            return new

        carry_ref[c] = lax.fori_loop(0, tm // SUBLANES, slab, carry_ref[c])
        y_ref[:, cols] = (_dot(h_ref[:, :S].astype(BF16), cr_ref[c])
                          - _dot(h_ref[:, S:].astype(BF16), ci_ref[c]))

    o_ref[...] = _glu_tail(x, u, y_ref[...], d_ref, wglu_ref, bglu_ref, gpost_ref)
    uprev_ref[nxt] = u[tm - SSM_LAGS:, :]

    @pl.when(t == pl.num_programs(1) - 1)
    def _():
        hout_ref[...] = carry_ref[...]


def _ssm_prompt(x, g_pre, g_post, tables, d_skip, w_glu, b_glu, bsz, seq):
    w_lag, c_r, c_i, _, lamk = tables
    tm = SSM_TOKENS
    per_seq = seq // tm
    S2 = 2 * SSM_TILE_STATE
    once = pl.Buffered(1)
    out, h_last = pl.pallas_call(
        _ssm_kernel,
        grid=(bsz, per_seq),
        in_specs=[
            pl.BlockSpec((tm, D_MODEL), lambda b, t: (b * per_seq + t, 0)),
            _const_spec((1, D_MODEL)),
            _const_spec((1, D_MODEL)),
            pl.BlockSpec(w_lag.shape, lambda b, t: (0, 0, 0), pipeline_mode=once),
            pl.BlockSpec(c_r.shape, lambda b, t: (0, 0, 0), pipeline_mode=once),
            pl.BlockSpec(c_i.shape, lambda b, t: (0, 0, 0), pipeline_mode=once),
            _const_spec(lamk.shape),
            _const_spec((1, D_MODEL)),
            pl.BlockSpec(w_glu.shape, lambda b, t: (0, 0), pipeline_mode=once),
            _const_spec((1, 2 * D_MODEL)),
        ],
        out_specs=[
            pl.BlockSpec((tm, D_MODEL), lambda b, t: (b * per_seq + t, 0)),
            pl.BlockSpec((None, SSM_TILES, SUBLANES, S2), lambda b, t: (b, 0, 0, 0)),
        ],
        out_shape=[jax.ShapeDtypeStruct(x.shape, F32),
                   jax.ShapeDtypeStruct((bsz, SSM_TILES, SUBLANES, S2), F32)],
        scratch_shapes=[
            pltpu.VMEM((tm + SSM_LAGS, D_MODEL), F32),
            pltpu.VMEM((2, SSM_LAGS, D_MODEL), F32),
            pltpu.VMEM((2, tm, S2), F32),
            pltpu.VMEM((tm, D_MODEL), F32),
            pltpu.VMEM((SSM_TILES, SUBLANES, S2), F32),
        ],
        compiler_params=_params("arbitrary", "arbitrary"),
        name="ssm_prompt",
    )(x, g_pre, g_post, w_lag, c_r, c_i, lamk, d_skip, w_glu, b_glu)
    h_last = h_last[:, :, SUBLANES - 1, :]
    shape = (bsz, SSM_GROUPS, SSM_STATE)
    return out, h_last[..., :SSM_TILE_STATE].reshape(shape), h_last[..., SSM_TILE_STATE:].reshape(shape)


def _ssm_sample_kernel(x_ref, sre_ref, sim_ref, gpre_ref, gpost_ref, w0_ref, cr_ref, ci_ref, lam1_ref,
                       d_ref, wglu_ref, bglu_ref, o_ref, nre_ref, nim_ref, y_ref):
    S = SSM_TILE_STATE
    x = x_ref[...]
    u = _rms(x, gpre_ref[...])
    for c in range(SSM_TILES):
        cols = slice(c * LANES, (c + 1) * LANES)
        st = slice(c * S, (c + 1) * S)
        bu = _dot(u[:, cols].astype(BF16), w0_ref[c])
        lam = lam1_ref[c]
        lr, li = lam[:, :S], lam[:, S:]
        h0r, h0i = sre_ref[:, st], sim_ref[:, st]
        hr = bu[:, :S] + (lr * h0r - li * h0i)
        hi = bu[:, S:] + (lr * h0i + li * h0r)
        nre_ref[:, st] = hr
        nim_ref[:, st] = hi
        y_ref[:, cols] = _dot(hr.astype(BF16), cr_ref[c]) - _dot(hi.astype(BF16), ci_ref[c])
    o_ref[...] = _glu_tail(x, u, y_ref[...], d_ref, wglu_ref, bglu_ref, gpost_ref)


def _ssm_sample(x, state_re, state_im, g_pre, g_post, tables, d_skip, w_glu, b_glu):
    w_lag, c_r, c_i, lam1, _ = tables
    n = x.shape[0]
    flat = (n, SSM_GROUPS * SSM_STATE)
    st = jax.ShapeDtypeStruct(flat, F32)
    out, nre, nim = pl.pallas_call(
        _ssm_sample_kernel,
        out_shape=[jax.ShapeDtypeStruct(x.shape, F32), st, st],
        scratch_shapes=[pltpu.VMEM((n, D_MODEL), F32)],
        compiler_params=pltpu.CompilerParams(vmem_limit_bytes=VMEM_LIMIT_BYTES),
        name="ssm_sample",
    )(x, state_re.reshape(flat), state_im.reshape(flat), g_pre, g_post, w_lag[:, :LANES, :], c_r, c_i, lam1,
      d_skip, w_glu, b_glu)
    shape = (n, SSM_GROUPS, SSM_STATE)
    return out, nre.reshape(shape), nim.reshape(shape)


def kernel(x_prompt, x_sample, mem_prompt, state_ssm_re, state_ssm_im, cache_win_k, cache_win_v, cache_mem_k, cache_mem_v, norm_g, mem_norm_g, ffn_w_in, ffn_w_out, ssm_a_re, ssm_a_im, ssm_log_dt, ssm_b_re, ssm_b_im, ssm_c_re, ssm_c_im, ssm_d, ssm_w_glu, ssm_b_glu, attn_w_qkv, attn_b_qkv, attn_w_o, attn_sinks, ca_w_q, ca_w_kv, ca_w_o):
    bp, seq, _ = x_prompt.shape
    bs = x_sample.shape[0]
    xp = x_prompt.reshape(bp * seq, D_MODEL)
    xs = x_sample.reshape(bs, D_MODEL)

    gain = lambda i, r: norm_g[i, r].astype(F32).reshape(1, D_MODEL)
    ffn_w_in_b, ffn_w_out_b = ffn_w_in.astype(BF16), ffn_w_out.astype(BF16)
    ssm_w_glu_b = ssm_w_glu.astype(BF16)
    attn_w_qkv_b, attn_w_o_b = attn_w_qkv.astype(BF16), attn_w_o.astype(BF16)
    ca_w_q_b, ca_w_o_b = ca_w_q.astype(BF16), ca_w_o.astype(BF16)

    mem_k, mem_v = _mem_kv(mem_prompt, mem_norm_g.astype(F32), ca_w_kv.astype(BF16))

    ssm_re_p, ssm_im_p, ssm_re_s, ssm_im_s = [], [], [], []
    wk_p, wv_p, wk_s, wv_s = [], [], [], []
    for i in range(DEPTH):
        li = i // N_MIXERS
        xp = _half_ffn(xp, gain(i, 0), gain(i, 1), ffn_w_in_b[i, 0], ffn_w_out_b[i, 0], FFN_TOKENS)
        xs = _half_ffn(xs, gain(i, 0), gain(i, 1), ffn_w_in_b[i, 0], ffn_w_out_b[i, 0], FFN_TOKENS)
        if i % N_MIXERS == 0:
            tables = _ssm_tables(ssm_a_re[li], ssm_a_im[li], ssm_log_dt[li], ssm_b_re[li], ssm_b_im[li],
                                 ssm_c_re[li], ssm_c_im[li])
            d_skip = ssm_d[li].astype(F32).reshape(1, D_MODEL)
            b_glu = ssm_b_glu[li].astype(F32).reshape(1, 2 * D_MODEL)
            xp, hr_p, hi_p = _ssm_prompt(xp, gain(i, 2), gain(i, 3), tables, d_skip, ssm_w_glu_b[li], b_glu,
                                         bp, seq)
            xs, hr_s, hi_s = _ssm_sample(xs, state_ssm_re[li], state_ssm_im[li], gain(i, 2), gain(i, 3), tables,
                                         d_skip, ssm_w_glu_b[li], b_glu)
            ssm_re_p.append(hr_p); ssm_im_p.append(hi_p)
            ssm_re_s.append(hr_s); ssm_im_s.append(hi_s)
        else:
            b_qkv = attn_b_qkv[li].astype(F32).reshape(1, QKV_WIDTH)
            sinks = attn_sinks[li].astype(F32)
            xp, bk_p, bv_p = _window_attention_prompt(xp, gain(i, 2), gain(i, 3), attn_w_qkv_b[li], b_qkv,
                                                      attn_w_o_b[li], sinks, bp, seq)
            xs, bk_s, bv_s = _window_attention_sample(
                xs, gain(i, 2), gain(i, 3), attn_w_qkv_b[li], b_qkv, attn_w_o_b[li], sinks,
                cache_win_k[li].reshape(bs, WINDOW, KV_WIDTH), cache_win_v[li].reshape(bs, WINDOW, KV_WIDTH))
            win = lambda a: a.reshape(a.shape[0], WINDOW, N_KV_HEADS, HEAD_DIM)
            wk_p.append(win(bk_p)); wv_p.append(win(bv_p))
            wk_s.append(win(bk_s)); wv_s.append(win(bv_s))
        xp = _cross_attention_prompt(xp, gain(i, 4), gain(i, 5), ca_w_q_b[i], ca_w_o_b[i], mem_k[i], mem_v[i], seq)
        xs = _cross_attention_sample(xs, gain(i, 4), gain(i, 5), ca_w_q_b[i], ca_w_o_b[i],
                                     cache_mem_k[i].reshape(bs, N_MEM, D_MODEL),
                                     cache_mem_v[i].reshape(bs, N_MEM, D_MODEL))
        xp = _half_ffn(xp, gain(i, 6), gain(i, 7), ffn_w_in_b[i, 1], ffn_w_out_b[i, 1], FFN_TOKENS)
        xs = _half_ffn(xs, gain(i, 6), gain(i, 7), ffn_w_in_b[i, 1], ffn_w_out_b[i, 1], FFN_TOKENS)

    mem_shape = (DEPTH, bp, N_MEM, CA_HEADS, CA_HEAD_DIM)
    return (xp.reshape(bp, seq, D_MODEL), xs.reshape(bs, 1, D_MODEL),
            jnp.stack(ssm_re_p), jnp.stack(ssm_im_p), jnp.stack(wk_p), jnp.stack(wv_p),
            mem_k.reshape(mem_shape), mem_v.reshape(mem_shape),
            jnp.stack(ssm_re_s), jnp.stack(ssm_im_s), jnp.stack(wk_s), jnp.stack(wv_s))
```

```python
import functools
import math

import jax
import jax.numpy as jnp
from jax import lax
from jax.experimental import pallas as pl
from jax.experimental.pallas import tpu as pltpu

F32 = jnp.float32
BF16 = jnp.bfloat16

D_MODEL = 1024
DEPTH = 4
N_MIXERS = 2
SSM_GROUP = 16
SSM_GROUPS = D_MODEL // SSM_GROUP
SSM_STATE = 64
HEAD_DIM = 64
N_HEADS = D_MODEL // HEAD_DIM
N_KV_HEADS = 4
GQA = N_HEADS // N_KV_HEADS
WINDOW = 128
KV_WIDTH = N_KV_HEADS * HEAD_DIM
QKV_WIDTH = (N_HEADS + 2 * N_KV_HEADS) * HEAD_DIM
N_MEM = 256
CA_HEADS = 4
CA_HEAD_DIM = D_MODEL // CA_HEADS
D_FF = ((8 * D_MODEL // 3 + 127) // 128) * 128
FFN_RES = 0.5
EPS = 1e-6
NEG = -1e30

SUBLANES = 8
LANES = 128
VMEM_LIMIT_BYTES = 56 * 1024 * 1024

SSM_LAGS = SUBLANES
SSM_TILE_GROUPS = LANES // SSM_GROUP
SSM_TILES = SSM_GROUPS // SSM_TILE_GROUPS
SSM_TILE_STATE = SSM_TILE_GROUPS * SSM_STATE

FFN_CHUNK = 256
FFN_TOKENS = 1024
CA_TOKENS = 512
SWA_TOKENS = 512
SSM_TOKENS = 512
SAMPLE_CA_BLOCK = 4
SAMPLE_SWA_BLOCK = 16


def _params(*sem):
    return pltpu.CompilerParams(dimension_semantics=sem, vmem_limit_bytes=VMEM_LIMIT_BYTES)


def _rms(x, g):
    r = lax.rsqrt(jnp.mean(x * x, axis=-1, keepdims=True) + EPS)
    return x * r * g


def _dot(a, b):
    return jnp.dot(a, b, preferred_element_type=F32)


def _dot_nt(a, b):
    return lax.dot_general(a, b, (((1,), (1,)), ((), ())), preferred_element_type=F32)


def _const_spec(shape):
    zeros = (0,) * len(shape)
    return pl.BlockSpec(shape, lambda *_: zeros)


def _ffn_kernel(x_ref, gpre_ref, gpost_ref, wg_ref, wu_ref, wo_ref, o_ref, xn_ref, acc_ref):
    c = pl.program_id(1)

    @pl.when(c == 0)
    def _():
        xn_ref[...] = _rms(x_ref[...], gpre_ref[...]).astype(BF16)
        acc_ref[...] = jnp.zeros_like(acc_ref)

    xn = xn_ref[...]
    gate = _dot(xn, wg_ref[...])
    up = _dot(xn, wu_ref[...])
    h = (gate * jax.nn.sigmoid(gate) * up).astype(BF16)
    acc_ref[...] += _dot(h, wo_ref[...])

    @pl.when(c == pl.num_programs(1) - 1)
    def _():
        o_ref[...] = x_ref[...] + FFN_RES * _rms(acc_ref[...], gpost_ref[...])


def _half_ffn(x, g_pre, g_post, w_in, w_out, layer, half, tokens):
    n = x.shape[0]
    tm = min(tokens, n)
    n_chunks = D_FF // FFN_CHUNK
    return pl.pallas_call(
        _ffn_kernel,
        grid=(n // tm, n_chunks),
        in_specs=[
            pl.BlockSpec((tm, D_MODEL), lambda i, c: (i, 0)),
            _const_spec((1, D_MODEL)),
            _const_spec((1, D_MODEL)),
            pl.BlockSpec((None, None, D_MODEL, FFN_CHUNK), lambda i, c: (layer, half, 0, c)),
            pl.BlockSpec((None, None, D_MODEL, FFN_CHUNK), lambda i, c: (layer, half, 0, c + n_chunks)),
            pl.BlockSpec((None, None, FFN_CHUNK, D_MODEL), lambda i, c: (layer, half, c, 0)),
        ],
        out_specs=pl.BlockSpec((tm, D_MODEL), lambda i, c: (i, 0)),
        out_shape=jax.ShapeDtypeStruct((n, D_MODEL), F32),
        scratch_shapes=[pltpu.VMEM((tm, D_MODEL), BF16), pltpu.VMEM((tm, D_MODEL), F32)],
        compiler_params=_params("parallel", "arbitrary"),
        name="half_ffn",
    )(x, g_pre, g_post, w_in, w_in, w_out)


def _mem_kv_kernel(mem_ref, g_ref, w_ref, k_ref, v_ref):
    mn = _rms(mem_ref[...], g_ref[...]).astype(BF16)
    kv = _dot(mn, w_ref[...])
    k_ref[...] = kv[:, :D_MODEL]
    v_ref[...] = kv[:, D_MODEL:]


def _mem_kv(mem, g_mem, w_kv):
    bsz = mem.shape[0]
    out = jax.ShapeDtypeStruct((DEPTH, bsz, N_MEM, D_MODEL), F32)
    out_spec = pl.BlockSpec((None, None, N_MEM, D_MODEL), lambda l, b: (l, b, 0, 0))
    return pl.pallas_call(
        _mem_kv_kernel,
        grid=(DEPTH, bsz),
        in_specs=[
            pl.BlockSpec((None, N_MEM, D_MODEL), lambda l, b: (b, 0, 0)),
            pl.BlockSpec((None, 1, D_MODEL), lambda l, b: (l, 0, 0)),
            pl.BlockSpec((None, D_MODEL, 2 * D_MODEL), lambda l, b: (l, 0, 0)),
        ],
        out_specs=[out_spec, out_spec],
        out_shape=[out, out],
        compiler_params=_params("parallel", "parallel"),
        name="mem_kv",
    )(mem, g_mem.reshape(DEPTH, 1, D_MODEL), w_kv)


def _ca_kernel(x_ref, gpre_ref, gpost_ref, wq_ref, wo_ref, mk_ref, mv_ref, o_ref):
    x = x_ref[...]
    xn = _rms(x, gpre_ref[...]).astype(BF16)
    q = _dot(xn, wq_ref[...])
    heads = []
    for h in range(CA_HEADS):
        cols = slice(h * CA_HEAD_DIM, (h + 1) * CA_HEAD_DIM)
        s = _dot_nt(q[:, cols].astype(BF16), mk_ref[:, cols].astype(BF16)) * (CA_HEAD_DIM ** -0.5)
        p = jnp.exp(s - jnp.max(s, axis=-1, keepdims=True))
        den = jnp.sum(p, axis=-1, keepdims=True)
        heads.append(_dot(p.astype(BF16), mv_ref[:, cols].astype(BF16)) / den)
    o = jnp.concatenate(heads, axis=-1).astype(BF16)
    o_ref[...] = x + _rms(_dot(o, wo_ref[...]), gpost_ref[...])


def _cross_attention_prompt(x, g_pre, g_post, w_q, w_o, mk, mv, layer, seq):
    n = x.shape[0]
    tm = CA_TOKENS
    per_seq = seq // tm
    mem_spec = pl.BlockSpec((None, None, N_MEM, D_MODEL), lambda i: (layer, i // per_seq, 0, 0))
    w_spec = pl.BlockSpec((None, D_MODEL, D_MODEL), lambda i: (layer, 0, 0))
    return pl.pallas_call(
        _ca_kernel,
        grid=(n // tm,),
        in_specs=[
            pl.BlockSpec((tm, D_MODEL), lambda i: (i, 0)),
            _const_spec((1, D_MODEL)),
            _const_spec((1, D_MODEL)),
            w_spec,
            w_spec,
            mem_spec,
            mem_spec,
        ],
        out_specs=pl.BlockSpec((tm, D_MODEL), lambda i: (i, 0)),
        out_shape=jax.ShapeDtypeStruct((n, D_MODEL), F32),
        compiler_params=_params("parallel"),
        name="cross_attn_prompt",
    )(x, g_pre, g_post, w_q, w_o, mk, mv)


def _stack_rows(pieces):
    n, w = len(pieces), pieces[0].shape[1]
    sub = lax.broadcasted_iota(jnp.int32, (n, w), 0)
    out = jnp.broadcast_to(pieces[0], (n, w))
    for r in range(1, n):
        out = jnp.where(sub == r, jnp.broadcast_to(pieces[r], (n, w)), out)
    return out


def _ca_sample_kernel(x_ref, gpre_ref, gpost_ref, wq_ref, wo_ref, mk_ref, mv_ref, o_ref, q_ref, att_ref):
    i = pl.program_id(0)

    @pl.when(i == 0)
    def _():
        xn = _rms(x_ref[...], gpre_ref[...]).astype(BF16)
        q_ref[...] = _dot(xn, wq_ref[...]) * (CA_HEAD_DIM ** -0.5)

    for b in range(SAMPLE_CA_BLOCK):
        row = i * SAMPLE_CA_BLOCK + b
        q = q_ref[pl.ds(row, 1), :]
        q4 = _stack_rows([q[:, h * CA_HEAD_DIM:(h + 1) * CA_HEAD_DIM] for h in range(CA_HEADS)])
        s = jnp.sum(mk_ref[b] * q4[None], axis=-1, keepdims=True)
        p = jnp.exp(s - jnp.max(s, axis=0, keepdims=True))
        den = jnp.sum(p, axis=0)
        o4 = jnp.sum(p * mv_ref[b], axis=0) / den
        att_ref[pl.ds(row, 1), :] = jnp.concatenate([o4[h:h + 1, :] for h in range(CA_HEADS)], axis=-1)

    @pl.when(i == pl.num_programs(0) - 1)
    def _():
        y = _dot(att_ref[...].astype(BF16), wo_ref[...])
        o_ref[...] = x_ref[...] + _rms(y, gpost_ref[...])


def _cross_attention_sample(x, g_pre, g_post, w_q, w_o, mk, mv, layer):
    n = x.shape[0]
    mem_spec = pl.BlockSpec((None, SAMPLE_CA_BLOCK, N_MEM, CA_HEADS, CA_HEAD_DIM), lambda i: (layer, i, 0, 0, 0))
    w_spec = pl.BlockSpec((None, D_MODEL, D_MODEL), lambda i: (layer, 0, 0))
    return pl.pallas_call(
        _ca_sample_kernel,
        grid=(n // SAMPLE_CA_BLOCK,),
        in_specs=[
            _const_spec((n, D_MODEL)),
            _const_spec((1, D_MODEL)),
            _const_spec((1, D_MODEL)),
            w_spec,
            w_spec,
            mem_spec,
            mem_spec,
        ],
        out_specs=_const_spec((n, D_MODEL)),
        out_shape=jax.ShapeDtypeStruct((n, D_MODEL), F32),
        scratch_shapes=[pltpu.VMEM((n, D_MODEL), F32), pltpu.VMEM((n, D_MODEL), F32)],
        compiler_params=_params("arbitrary"),
        name="cross_attn_sample",
    )(x, g_pre, g_post, w_q, w_o, mk, mv)


def _swa_kernel(sink_ref, x_ref, gpre_ref, gpost_ref, wqkv_ref, bqkv_ref, wo_ref,
                o_ref, kout_ref, vout_ref, kprev_ref, vprev_ref, att_ref):
    t = pl.program_id(1)
    tm = x_ref.shape[0]
    x = x_ref[...]
    xn = _rms(x, gpre_ref[...]).astype(BF16)
    qkv = _dot(xn, wqkv_ref[...]) + bqkv_ref[...]
    k = qkv[:, N_HEADS * HEAD_DIM:N_HEADS * HEAD_DIM + KV_WIDTH]
    v = qkv[:, N_HEADS * HEAD_DIM + KV_WIDTH:]

    cur, nxt = t % 2, (t + 1) % 2

    @pl.when(t == 0)
    def _():
        kprev_ref[0] = jnp.zeros((WINDOW, KV_WIDTH), BF16)
        vprev_ref[0] = jnp.zeros((WINDOW, KV_WIDTH), BF16)

    kall = jnp.concatenate([kprev_ref[cur], k.astype(BF16)], axis=0)
    vall = jnp.concatenate([vprev_ref[cur], v.astype(BF16)], axis=0)

    row = lax.broadcasted_iota(jnp.int32, (GQA * WINDOW, 2 * WINDOW), 0)
    qi = row & (WINDOW - 1)
    kj = lax.broadcasted_iota(jnp.int32, (GQA * WINDOW, 2 * WINDOW), 1)
    band = (kj >= qi) & (kj <= qi + WINDOW)
    row_g = lax.broadcasted_iota(jnp.int32, (GQA * WINDOW, 1), 0) // WINDOW

    for n in range(tm // WINDOW):
        rows = slice(n * WINDOW, (n + 1) * WINDOW)
        kk = kall[n * WINDOW:(n + 2) * WINDOW, :]
        vv = vall[n * WINDOW:(n + 2) * WINDOW, :]
        valid = band & (kj >= WINDOW - (t * tm + n * WINDOW)) if n == 0 else band
        for kh in range(N_KV_HEADS):
            kv_cols = slice(kh * HEAD_DIM, (kh + 1) * HEAD_DIM)
            q4 = jnp.concatenate(
                [qkv[rows, (kh * GQA + g) * HEAD_DIM:(kh * GQA + g + 1) * HEAD_DIM] for g in range(GQA)],
                axis=0).astype(BF16)
            sink = jnp.full((GQA * WINDOW, 1), sink_ref[kh * GQA], F32)
            for g in range(1, GQA):
                sink = jnp.where(row_g == g, sink_ref[kh * GQA + g], sink)
            s = _dot_nt(q4, kk[:, kv_cols]) * (HEAD_DIM ** -0.5)
            s = jnp.where(valid, s, NEG)
            m = jnp.maximum(jnp.max(s, axis=-1, keepdims=True), sink)
            p = jnp.exp(s - m)
            den = jnp.sum(p, axis=-1, keepdims=True) + jnp.exp(sink - m)
            o4 = _dot(p.astype(BF16), vv[:, kv_cols]) / den
            for g in range(GQA):
                h = kh * GQA + g
                att_ref[rows, h * HEAD_DIM:(h + 1) * HEAD_DIM] = o4[g * WINDOW:(g + 1) * WINDOW]

    y = _dot(att_ref[...].astype(BF16), wo_ref[...])
    o_ref[...] = x + _rms(y, gpost_ref[...])

    kprev_ref[nxt] = k[tm - WINDOW:, :].astype(BF16)
    vprev_ref[nxt] = v[tm - WINDOW:, :].astype(BF16)

    kout_ref[...] = k[tm - WINDOW:, :]
    vout_ref[...] = v[tm - WINDOW:, :]


def _window_attention_prompt(x, g_pre, g_post, w_qkv, b_qkv, w_o, sinks, layer, bsz, seq):
    tm = SWA_TOKENS
    per_seq = seq // tm
    win_spec = pl.BlockSpec((None, WINDOW, KV_WIDTH), lambda b, t, *_: (b, 0, 0))
    win_shape = jax.ShapeDtypeStruct((bsz, WINDOW, KV_WIDTH), F32)
    grid_spec = pltpu.PrefetchScalarGridSpec(
        num_scalar_prefetch=1,
        grid=(bsz, per_seq),
        in_specs=[
            pl.BlockSpec((tm, D_MODEL), lambda b, t, *_: (b * per_seq + t, 0)),
            _const_spec((1, D_MODEL)),
            _const_spec((1, D_MODEL)),
            pl.BlockSpec((None, D_MODEL, QKV_WIDTH), lambda b, t, *_: (layer, 0, 0)),
            _const_spec((1, QKV_WIDTH)),
            pl.BlockSpec((None, N_HEADS * HEAD_DIM, D_MODEL), lambda b, t, *_: (layer, 0, 0)),
        ],
        out_specs=[pl.BlockSpec((tm, D_MODEL), lambda b, t, *_: (b * per_seq + t, 0)), win_spec, win_spec],
        scratch_shapes=[
            pltpu.VMEM((2, WINDOW, KV_WIDTH), BF16),
            pltpu.VMEM((2, WINDOW, KV_WIDTH), BF16),
            pltpu.VMEM((tm, N_HEADS * HEAD_DIM), F32),
        ],
    )
    return pl.pallas_call(
        _swa_kernel,
        grid_spec=grid_spec,
        out_shape=[jax.ShapeDtypeStruct(x.shape, F32), win_shape, win_shape],
        compiler_params=_params("arbitrary", "arbitrary"),
        name="window_attn_prompt",
    )(sinks, x, g_pre, g_post, w_qkv, b_qkv, w_o)


def _swa_sample_kernel(sink_ref, x_ref, gpre_ref, gpost_ref, wqkv_ref, bqkv_ref, wo_ref, ck_ref, cv_ref,
                       o_ref, nk_ref, nv_ref, qkv_ref, att_ref):
    i = pl.program_id(0)
    k0 = N_HEADS * HEAD_DIM
    v0 = k0 + KV_WIDTH

    @pl.when(i == 0)
    def _():
        xn = _rms(x_ref[...], gpre_ref[...]).astype(BF16)
        qkv_ref[...] = _dot(xn, wqkv_ref[...]) + bqkv_ref[...]

    kv_row = lax.broadcasted_iota(jnp.int32, (N_KV_HEADS, 1), 0)
    for b in range(SAMPLE_SWA_BLOCK):
        row = i * SAMPLE_SWA_BLOCK + b
        qkv = qkv_ref[pl.ds(row, 1), :]
        head = lambda c0, h: qkv[:, c0 + h * HEAD_DIM:c0 + (h + 1) * HEAD_DIM]
        kc = ck_ref[b]
        vc = cv_ref[b]
        k_new = _stack_rows([head(k0, kh) for kh in range(N_KV_HEADS)])
        v_new = _stack_rows([head(v0, kh) for kh in range(N_KV_HEADS)])
        outs = []
        for g in range(GQA):
            q = _stack_rows([head(0, kh * GQA + g) for kh in range(N_KV_HEADS)]) * (HEAD_DIM ** -0.5)
            sink = jnp.full((N_KV_HEADS, 1), sink_ref[g], F32)
            for kh in range(1, N_KV_HEADS):
                sink = jnp.where(kv_row == kh, sink_ref[kh * GQA + g], sink)
            s = jnp.sum(kc * q[None], axis=-1, keepdims=True)
            s_new = jnp.sum(k_new * q, axis=-1, keepdims=True)
            m = jnp.maximum(jnp.maximum(jnp.max(s, axis=0), s_new), sink)
            p = jnp.exp(s - m[None])
            p_new = jnp.exp(s_new - m)
            den = jnp.sum(p, axis=0) + p_new + jnp.exp(sink - m)
            outs.append((jnp.sum(p * vc, axis=0) + p_new * v_new) / den)
        att_ref[pl.ds(row, 1), :] = jnp.concatenate(
            [outs[h % GQA][h // GQA:h // GQA + 1, :] for h in range(N_HEADS)], axis=-1)
        nk_ref[b, 0:WINDOW - 1] = kc[1:]
        nv_ref[b, 0:WINDOW - 1] = vc[1:]
        nk_ref[b, WINDOW - 1] = k_new
        nv_ref[b, WINDOW - 1] = v_new

    @pl.when(i == pl.num_programs(0) - 1)
    def _():
        y = _dot(att_ref[...].astype(BF16), wo_ref[...])
        o_ref[...] = x_ref[...] + _rms(y, gpost_ref[...])


def _window_attention_sample(x, g_pre, g_post, w_qkv, b_qkv, w_o, sinks, cache_k, cache_v, layer):
    n = x.shape[0]
    blk = SAMPLE_SWA_BLOCK
    cache_spec = pl.BlockSpec((None, blk, WINDOW, N_KV_HEADS, HEAD_DIM), lambda i, *_: (layer, i, 0, 0, 0))
    win_spec = pl.BlockSpec((blk, WINDOW, N_KV_HEADS, HEAD_DIM), lambda i, *_: (i, 0, 0, 0))
    win_shape = jax.ShapeDtypeStruct((n, WINDOW, N_KV_HEADS, HEAD_DIM), F32)
    grid_spec = pltpu.PrefetchScalarGridSpec(
        num_scalar_prefetch=1,
        grid=(n // blk,),
        in_specs=[
            _const_spec((n, D_MODEL)),
            _const_spec((1, D_MODEL)),
            _const_spec((1, D_MODEL)),
            pl.BlockSpec((None, D_MODEL, QKV_WIDTH), lambda i, *_: (layer, 0, 0)),
            _const_spec((1, QKV_WIDTH)),
            pl.BlockSpec((None, N_HEADS * HEAD_DIM, D_MODEL), lambda i, *_: (layer, 0, 0)),
            cache_spec,
            cache_spec,
        ],
        out_specs=[_const_spec((n, D_MODEL)), win_spec, win_spec],
        scratch_shapes=[pltpu.VMEM((n, QKV_WIDTH), F32), pltpu.VMEM((n, N_HEADS * HEAD_DIM), F32)],
    )
    return pl.pallas_call(
        _swa_sample_kernel,
        grid_spec=grid_spec,
        out_shape=[jax.ShapeDtypeStruct(x.shape, F32), win_shape, win_shape],
        compiler_params=_params("arbitrary"),
        name="window_attn_sample",
    )(sinks, x, g_pre, g_post, w_qkv, b_qkv, w_o, cache_k, cache_v)


def _ssm_prep_kernel(ar_ref, ai_ref, dt_ref, br_ref, bi_ref, wr_ref, wi_ref, lam_ref):
    ar, ai, dt = ar_ref[...], ai_ref[...], jnp.exp(dt_ref[...])
    mag = jnp.exp(ar * dt)
    lr, li = mag * jnp.cos(ai * dt), mag * jnp.sin(ai * dt)
    den = ar * ar + ai * ai
    nr, ni = lr - 1.0, li
    zr = (nr * ar + ni * ai) / den
    zi = (ni * ar - nr * ai) / den
    br, bi = br_ref[...], bi_ref[...]
    wr = zr * br - zi * bi
    wi = zr * bi + zi * br
    pr, pi = lr, li
    for k in range(SSM_LAGS):
        wr_ref[k] = wr
        wi_ref[k] = wi
        wr, wi = lr * wr - li * wi, lr * wi + li * wr
        if k > 0:
            pr, pi = lr * pr - li * pi, lr * pi + li * pr
    lam_ref[0] = lr
    lam_ref[1] = li
    lam_ref[2] = pr
    lam_ref[3] = pi


def _ssm_tables(a_re, a_im, log_dt, b_re, b_im, c_re, c_im):
    G, P, GS, R, TG, NT = SSM_GROUPS, SSM_STATE, SSM_GROUP, SSM_LAGS, SSM_TILE_GROUPS, SSM_TILES
    rows = G * P
    dense = (rows * GS // LANES, LANES)
    spread = lambda a: jnp.broadcast_to(a.astype(F32)[:, :, None], (G, P, GS)).reshape(dense)
    log_dt = jnp.broadcast_to(log_dt.astype(F32)[:, None], (G, P))
    wr, wi, lam = pl.pallas_call(
        _ssm_prep_kernel,
        out_shape=[jax.ShapeDtypeStruct((R,) + dense, F32), jax.ShapeDtypeStruct((R,) + dense, F32),
                   jax.ShapeDtypeStruct((4,) + dense, F32)],
        name="ssm_prep",
    )(spread(a_re), spread(a_im), spread(log_dt), b_re.astype(F32).reshape(dense), b_im.astype(F32).reshape(dense))
    wr, wi = wr.reshape(R, rows, GS), wi.reshape(R, rows, GS)
    lam = lam.reshape(4, rows, GS)[:, :, 0]
    eye = jnp.eye(TG, dtype=F32)
    w = jnp.stack([wr, wi]).reshape(2, R, NT, TG, P, GS).transpose(2, 1, 3, 5, 0, 4)
    w_lag = (w[:, :, :, :, :, None, :] * eye[None, None, :, None, None, :, None]).reshape(
        NT, R * LANES, 2 * SSM_TILE_STATE).astype(BF16)

    def c_blocks(cm):
        ct = cm.astype(F32).reshape(NT, TG, GS, P).transpose(0, 1, 3, 2)
        return (ct[:, :, :, None, :] * eye[None, :, None, :, None]).reshape(NT, SSM_TILE_STATE, LANES).astype(BF16)

    def lam_rows(re, im):
        return jnp.concatenate([re.reshape(NT, 1, SSM_TILE_STATE), im.reshape(NT, 1, SSM_TILE_STATE)], axis=-1)

    return (w_lag, c_blocks(c_re), c_blocks(c_im), lam_rows(lam[0], lam[1]), lam_rows(lam[2], lam[3]))


def _glu_tail(x, u, y, d_ref, wglu_ref, bglu_ref, gpost_ref):
    y = y + d_ref[...] * u
    y = 0.5 * y * (1.0 + lax.erf(y * (2.0 ** -0.5)))
    z = _dot(y.astype(BF16), wglu_ref[...]) + bglu_ref[...]
    out = z[:, :D_MODEL] * jax.nn.sigmoid(z[:, D_MODEL:])
    return x + _rms(out, gpost_ref[...])


def _ssm_kernel(x_ref, gpre_ref, gpost_ref, wlag_ref, cr_ref, ci_ref, lamk_ref, d_ref, wglu_ref, bglu_ref,
                o_ref, hout_ref, ubuf_ref, uprev_ref, h2_ref, y_ref, carry_ref):
    t = pl.program_id(1)
    tm = x_ref.shape[0]
    S = SSM_TILE_STATE
    cur, nxt = t % 2, (t + 1) % 2

    @pl.when(t == 0)
    def _():
        uprev_ref[0] = jnp.zeros((SSM_LAGS, D_MODEL), F32)
        carry_ref[...] = jnp.zeros_like(carry_ref)

    x = x_ref[...]
    u = _rms(x, gpre_ref[...])
    ubuf_ref[0:SSM_LAGS, :] = uprev_ref[cur]
    ubuf_ref[SSM_LAGS:, :] = u

    for c in range(SSM_TILES):
        cols = slice(c * LANES, (c + 1) * LANES)
        h_ref = h2_ref.at[c % 2]
        lhs = jnp.concatenate(
            [ubuf_ref[SSM_LAGS - k:SSM_LAGS - k + tm, cols] for k in range(SSM_LAGS)], axis=1).astype(BF16)
        h_ref[...] = _dot(lhs, wlag_ref[c])
        lam = lamk_ref[c]
        lr = jnp.broadcast_to(lam[:, :S], (SUBLANES, S))
        li = jnp.broadcast_to(lam[:, S:], (SUBLANES, S))

        def slab(m, carry):
            r0 = pl.multiple_of(m * SUBLANES, SUBLANES)
            hw = h_ref[pl.ds(r0, SUBLANES), :]
            cr, ci = carry[:, :S], carry[:, S:]
            new = jnp.concatenate([hw[:, :S] + (lr * cr - li * ci), hw[:, S:] + (lr * ci + li * cr)], axis=1)
            h_ref[pl.ds(r0, SUBLANES), :] = new
            return new

        carry_ref[c] = lax.fori_loop(0, tm // SUBLANES, slab, carry_ref[c])
        y_ref[:, cols] = (_dot(h_ref[:, :S].astype(BF16), cr_ref[c])
                          - _dot(h_ref[:, S:].astype(BF16), ci_ref[c]))

    o_ref[...] = _glu_tail(x, u, y_ref[...], d_ref, wglu_ref, bglu_ref, gpost_ref)
    uprev_ref[nxt] = u[tm - SSM_LAGS:, :]

    @pl.when(t == pl.num_programs(1) - 1)
    def _():
        hout_ref[...] = carry_ref[...]


def _ssm_prompt(x, g_pre, g_post, tables, d_skip, w_glu, b_glu, layer, bsz, seq):
    w_lag, c_r, c_i, _, lamk = tables
    tm = SSM_TOKENS
    per_seq = seq // tm
    S2 = 2 * SSM_TILE_STATE
    once = pl.Buffered(1)
    out, h_last = pl.pallas_call(
        _ssm_kernel,
        grid=(bsz, per_seq),
        in_specs=[
            pl.BlockSpec((tm, D_MODEL), lambda b, t: (b * per_seq + t, 0)),
            _const_spec((1, D_MODEL)),
            _const_spec((1, D_MODEL)),
            pl.BlockSpec(w_lag.shape, lambda b, t: (0, 0, 0), pipeline_mode=once),
            pl.BlockSpec(c_r.shape, lambda b, t: (0, 0, 0), pipeline_mode=once),
            pl.BlockSpec(c_i.shape, lambda b, t: (0, 0, 0), pipeline_mode=once),
            _const_spec(lamk.shape),
            _const_spec((1, D_MODEL)),
            pl.BlockSpec((None, D_MODEL, 2 * D_MODEL), lambda b, t: (layer, 0, 0), pipeline_mode=once),
            _const_spec((1, 2 * D_MODEL)),
        ],
        out_specs=[
            pl.BlockSpec((tm, D_MODEL), lambda b, t: (b * per_seq + t, 0)),
            pl.BlockSpec((None, SSM_TILES, SUBLANES, S2), lambda b, t: (b, 0, 0, 0)),
        ],
        out_shape=[jax.ShapeDtypeStruct(x.shape, F32),
                   jax.ShapeDtypeStruct((bsz, SSM_TILES, SUBLANES, S2), F32)],
        scratch_shapes=[
            pltpu.VMEM((tm + SSM_LAGS, D_MODEL), F32),
            pltpu.VMEM((2, SSM_LAGS, D_MODEL), F32),
            pltpu.VMEM((2, tm, S2), F32),
            pltpu.VMEM((tm, D_MODEL), F32),
            pltpu.VMEM((SSM_TILES, SUBLANES, S2), F32),
        ],
        compiler_params=_params("arbitrary", "arbitrary"),
        name="ssm_prompt",
    )(x, g_pre, g_post, w_lag, c_r, c_i, lamk, d_skip, w_glu, b_glu)
    h_last = h_last[:, :, SUBLANES - 1, :]
    shape = (bsz, SSM_GROUPS, SSM_STATE)
    return out, h_last[..., :SSM_TILE_STATE].reshape(shape), h_last[..., SSM_TILE_STATE:].reshape(shape)


def _ssm_sample_kernel(x_ref, sre_ref, sim_ref, gpre_ref, gpost_ref, w0_ref, cr_ref, ci_ref, lam1_ref,
                       d_ref, wglu_ref, bglu_ref, o_ref, nre_ref, nim_ref, y_ref):
    S = SSM_TILE_STATE
    x = x_ref[...]
    u = _rms(x, gpre_ref[...])
    for c in range(SSM_TILES):
        cols = slice(c * LANES, (c + 1) * LANES)
        st = slice(c * S, (c + 1) * S)
        bu = _dot(u[:, cols].astype(BF16), w0_ref[c])
        lam = lam1_ref[c]
        lr, li = lam[:, :S], lam[:, S:]
        h0r, h0i = sre_ref[:, st], sim_ref[:, st]
        hr = bu[:, :S] + (lr * h0r - li * h0i)
        hi = bu[:, S:] + (lr * h0i + li * h0r)
        nre_ref[:, st] = hr
        nim_ref[:, st] = hi
        y_ref[:, cols] = _dot(hr.astype(BF16), cr_ref[c]) - _dot(hi.astype(BF16), ci_ref[c])
    o_ref[...] = _glu_tail(x, u, y_ref[...], d_ref, wglu_ref, bglu_ref, gpost_ref)


def _ssm_sample(x, state_re, state_im, g_pre, g_post, tables, d_skip, w_glu, b_glu, layer):
    w_lag, c_r, c_i, lam1, _ = tables
    n = x.shape[0]
    flat = (n, SSM_GROUPS * SSM_STATE)
    st = jax.ShapeDtypeStruct(flat, F32)
    S2 = 2 * SSM_TILE_STATE
    out, nre, nim = pl.pallas_call(
        _ssm_sample_kernel,
        grid=(1,),
        in_specs=[
            _const_spec((n, D_MODEL)),
            _const_spec(flat),
            _const_spec(flat),
            _const_spec((1, D_MODEL)),
            _const_spec((1, D_MODEL)),
            _const_spec((SSM_TILES, LANES, S2)),
            _const_spec(c_r.shape),
            _const_spec(c_i.shape),
            _const_spec(lam1.shape),
            _const_spec((1, D_MODEL)),
            pl.BlockSpec((None, D_MODEL, 2 * D_MODEL), lambda i: (layer, 0, 0)),
            _const_spec((1, 2 * D_MODEL)),
        ],
        out_specs=[_const_spec((n, D_MODEL)), _const_spec(flat), _const_spec(flat)],
        out_shape=[jax.ShapeDtypeStruct(x.shape, F32), st, st],
        scratch_shapes=[pltpu.VMEM((n, D_MODEL), F32)],
        compiler_params=_params("arbitrary"),
        name="ssm_sample",
    )(x, state_re.reshape(flat), state_im.reshape(flat), g_pre, g_post, w_lag, c_r, c_i, lam1,
      d_skip, w_glu, b_glu)
    shape = (n, SSM_GROUPS, SSM_STATE)
    return out, nre.reshape(shape), nim.reshape(shape)


def kernel(x_prompt, x_sample, mem_prompt, state_ssm_re, state_ssm_im, cache_win_k, cache_win_v, cache_mem_k, cache_mem_v, norm_g, mem_norm_g, ffn_w_in, ffn_w_out, ssm_a_re, ssm_a_im, ssm_log_dt, ssm_b_re, ssm_b_im, ssm_c_re, ssm_c_im, ssm_d, ssm_w_glu, ssm_b_glu, attn_w_qkv, attn_b_qkv, attn_w_o, attn_sinks, ca_w_q, ca_w_kv, ca_w_o):
    bp, seq, _ = x_prompt.shape
    bs = x_sample.shape[0]
    xp = x_prompt.reshape(bp * seq, D_MODEL)
    xs = x_sample.reshape(bs, D_MODEL)

    gain = lambda i, r: norm_g[i, r].astype(F32).reshape(1, D_MODEL)
    ffn_w_in_b, ffn_w_out_b = ffn_w_in.astype(BF16), ffn_w_out.astype(BF16)
    ssm_w_glu_b = ssm_w_glu.astype(BF16)
    attn_w_qkv_b, attn_w_o_b = attn_w_qkv.astype(BF16), attn_w_o.astype(BF16)
    ca_w_q_b, ca_w_o_b = ca_w_q.astype(BF16), ca_w_o.astype(BF16)

    mem_k, mem_v = _mem_kv(mem_prompt, mem_norm_g.astype(F32), ca_w_kv.astype(BF16))

    ssm_re_p, ssm_im_p, ssm_re_s, ssm_im_s = [], [], [], []
    wk_p, wv_p, wk_s, wv_s = [], [], [], []
    for i in range(DEPTH):
        li = i // N_MIXERS
        xp = _half_ffn(xp, gain(i, 0), gain(i, 1), ffn_w_in_b, ffn_w_out_b, i, 0, FFN_TOKENS)
        xs = _half_ffn(xs, gain(i, 0), gain(i, 1), ffn_w_in_b, ffn_w_out_b, i, 0, FFN_TOKENS)
        if i % N_MIXERS == 0:
            tables = _ssm_tables(ssm_a_re[li], ssm_a_im[li], ssm_log_dt[li], ssm_b_re[li], ssm_b_im[li],
                                 ssm_c_re[li], ssm_c_im[li])
            d_skip = ssm_d[li].astype(F32).reshape(1, D_MODEL)
            b_glu = ssm_b_glu[li].astype(F32).reshape(1, 2 * D_MODEL)
            xp, hr_p, hi_p = _ssm_prompt(xp, gain(i, 2), gain(i, 3), tables, d_skip, ssm_w_glu_b, b_glu, li,
                                         bp, seq)
            xs, hr_s, hi_s = _ssm_sample(xs, state_ssm_re[li], state_ssm_im[li], gain(i, 2), gain(i, 3), tables,
                                         d_skip, ssm_w_glu_b, b_glu, li)
            ssm_re_p.append(hr_p); ssm_im_p.append(hi_p)
            ssm_re_s.append(hr_s); ssm_im_s.append(hi_s)
        else:
            b_qkv = attn_b_qkv[li].astype(F32).reshape(1, QKV_WIDTH)
            sinks = attn_sinks[li].astype(F32)
            xp, bk_p, bv_p = _window_attention_prompt(xp, gain(i, 2), gain(i, 3), attn_w_qkv_b, b_qkv,
                                                      attn_w_o_b, sinks, li, bp, seq)
            xs, bk_s, bv_s = _window_attention_sample(xs, gain(i, 2), gain(i, 3), attn_w_qkv_b, b_qkv, attn_w_o_b,
                                                      sinks, cache_win_k, cache_win_v, li)
            wk_p.append(bk_p.reshape(bp, WINDOW, N_KV_HEADS, HEAD_DIM))
            wv_p.append(bv_p.reshape(bp, WINDOW, N_KV_HEADS, HEAD_DIM))
            wk_s.append(bk_s); wv_s.append(bv_s)
        xp = _cross_attention_prompt(xp, gain(i, 4), gain(i, 5), ca_w_q_b, ca_w_o_b, mem_k, mem_v, i, seq)
        xs = _cross_attention_sample(xs, gain(i, 4), gain(i, 5), ca_w_q_b, ca_w_o_b, cache_mem_k, cache_mem_v, i)
        xp = _half_ffn(xp, gain(i, 6), gain(i, 7), ffn_w_in_b, ffn_w_out_b, i, 1, FFN_TOKENS)
        xs = _half_ffn(xs, gain(i, 6), gain(i, 7), ffn_w_in_b, ffn_w_out_b, i, 1, FFN_TOKENS)

    mem_shape = (DEPTH, bp, N_MEM, CA_HEADS, CA_HEAD_DIM)
    return (xp.reshape(bp, seq, D_MODEL), xs.reshape(bs, 1, D_MODEL),
            jnp.stack(ssm_re_p), jnp.stack(ssm_im_p), jnp.stack(wk_p), jnp.stack(wv_p),
            mem_k.reshape(mem_shape), mem_v.reshape(mem_shape),
            jnp.stack(ssm_re_s), jnp.stack(ssm_im_s), jnp.stack(wk_s), jnp.stack(wv_s))
```

```python
import functools
import math

import jax
import jax.numpy as jnp
from jax import lax
from jax.experimental import pallas as pl
from jax.experimental.pallas import tpu as pltpu

F32 = jnp.float32
BF16 = jnp.bfloat16

D_MODEL = 1024
DEPTH = 4
N_MIXERS = 2
SSM_GROUP = 16
SSM_GROUPS = D_MODEL // SSM_GROUP
SSM_STATE = 64
HEAD_DIM = 64
N_HEADS = D_MODEL // HEAD_DIM
N_KV_HEADS = 4
GQA = N_HEADS // N_KV_HEADS
WINDOW = 128
KV_WIDTH = N_KV_HEADS * HEAD_DIM
QKV_WIDTH = (N_HEADS + 2 * N_KV_HEADS) * HEAD_DIM
N_MEM = 256
CA_HEADS = 4
CA_HEAD_DIM = D_MODEL // CA_HEADS
D_FF = ((8 * D_MODEL // 3 + 127) // 128) * 128
FFN_RES = 0.5
EPS = 1e-6
NEG = -1e30

SUBLANES = 8
LANES = 128
VMEM_LIMIT_BYTES = 56 * 1024 * 1024

SSM_LAGS = SUBLANES
SSM_TILE_GROUPS = LANES // SSM_GROUP
SSM_TILES = SSM_GROUPS // SSM_TILE_GROUPS
SSM_TILE_STATE = SSM_TILE_GROUPS * SSM_STATE

FFN_CHUNK = 256
FFN_TOKENS = 1024
CA_TOKENS = 512
SWA_TOKENS = 512
SSM_TOKENS = 512
SAMPLE_CA_BLOCK = 4
SAMPLE_SWA_BLOCK = 16


def _params(*sem):
    return pltpu.CompilerParams(dimension_semantics=sem, vmem_limit_bytes=VMEM_LIMIT_BYTES)


def _rms(x, g):
    r = lax.rsqrt(jnp.mean(x * x, axis=-1, keepdims=True) + EPS)
    return x * r * g


def _dot(a, b):
    return jnp.dot(a, b, preferred_element_type=F32)


def _dot_nt(a, b):
    return lax.dot_general(a, b, (((1,), (1,)), ((), ())), preferred_element_type=F32)


def _const_spec(shape):
    zeros = (0,) * len(shape)
    return pl.BlockSpec(shape, lambda *_: zeros)


def _ffn_kernel(x_ref, gpre_ref, gpost_ref, wg_ref, wu_ref, wo_ref, o_ref, xn_ref, acc_ref):
    c = pl.program_id(1)

    @pl.when(c == 0)
    def _():
        xn_ref[...] = _rms(x_ref[...], gpre_ref[...]).astype(BF16)
        acc_ref[...] = jnp.zeros_like(acc_ref)

    xn = xn_ref[...]
    gate = _dot(xn, wg_ref[...])
    up = _dot(xn, wu_ref[...])
    h = (gate * jax.nn.sigmoid(gate) * up).astype(BF16)
    acc_ref[...] += _dot(h, wo_ref[...])

    @pl.when(c == pl.num_programs(1) - 1)
    def _():
        o_ref[...] = x_ref[...] + FFN_RES * _rms(acc_ref[...], gpost_ref[...])


def _half_ffn(x, g_pre, g_post, w_in, w_out, layer, half, tokens):
    n = x.shape[0]
    tm = min(tokens, n)
    n_chunks = D_FF // FFN_CHUNK
    return pl.pallas_call(
        _ffn_kernel,
        grid=(n // tm, n_chunks),
        in_specs=[
            pl.BlockSpec((tm, D_MODEL), lambda i, c: (i, 0)),
            _const_spec((1, D_MODEL)),
            _const_spec((1, D_MODEL)),
            pl.BlockSpec((None, None, D_MODEL, FFN_CHUNK), lambda i, c: (layer, half, 0, c)),
            pl.BlockSpec((None, None, D_MODEL, FFN_CHUNK), lambda i, c: (layer, half, 0, c + n_chunks)),
            pl.BlockSpec((None, None, FFN_CHUNK, D_MODEL), lambda i, c: (layer, half, c, 0)),
        ],
        out_specs=pl.BlockSpec((tm, D_MODEL), lambda i, c: (i, 0)),
        out_shape=jax.ShapeDtypeStruct((n, D_MODEL), F32),
        scratch_shapes=[pltpu.VMEM((tm, D_MODEL), BF16), pltpu.VMEM((tm, D_MODEL), F32)],
        compiler_params=_params("parallel", "arbitrary"),
        name="half_ffn",
    )(x, g_pre, g_post, w_in, w_in, w_out)


def _mem_kv_kernel(mem_ref, g_ref, w_ref, k_ref, v_ref):
    mn = _rms(mem_ref[...], g_ref[...]).astype(BF16)
    kv = _dot(mn, w_ref[...])
    k_ref[...] = kv[:, :D_MODEL]
    v_ref[...] = kv[:, D_MODEL:]


def _mem_kv(mem, g_mem, w_kv):
    bsz = mem.shape[0]
    out = jax.ShapeDtypeStruct((DEPTH, bsz, N_MEM, D_MODEL), F32)
    out_spec = pl.BlockSpec((None, None, N_MEM, D_MODEL), lambda l, b: (l, b, 0, 0))
    return pl.pallas_call(
        _mem_kv_kernel,
        grid=(DEPTH, bsz),
        in_specs=[
            pl.BlockSpec((None, N_MEM, D_MODEL), lambda l, b: (b, 0, 0)),
            pl.BlockSpec((None, 1, D_MODEL), lambda l, b: (l, 0, 0)),
            pl.BlockSpec((None, D_MODEL, 2 * D_MODEL), lambda l, b: (l, 0, 0)),
        ],
        out_specs=[out_spec, out_spec],
        out_shape=[out, out],
        compiler_params=_params("parallel", "parallel"),
        name="mem_kv",
    )(mem, g_mem.reshape(DEPTH, 1, D_MODEL), w_kv)


def _ca_kernel(x_ref, gpre_ref, gpost_ref, wq_ref, wo_ref, mk_ref, mv_ref, o_ref):
    x = x_ref[...]
    xn = _rms(x, gpre_ref[...]).astype(BF16)
    q = _dot(xn, wq_ref[...])
    heads = []
    for h in range(CA_HEADS):
        cols = slice(h * CA_HEAD_DIM, (h + 1) * CA_HEAD_DIM)
        s = _dot_nt(q[:, cols].astype(BF16), mk_ref[:, cols].astype(BF16)) * (CA_HEAD_DIM ** -0.5)
        p = jnp.exp(s - jnp.max(s, axis=-1, keepdims=True))
        den = jnp.sum(p, axis=-1, keepdims=True)
        heads.append(_dot(p.astype(BF16), mv_ref[:, cols].astype(BF16)) / den)
    o = jnp.concatenate(heads, axis=-1).astype(BF16)
    o_ref[...] = x + _rms(_dot(o, wo_ref[...]), gpost_ref[...])


def _cross_attention_prompt(x, g_pre, g_post, w_q, w_o, mk, mv, layer, seq):
    n = x.shape[0]
    tm = CA_TOKENS
    per_seq = seq // tm
    mem_spec = pl.BlockSpec((None, None, N_MEM, D_MODEL), lambda i: (layer, i // per_seq, 0, 0))
    w_spec = pl.BlockSpec((None, D_MODEL, D_MODEL), lambda i: (layer, 0, 0))
    return pl.pallas_call(
        _ca_kernel,
        grid=(n // tm,),
        in_specs=[
            pl.BlockSpec((tm, D_MODEL), lambda i: (i, 0)),
            _const_spec((1, D_MODEL)),
            _const_spec((1, D_MODEL)),
            w_spec,
            w_spec,
            mem_spec,
            mem_spec,
        ],
        out_specs=pl.BlockSpec((tm, D_MODEL), lambda i: (i, 0)),
        out_shape=jax.ShapeDtypeStruct((n, D_MODEL), F32),
        compiler_params=_params("parallel"),
        name="cross_attn_prompt",
    )(x, g_pre, g_post, w_q, w_o, mk, mv)


def _stack_rows(pieces):
    n, w = len(pieces), pieces[0].shape[1]
    sub = lax.broadcasted_iota(jnp.int32, (n, w), 0)
    out = jnp.broadcast_to(pieces[0], (n, w))
    for r in range(1, n):
        out = jnp.where(sub == r, jnp.broadcast_to(pieces[r], (n, w)), out)
    return out


def _ca_sample_kernel(x_ref, gpre_ref, gpost_ref, wq_ref, wo_ref, mk_ref, mv_ref, o_ref, q_ref, att_ref):
    i = pl.program_id(0)

    @pl.when(i == 0)
    def _():
        xn = _rms(x_ref[...], gpre_ref[...]).astype(BF16)
        q_ref[...] = _dot(xn, wq_ref[...]) * (CA_HEAD_DIM ** -0.5)

    for b in range(SAMPLE_CA_BLOCK):
        row = i * SAMPLE_CA_BLOCK + b
        q = q_ref[pl.ds(row, 1), :]
        q4 = _stack_rows([q[:, h * CA_HEAD_DIM:(h + 1) * CA_HEAD_DIM] for h in range(CA_HEADS)])
        s = jnp.sum(mk_ref[b] * q4[None], axis=-1, keepdims=True)
        p = jnp.exp(s - jnp.max(s, axis=0, keepdims=True))
        den = jnp.sum(p, axis=0)
        o4 = jnp.sum(p * mv_ref[b], axis=0) / den
        att_ref[pl.ds(row, 1), :] = jnp.concatenate([o4[h:h + 1, :] for h in range(CA_HEADS)], axis=-1)

    @pl.when(i == pl.num_programs(0) - 1)
    def _():
        y = _dot(att_ref[...].astype(BF16), wo_ref[...])
        o_ref[...] = x_ref[...] + _rms(y, gpost_ref[...])


def _cross_attention_sample(x, g_pre, g_post, w_q, w_o, mk, mv, layer):
    n = x.shape[0]
    mem_spec = pl.BlockSpec((None, SAMPLE_CA_BLOCK, N_MEM, CA_HEADS, CA_HEAD_DIM), lambda i: (layer, i, 0, 0, 0))
    w_spec = pl.BlockSpec((None, D_MODEL, D_MODEL), lambda i: (layer, 0, 0))
    return pl.pallas_call(
        _ca_sample_kernel,
        grid=(n // SAMPLE_CA_BLOCK,),
        in_specs=[
            _const_spec((n, D_MODEL)),
            _const_spec((1, D_MODEL)),
            _const_spec((1, D_MODEL)),
            w_spec,
            w_spec,
            mem_spec,
            mem_spec,
        ],
        out_specs=_const_spec((n, D_MODEL)),
        out_shape=jax.ShapeDtypeStruct((n, D_MODEL), F32),
        scratch_shapes=[pltpu.VMEM((n, D_MODEL), F32), pltpu.VMEM((n, D_MODEL), F32)],
        compiler_params=_params("arbitrary"),
        name="cross_attn_sample",
    )(x, g_pre, g_post, w_q, w_o, mk, mv)


def _swa_kernel(sink_ref, x_ref, gpre_ref, gpost_ref, wqkv_ref, bqkv_ref, wo_ref,
                o_ref, kout_ref, vout_ref, kprev_ref, vprev_ref, att_ref):
    t = pl.program_id(1)
    tm = x_ref.shape[0]
    x = x_ref[...]
    xn = _rms(x, gpre_ref[...]).astype(BF16)
    qkv = _dot(xn, wqkv_ref[...]) + bqkv_ref[...]
    k = qkv[:, N_HEADS * HEAD_DIM:N_HEADS * HEAD_DIM + KV_WIDTH]
    v = qkv[:, N_HEADS * HEAD_DIM + KV_WIDTH:]

    cur, nxt = t % 2, (t + 1) % 2

    @pl.when(t == 0)
    def _():
        kprev_ref[0] = jnp.zeros((WINDOW, KV_WIDTH), BF16)
        vprev_ref[0] = jnp.zeros((WINDOW, KV_WIDTH), BF16)

    kall = jnp.concatenate([kprev_ref[cur], k.astype(BF16)], axis=0)
    vall = jnp.concatenate([vprev_ref[cur], v.astype(BF16)], axis=0)

    row = lax.broadcasted_iota(jnp.int32, (GQA * WINDOW, 2 * WINDOW), 0)
    qi = row & (WINDOW - 1)
    kj = lax.broadcasted_iota(jnp.int32, (GQA * WINDOW, 2 * WINDOW), 1)
    band = (kj >= qi) & (kj <= qi + WINDOW)
    row_g = lax.broadcasted_iota(jnp.int32, (GQA * WINDOW, 1), 0) // WINDOW

    for n in range(tm // WINDOW):
        rows = slice(n * WINDOW, (n + 1) * WINDOW)
        kk = kall[n * WINDOW:(n + 2) * WINDOW, :]
        vv = vall[n * WINDOW:(n + 2) * WINDOW, :]
        valid = band & (kj >= WINDOW - (t * tm + n * WINDOW)) if n == 0 else band
        for kh in range(N_KV_HEADS):
            kv_cols = slice(kh * HEAD_DIM, (kh + 1) * HEAD_DIM)
            q4 = jnp.concatenate(
                [qkv[rows, (kh * GQA + g) * HEAD_DIM:(kh * GQA + g + 1) * HEAD_DIM] for g in range(GQA)],
                axis=0).astype(BF16)
            sink = jnp.full((GQA * WINDOW, 1), sink_ref[kh * GQA], F32)
            for g in range(1, GQA):
                sink = jnp.where(row_g == g, sink_ref[kh * GQA + g], sink)
            s = _dot_nt(q4, kk[:, kv_cols]) * (HEAD_DIM ** -0.5)
            s = jnp.where(valid, s, NEG)
            m = jnp.maximum(jnp.max(s, axis=-1, keepdims=True), sink)
            p = jnp.exp(s - m)
            den = jnp.sum(p, axis=-1, keepdims=True) + jnp.exp(sink - m)
            o4 = _dot(p.astype(BF16), vv[:, kv_cols]) / den
            for g in range(GQA):
                h = kh * GQA + g
                att_ref[rows, h * HEAD_DIM:(h + 1) * HEAD_DIM] = o4[g * WINDOW:(g + 1) * WINDOW]

    y = _dot(att_ref[...].astype(BF16), wo_ref[...])
    o_ref[...] = x + _rms(y, gpost_ref[...])

    kprev_ref[nxt] = k[tm - WINDOW:, :].astype(BF16)
    vprev_ref[nxt] = v[tm - WINDOW:, :].astype(BF16)

    kout_ref[...] = k[tm - WINDOW:, :]
    vout_ref[...] = v[tm - WINDOW:, :]


def _window_attention_prompt(x, g_pre, g_post, w_qkv, b_qkv, w_o, sinks, layer, bsz, seq):
    tm = SWA_TOKENS
    per_seq = seq // tm
    win_spec = pl.BlockSpec((None, WINDOW, KV_WIDTH), lambda b, t, *_: (b, 0, 0))
    win_shape = jax.ShapeDtypeStruct((bsz, WINDOW, KV_WIDTH), F32)
    grid_spec = pltpu.PrefetchScalarGridSpec(
        num_scalar_prefetch=1,
        grid=(bsz, per_seq),
        in_specs=[
            pl.BlockSpec((tm, D_MODEL), lambda b, t, *_: (b * per_seq + t, 0)),
            _const_spec((1, D_MODEL)),
            _const_spec((1, D_MODEL)),
            pl.BlockSpec((None, D_MODEL, QKV_WIDTH), lambda b, t, *_: (layer, 0, 0)),
            _const_spec((1, QKV_WIDTH)),
            pl.BlockSpec((None, N_HEADS * HEAD_DIM, D_MODEL), lambda b, t, *_: (layer, 0, 0)),
        ],
        out_specs=[pl.BlockSpec((tm, D_MODEL), lambda b, t, *_: (b * per_seq + t, 0)), win_spec, win_spec],
        scratch_shapes=[
            pltpu.VMEM((2, WINDOW, KV_WIDTH), BF16),
            pltpu.VMEM((2, WINDOW, KV_WIDTH), BF16),
            pltpu.VMEM((tm, N_HEADS * HEAD_DIM), F32),
        ],
    )
    return pl.pallas_call(
        _swa_kernel,
        grid_spec=grid_spec,
        out_shape=[jax.ShapeDtypeStruct(x.shape, F32), win_shape, win_shape],
        compiler_params=_params("arbitrary", "arbitrary"),
        name="window_attn_prompt",
    )(sinks, x, g_pre, g_post, w_qkv, b_qkv, w_o)


def _swa_sample_kernel(sink_ref, x_ref, gpre_ref, gpost_ref, wqkv_ref, bqkv_ref, wo_ref, ck_ref, cv_ref,
                       o_ref, nk_ref, nv_ref, qkvt_ref, attt_ref, blkt_ref):
    i = pl.program_id(0)
    n = x_ref.shape[0]
    k0 = N_HEADS * HEAD_DIM
    v0 = k0 + KV_WIDTH

    @pl.when(i == 0)
    def _():
        xn = _rms(x_ref[...], gpre_ref[...]).astype(BF16)
        qkvt_ref[...] = (_dot(xn, wqkv_ref[...]) + bqkv_ref[...]).T
        attt_ref[...] = jnp.zeros_like(attt_ref)
        blkt_ref[...] = jnp.zeros_like(blkt_ref)

    base = i * SAMPLE_SWA_BLOCK
    qkvt = pltpu.roll(qkvt_ref[...], (n - base) % n, axis=1)
    newest = lax.broadcasted_iota(jnp.int32, (KV_WIDTH, WINDOW), 1) == WINDOW - 1
    heads3 = lambda a: a.reshape(N_HEADS, HEAD_DIM, a.shape[-1])
    per_q_head = lambda a: jnp.concatenate(
        [a[(h // GQA) * HEAD_DIM:(h // GQA + 1) * HEAD_DIM] for h in range(N_HEADS)], axis=0)
    sink = sink_ref[...]
    for b in range(SAMPLE_SWA_BLOCK):
        col = qkvt[:, b:b + 1]
        q = col[0:k0] * (HEAD_DIM ** -0.5)
        k_new, v_new = col[k0:v0], col[v0:]
        kt = ck_ref[b].reshape(KV_WIDTH, WINDOW)
        vt = cv_ref[b].reshape(KV_WIDTH, WINDOW)
        s = jnp.sum(heads3(per_q_head(kt) * q), axis=1, keepdims=True)
        s_new = jnp.sum(heads3(per_q_head(k_new) * q), axis=1, keepdims=True)
        m = jnp.maximum(jnp.maximum(jnp.max(s, axis=2, keepdims=True), s_new), sink)
        p = jnp.exp(s - m)
        p_new = jnp.exp(s_new - m)
        den = jnp.sum(p, axis=2, keepdims=True) + p_new + jnp.exp(sink - m)
        o = jnp.sum(heads3(per_q_head(vt)) * p, axis=2, keepdims=True)
        o = (o + p_new * heads3(per_q_head(v_new))) / den
        blkt_ref[:, b:b + 1] = o.reshape(N_HEADS * HEAD_DIM, 1)
        nk_ref[b] = jnp.where(newest, k_new, pltpu.roll(kt, WINDOW - 1, axis=1)).reshape(nk_ref.shape[1:])
        nv_ref[b] = jnp.where(newest, v_new, pltpu.roll(vt, WINDOW - 1, axis=1)).reshape(nv_ref.shape[1:])

    lane = lax.broadcasted_iota(jnp.int32, attt_ref.shape, 1)
    mine = (lane >= base) & (lane < base + SAMPLE_SWA_BLOCK)
    attt_ref[...] = jnp.where(mine, pltpu.roll(blkt_ref[...], base, axis=1), attt_ref[...])

    @pl.when(i == pl.num_programs(0) - 1)
    def _():
        y = _dot(attt_ref[...].T.astype(BF16), wo_ref[...])
        o_ref[...] = x_ref[...] + _rms(y, gpost_ref[...])


def _window_attention_sample(x, g_pre, g_post, w_qkv, b_qkv, w_o, sinks, cache_k, cache_v, layer):
    n = x.shape[0]
    assert n == LANES, "the sample kernel keeps one sample per lane"
    blk = SAMPLE_SWA_BLOCK
    cache_spec = pl.BlockSpec((None, blk, N_KV_HEADS, HEAD_DIM, WINDOW), lambda i, *_: (layer, i, 0, 0, 0))
    win_spec = pl.BlockSpec((blk, N_KV_HEADS, HEAD_DIM, WINDOW), lambda i, *_: (i, 0, 0, 0))
    win_shape = jax.ShapeDtypeStruct((n, N_KV_HEADS, HEAD_DIM, WINDOW), F32)
    return pl.pallas_call(
        _swa_sample_kernel,
        grid=(n // blk,),
        in_specs=[
            _const_spec((N_HEADS, 1, 1)),
            _const_spec((n, D_MODEL)),
            _const_spec((1, D_MODEL)),
            _const_spec((1, D_MODEL)),
            pl.BlockSpec((None, D_MODEL, QKV_WIDTH), lambda i, *_: (layer, 0, 0)),
            _const_spec((1, QKV_WIDTH)),
            pl.BlockSpec((None, N_HEADS * HEAD_DIM, D_MODEL), lambda i, *_: (layer, 0, 0)),
            cache_spec,
            cache_spec,
        ],
        out_specs=[_const_spec((n, D_MODEL)), win_spec, win_spec],
        out_shape=[jax.ShapeDtypeStruct(x.shape, F32), win_shape, win_shape],
        scratch_shapes=[pltpu.VMEM((QKV_WIDTH, n), F32), pltpu.VMEM((N_HEADS * HEAD_DIM, n), F32),
                        pltpu.VMEM((N_HEADS * HEAD_DIM, n), F32)],
        compiler_params=_params("arbitrary"),
        name="window_attn_sample",
    )(sinks.reshape(N_HEADS, 1, 1), x, g_pre, g_post, w_qkv, b_qkv, w_o, cache_k, cache_v)


def _ssm_prep_kernel(ar_ref, ai_ref, dt_ref, br_ref, bi_ref, wr_ref, wi_ref, lam_ref):
    ar, ai, dt = ar_ref[...], ai_ref[...], jnp.exp(dt_ref[...])
    mag = jnp.exp(ar * dt)
    lr, li = mag * jnp.cos(ai * dt), mag * jnp.sin(ai * dt)
    den = ar * ar + ai * ai
    nr, ni = lr - 1.0, li
    zr = (nr * ar + ni * ai) / den
    zi = (ni * ar - nr * ai) / den
    br, bi = br_ref[...], bi_ref[...]
    wr = zr * br - zi * bi
    wi = zr * bi + zi * br
    pr, pi = lr, li
    for k in range(SSM_LAGS):
        wr_ref[k] = wr
        wi_ref[k] = wi
        wr, wi = lr * wr - li * wi, lr * wi + li * wr
        if k > 0:
            pr, pi = lr * pr - li * pi, lr * pi + li * pr
    lam_ref[0] = lr
    lam_ref[1] = li
    lam_ref[2] = pr
    lam_ref[3] = pi


def _ssm_tables(a_re, a_im, log_dt, b_re, b_im, c_re, c_im):
    G, P, GS, R, TG, NT = SSM_GROUPS, SSM_STATE, SSM_GROUP, SSM_LAGS, SSM_TILE_GROUPS, SSM_TILES
    rows = G * P
    dense = (rows * GS // LANES, LANES)
    spread = lambda a: jnp.broadcast_to(a.astype(F32)[:, :, None], (G, P, GS)).reshape(dense)
    log_dt = jnp.broadcast_to(log_dt.astype(F32)[:, None], (G, P))
    wr, wi, lam = pl.pallas_call(
        _ssm_prep_kernel,
        out_shape=[jax.ShapeDtypeStruct((R,) + dense, F32), jax.ShapeDtypeStruct((R,) + dense, F32),
                   jax.ShapeDtypeStruct((4,) + dense, F32)],
        name="ssm_prep",
    )(spread(a_re), spread(a_im), spread(log_dt), b_re.astype(F32).reshape(dense), b_im.astype(F32).reshape(dense))
    wr, wi = wr.reshape(R, rows, GS), wi.reshape(R, rows, GS)
    lam = lam.reshape(4, rows, GS)[:, :, 0]
    w = jnp.stack([wr, wi]).reshape(2, R, NT, TG, P, GS).transpose(2, 1, 3, 5, 0, 4)
    w = jnp.tile(w.reshape(NT * R * LANES, 2, P).astype(BF16), (1, 1, TG))
    row_group = (jnp.arange(NT * R * LANES, dtype=jnp.int32) // GS) % TG
    col_group = jnp.arange(SSM_TILE_STATE, dtype=jnp.int32) // P
    w_lag = jnp.where((row_group[:, None] == col_group[None, :])[:, None, :], w, 0).reshape(
        NT, R * LANES, 2 * SSM_TILE_STATE)

    def c_blocks(cm):
        ct = cm.astype(BF16).reshape(NT, TG, GS, P).transpose(0, 1, 3, 2)
        ct = jnp.tile(ct.reshape(NT * SSM_TILE_STATE, GS), (1, TG))
        rg = (jnp.arange(NT * SSM_TILE_STATE, dtype=jnp.int32) // P) % TG
        cg = jnp.arange(LANES, dtype=jnp.int32) // GS
        return jnp.where(rg[:, None] == cg[None, :], ct, 0).reshape(NT, SSM_TILE_STATE, LANES)

    def lam_rows(re, im):
        return jnp.concatenate([re.reshape(NT, 1, SSM_TILE_STATE), im.reshape(NT, 1, SSM_TILE_STATE)], axis=-1)

    return (w_lag, c_blocks(c_re), c_blocks(c_im), lam_rows(lam[0], lam[1]), lam_rows(lam[2], lam[3]))


def _glu_tail(x, u, y, d_ref, wglu_ref, bglu_ref, gpost_ref):
    y = y + d_ref[...] * u
    y = 0.5 * y * (1.0 + lax.erf(y * (2.0 ** -0.5)))
    z = _dot(y.astype(BF16), wglu_ref[...]) + bglu_ref[...]
    out = z[:, :D_MODEL] * jax.nn.sigmoid(z[:, D_MODEL:])
    return x + _rms(out, gpost_ref[...])


def _ssm_kernel(x_ref, gpre_ref, gpost_ref, wlag_ref, cr_ref, ci_ref, lamk_ref, d_ref, wglu_ref, bglu_ref,
                o_ref, hout_ref, ubuf_ref, uprev_ref, h2_ref, y_ref, carry_ref):
    t = pl.program_id(1)
    tm = x_ref.shape[0]
    S = SSM_TILE_STATE
    cur, nxt = t % 2, (t + 1) % 2

    @pl.when(t == 0)
    def _():
        uprev_ref[0] = jnp.zeros((SSM_LAGS, D_MODEL), F32)
        carry_ref[...] = jnp.zeros_like(carry_ref)

    x = x_ref[...]
    u = _rms(x, gpre_ref[...])
    ubuf_ref[0:SSM_LAGS, :] = uprev_ref[cur]
    ubuf_ref[SSM_LAGS:, :] = u

    for c in range(SSM_TILES):
        cols = slice(c * LANES, (c + 1) * LANES)
        h_ref = h2_ref.at[c % 2]
        lhs = jnp.concatenate(
            [ubuf_ref[SSM_LAGS - k:SSM_LAGS - k + tm, cols] for k in range(SSM_LAGS)], axis=1).astype(BF16)
        h_ref[...] = _dot(lhs, wlag_ref[c])
        lam = lamk_ref[c]
        lr = jnp.broadcast_to(lam[:, :S], (SUBLANES, S))
        li = jnp.broadcast_to(lam[:, S:], (SUBLANES, S))

        def slab(m, carry):
            r0 = pl.multiple_of(m * SUBLANES, SUBLANES)
            hw = h_ref[pl.ds(r0, SUBLANES), :]
            cr, ci = carry[:, :S], carry[:, S:]
            new = jnp.concatenate([hw[:, :S] + (lr * cr - li * ci), hw[:, S:] + (lr * ci + li * cr)], axis=1)
            h_ref[pl.ds(r0, SUBLANES), :] = new
            return new

        carry_ref[c] = lax.fori_loop(0, tm // SUBLANES, slab, carry_ref[c])
        y_ref[:, cols] = (_dot(h_ref[:, :S].astype(BF16), cr_ref[c])
                          - _dot(h_ref[:, S:].astype(BF16), ci_ref[c]))

    o_ref[...] = _glu_tail(x, u, y_ref[...], d_ref, wglu_ref, bglu_ref, gpost_ref)
    uprev_ref[nxt] = u[tm - SSM_LAGS:, :]

    @pl.when(t == pl.num_programs(1) - 1)
    def _():
        hout_ref[...] = carry_ref[...]


def _ssm_prompt(x, g_pre, g_post, tables, d_skip, w_glu, b_glu, layer, bsz, seq):
    w_lag, c_r, c_i, _, lamk = tables
    tm = SSM_TOKENS
    per_seq = seq // tm
    S2 = 2 * SSM_TILE_STATE
    once = pl.Buffered(1)
    out, h_last = pl.pallas_call(
        _ssm_kernel,
        grid=(bsz, per_seq),
        in_specs=[
            pl.BlockSpec((tm, D_MODEL), lambda b, t: (b * per_seq + t, 0)),
            _const_spec((1, D_MODEL)),
            _const_spec((1, D_MODEL)),
            pl.BlockSpec(w_lag.shape, lambda b, t: (0, 0, 0), pipeline_mode=once),
            pl.BlockSpec(c_r.shape, lambda b, t: (0, 0, 0), pipeline_mode=once),
            pl.BlockSpec(c_i.shape, lambda b, t: (0, 0, 0), pipeline_mode=once),
            _const_spec(lamk.shape),
            _const_spec((1, D_MODEL)),
            pl.BlockSpec((None, D_MODEL, 2 * D_MODEL), lambda b, t: (layer, 0, 0), pipeline_mode=once),
            _const_spec((1, 2 * D_MODEL)),
        ],
        out_specs=[
            pl.BlockSpec((tm, D_MODEL), lambda b, t: (b * per_seq + t, 0)),
            pl.BlockSpec((None, SSM_TILES, SUBLANES, S2), lambda b, t: (b, 0, 0, 0)),
        ],
        out_shape=[jax.ShapeDtypeStruct(x.shape, F32),
                   jax.ShapeDtypeStruct((bsz, SSM_TILES, SUBLANES, S2), F32)],
        scratch_shapes=[
            pltpu.VMEM((tm + SSM_LAGS, D_MODEL), F32),
            pltpu.VMEM((2, SSM_LAGS, D_MODEL), F32),
            pltpu.VMEM((2, tm, S2), F32),
            pltpu.VMEM((tm, D_MODEL), F32),
            pltpu.VMEM((SSM_TILES, SUBLANES, S2), F32),
        ],
        compiler_params=_params("arbitrary", "arbitrary"),
        name="ssm_prompt",
    )(x, g_pre, g_post, w_lag, c_r, c_i, lamk, d_skip, w_glu, b_glu)
    h_last = h_last[:, :, SUBLANES - 1, :]
    shape = (bsz, SSM_GROUPS, SSM_STATE)
    return out, h_last[..., :SSM_TILE_STATE].reshape(shape), h_last[..., SSM_TILE_STATE:].reshape(shape)


def _ssm_sample_kernel(x_ref, sre_ref, sim_ref, gpre_ref, gpost_ref, w0_ref, cr_ref, ci_ref, lam1_ref,
                       d_ref, wglu_ref, bglu_ref, o_ref, nre_ref, nim_ref, y_ref):
    S = SSM_TILE_STATE
    x = x_ref[...]
    u = _rms(x, gpre_ref[...])
    for c in range(SSM_TILES):
        cols = slice(c * LANES, (c + 1) * LANES)
        st = slice(c * S, (c + 1) * S)
        bu = _dot(u[:, cols].astype(BF16), w0_ref[c])
        lam = lam1_ref[c]
        lr, li = lam[:, :S], lam[:, S:]
        h0r, h0i = sre_ref[:, st], sim_ref[:, st]
        hr = bu[:, :S] + (lr * h0r - li * h0i)
        hi = bu[:, S:] + (lr * h0i + li * h0r)
        nre_ref[:, st] = hr
        nim_ref[:, st] = hi
        y_ref[:, cols] = _dot(hr.astype(BF16), cr_ref[c]) - _dot(hi.astype(BF16), ci_ref[c])
    o_ref[...] = _glu_tail(x, u, y_ref[...], d_ref, wglu_ref, bglu_ref, gpost_ref)


def _ssm_sample(x, state_re, state_im, g_pre, g_post, tables, d_skip, w_glu, b_glu, layer):
    w_lag, c_r, c_i, lam1, _ = tables
    n = x.shape[0]
    flat = (n, SSM_GROUPS * SSM_STATE)
    st = jax.ShapeDtypeStruct(flat, F32)
    S2 = 2 * SSM_TILE_STATE
    out, nre, nim = pl.pallas_call(
        _ssm_sample_kernel,
        grid=(1,),
        in_specs=[
            _const_spec((n, D_MODEL)),
            _const_spec(flat),
            _const_spec(flat),
            _const_spec((1, D_MODEL)),
            _const_spec((1, D_MODEL)),
            _const_spec((SSM_TILES, LANES, S2)),
            _const_spec(c_r.shape),
            _const_spec(c_i.shape),
            _const_spec(lam1.shape),
            _const_spec((1, D_MODEL)),
            pl.BlockSpec((None, D_MODEL, 2 * D_MODEL), lambda i: (layer, 0, 0)),
            _const_spec((1, 2 * D_MODEL)),
        ],
        out_specs=[_const_spec((n, D_MODEL)), _const_spec(flat), _const_spec(flat)],
        out_shape=[jax.ShapeDtypeStruct(x.shape, F32), st, st],
        scratch_shapes=[pltpu.VMEM((n, D_MODEL), F32)],
        compiler_params=_params("arbitrary"),
        name="ssm_sample",
    )(x, state_re.reshape(flat), state_im.reshape(flat), g_pre, g_post, w_lag, c_r, c_i, lam1,
      d_skip, w_glu, b_glu)
    shape = (n, SSM_GROUPS, SSM_STATE)
    return out, nre.reshape(shape), nim.reshape(shape)


def kernel(x_prompt, x_sample, mem_prompt, state_ssm_re, state_ssm_im, cache_win_k, cache_win_v, cache_mem_k, cache_mem_v, norm_g, mem_norm_g, ffn_w_in, ffn_w_out, ssm_a_re, ssm_a_im, ssm_log_dt, ssm_b_re, ssm_b_im, ssm_c_re, ssm_c_im, ssm_d, ssm_w_glu, ssm_b_glu, attn_w_qkv, attn_b_qkv, attn_w_o, attn_sinks, ca_w_q, ca_w_kv, ca_w_o):
    bp, seq, _ = x_prompt.shape
    bs = x_sample.shape[0]
    xp = x_prompt.reshape(bp * seq, D_MODEL)
    xs = x_sample.reshape(bs, D_MODEL)

    gain = lambda i, r: norm_g[i, r].astype(F32).reshape(1, D_MODEL)
    ffn_w_in_b, ffn_w_out_b = ffn_w_in.astype(BF16), ffn_w_out.astype(BF16)
    ssm_w_glu_b = ssm_w_glu.astype(BF16)
    attn_w_qkv_b, attn_w_o_b = attn_w_qkv.astype(BF16), attn_w_o.astype(BF16)
    ca_w_q_b, ca_w_o_b = ca_w_q.astype(BF16), ca_w_o.astype(BF16)

    mem_k, mem_v = _mem_kv(mem_prompt, mem_norm_g.astype(F32), ca_w_kv.astype(BF16))
    cache_win_kt = cache_win_k.transpose(0, 1, 3, 4, 2)
    cache_win_vt = cache_win_v.transpose(0, 1, 3, 4, 2)

    ssm_re_p, ssm_im_p, ssm_re_s, ssm_im_s = [], [], [], []
    wk_p, wv_p, wk_s, wv_s = [], [], [], []
    for i in range(DEPTH):
        li = i // N_MIXERS
        xp = _half_ffn(xp, gain(i, 0), gain(i, 1), ffn_w_in_b, ffn_w_out_b, i, 0, FFN_TOKENS)
        xs = _half_ffn(xs, gain(i, 0), gain(i, 1), ffn_w_in_b, ffn_w_out_b, i, 0, FFN_TOKENS)
        if i % N_MIXERS == 0:
            tables = _ssm_tables(ssm_a_re[li], ssm_a_im[li], ssm_log_dt[li], ssm_b_re[li], ssm_b_im[li],
                                 ssm_c_re[li], ssm_c_im[li])
            d_skip = ssm_d[li].astype(F32).reshape(1, D_MODEL)
            b_glu = ssm_b_glu[li].astype(F32).reshape(1, 2 * D_MODEL)
            xp, hr_p, hi_p = _ssm_prompt(xp, gain(i, 2), gain(i, 3), tables, d_skip, ssm_w_glu_b, b_glu, li,
                                         bp, seq)
            xs, hr_s, hi_s = _ssm_sample(xs, state_ssm_re[li], state_ssm_im[li], gain(i, 2), gain(i, 3), tables,
                                         d_skip, ssm_w_glu_b, b_glu, li)
            ssm_re_p.append(hr_p); ssm_im_p.append(hi_p)
            ssm_re_s.append(hr_s); ssm_im_s.append(hi_s)
        else:
            b_qkv = attn_b_qkv[li].astype(F32).reshape(1, QKV_WIDTH)
            sinks = attn_sinks[li].astype(F32)
            xp, bk_p, bv_p = _window_attention_prompt(xp, gain(i, 2), gain(i, 3), attn_w_qkv_b, b_qkv,
                                                      attn_w_o_b, sinks, li, bp, seq)
            xs, bk_s, bv_s = _window_attention_sample(xs, gain(i, 2), gain(i, 3), attn_w_qkv_b, b_qkv, attn_w_o_b,
                                                      sinks, cache_win_kt, cache_win_vt, li)
            wk_p.append(bk_p.reshape(bp, WINDOW, N_KV_HEADS, HEAD_DIM))
            wv_p.append(bv_p.reshape(bp, WINDOW, N_KV_HEADS, HEAD_DIM))
            wk_s.append(bk_s); wv_s.append(bv_s)
        xp = _cross_attention_prompt(xp, gain(i, 4), gain(i, 5), ca_w_q_b, ca_w_o_b, mem_k, mem_v, i, seq)
        xs = _cross_attention_sample(xs, gain(i, 4), gain(i, 5), ca_w_q_b, ca_w_o_b, cache_mem_k, cache_mem_v, i)
        xp = _half_ffn(xp, gain(i, 6), gain(i, 7), ffn_w_in_b, ffn_w_out_b, i, 1, FFN_TOKENS)
        xs = _half_ffn(xs, gain(i, 6), gain(i, 7), ffn_w_in_b, ffn_w_out_b, i, 1, FFN_TOKENS)

    mem_shape = (DEPTH, bp, N_MEM, CA_HEADS, CA_HEAD_DIM)
    return (xp.reshape(bp, seq, D_MODEL), xs.reshape(bs, 1, D_MODEL),
            jnp.stack(ssm_re_p), jnp.stack(ssm_im_p), jnp.stack(wk_p), jnp.stack(wv_p),
            mem_k.reshape(mem_shape), mem_v.reshape(mem_shape),
            jnp.stack(ssm_re_s), jnp.stack(ssm_im_s),
            jnp.stack(wk_s).transpose(0, 1, 4, 2, 3), jnp.stack(wv_s).transpose(0, 1, 4, 2, 3))
```

```python
import functools
import math

import jax
import jax.numpy as jnp
from jax import lax
from jax.experimental import pallas as pl
from jax.experimental.pallas import tpu as pltpu

F32 = jnp.float32
BF16 = jnp.bfloat16

D_MODEL = 1024
DEPTH = 4
N_MIXERS = 2
SSM_GROUP = 16
SSM_GROUPS = D_MODEL // SSM_GROUP
SSM_STATE = 64
HEAD_DIM = 64
N_HEADS = D_MODEL // HEAD_DIM
N_KV_HEADS = 4
GQA = N_HEADS // N_KV_HEADS
WINDOW = 128
KV_WIDTH = N_KV_HEADS * HEAD_DIM
QKV_WIDTH = (N_HEADS + 2 * N_KV_HEADS) * HEAD_DIM
N_MEM = 256
CA_HEADS = 4
CA_HEAD_DIM = D_MODEL // CA_HEADS
D_FF = ((8 * D_MODEL // 3 + 127) // 128) * 128
FFN_RES = 0.5
EPS = 1e-6
NEG = -1e30

SUBLANES = 8
LANES = 128
VMEM_LIMIT_BYTES = 56 * 1024 * 1024

SSM_LAGS = SUBLANES
SSM_TILE_GROUPS = LANES // SSM_GROUP
SSM_TILES = SSM_GROUPS // SSM_TILE_GROUPS
SSM_TILE_STATE = SSM_TILE_GROUPS * SSM_STATE
SSM_PAIR_GROUPS = 4
SSM_TILE_PAIRS = SSM_TILE_GROUPS // SSM_PAIR_GROUPS
SSM_PAIR_CH = SSM_PAIR_GROUPS * SSM_GROUP
SSM_PAIR_STATE = SSM_PAIR_GROUPS * SSM_STATE
SSM_PAIR_WIDTH = 2 * SSM_PAIR_STATE

FFN_CHUNK = 256
FFN_TOKENS = 1024
FFN_ROW_BLOCK = 256
CA_TOKENS = 512
SWA_TOKENS = 512
SSM_TOKENS = 512
SAMPLE_CA_BLOCK = 4
SAMPLE_SWA_BLOCK = 16


def _params(*sem):
    return pltpu.CompilerParams(dimension_semantics=sem, vmem_limit_bytes=VMEM_LIMIT_BYTES)


def _rms(x, g):
    r = lax.rsqrt(jnp.mean(x * x, axis=-1, keepdims=True) + EPS)
    return x * r * g


def _dot(a, b):
    return jnp.dot(a, b, preferred_element_type=F32)


def _dot_nt(a, b):
    return lax.dot_general(a, b, (((1,), (1,)), ((), ())), preferred_element_type=F32)


def _const_spec(shape):
    zeros = (0,) * len(shape)
    return pl.BlockSpec(shape, lambda *_: zeros)


def _ffn_kernel(x_ref, gpre_ref, gpost_ref, wg_ref, wu_ref, wo_ref, o_ref, xn_ref, acc_ref):
    c = pl.program_id(1)
    last = pl.num_programs(1) - 1
    tm = x_ref.shape[0]
    rb = min(FFN_ROW_BLOCK, tm)
    row_blocks = [slice(r * rb, (r + 1) * rb) for r in range(tm // rb)]

    def chunk(xn):
        gate = _dot(xn, wg_ref[...])
        up = _dot(xn, wu_ref[...])
        h = (gate * jax.nn.sigmoid(gate) * up).astype(BF16)
        return _dot(h, wo_ref[...])

    @pl.when(c == 0)
    def _():
        for rows in row_blocks:
            xn = _rms(x_ref[rows, :], gpre_ref[...]).astype(BF16)
            xn_ref[rows, :] = xn
            acc_ref[rows, :] = chunk(xn)

    @pl.when((c > 0) & (c < last))
    def _():
        acc_ref[...] += chunk(xn_ref[...])

    @pl.when(c == last)
    def _():
        for rows in row_blocks:
            y = acc_ref[rows, :] + chunk(xn_ref[rows, :])
            o_ref[rows, :] = x_ref[rows, :] + FFN_RES * _rms(y, gpost_ref[...])


def _half_ffn(x, g_pre, g_post, w_in, w_out, layer, half, tokens):
    n = x.shape[0]
    tm = min(tokens, n)
    n_chunks = D_FF // FFN_CHUNK
    return pl.pallas_call(
        _ffn_kernel,
        grid=(n // tm, n_chunks),
        in_specs=[
            pl.BlockSpec((tm, D_MODEL), lambda i, c: (i, 0)),
            _const_spec((1, D_MODEL)),
            _const_spec((1, D_MODEL)),
            pl.BlockSpec((None, None, D_MODEL, FFN_CHUNK), lambda i, c: (layer, half, 0, c)),
            pl.BlockSpec((None, None, D_MODEL, FFN_CHUNK), lambda i, c: (layer, half, 0, c + n_chunks)),
            pl.BlockSpec((None, None, FFN_CHUNK, D_MODEL), lambda i, c: (layer, half, c, 0)),
        ],
        out_specs=pl.BlockSpec((tm, D_MODEL), lambda i, c: (i, 0)),
        out_shape=jax.ShapeDtypeStruct((n, D_MODEL), F32),
        scratch_shapes=[pltpu.VMEM((tm, D_MODEL), BF16), pltpu.VMEM((tm, D_MODEL), F32)],
        compiler_params=_params("parallel", "arbitrary"),
        name="half_ffn",
    )(x, g_pre, g_post, w_in, w_in, w_out)


def _mem_kv_kernel(mem_ref, g_ref, w_ref, k_ref, v_ref):
    mn = _rms(mem_ref[...], g_ref[...]).astype(BF16)
    kv = _dot(mn, w_ref[...])
    k_ref[...] = kv[:, :D_MODEL]
    v_ref[...] = kv[:, D_MODEL:]


def _mem_kv(mem, g_mem, w_kv):
    bsz = mem.shape[0]
    out = jax.ShapeDtypeStruct((DEPTH, bsz, N_MEM, D_MODEL), F32)
    out_spec = pl.BlockSpec((None, None, N_MEM, D_MODEL), lambda l, b: (l, b, 0, 0))
    return pl.pallas_call(
        _mem_kv_kernel,
        grid=(DEPTH, bsz),
        in_specs=[
            pl.BlockSpec((None, N_MEM, D_MODEL), lambda l, b: (b, 0, 0)),
            pl.BlockSpec((None, 1, D_MODEL), lambda l, b: (l, 0, 0)),
            pl.BlockSpec((None, D_MODEL, 2 * D_MODEL), lambda l, b: (l, 0, 0)),
        ],
        out_specs=[out_spec, out_spec],
        out_shape=[out, out],
        compiler_params=_params("parallel", "parallel"),
        name="mem_kv",
    )(mem, g_mem.reshape(DEPTH, 1, D_MODEL), w_kv)


def _ca_kernel(x_ref, gpre_ref, gpost_ref, wq_ref, wo_ref, mk_ref, mv_ref, o_ref):
    x = x_ref[...]
    xn = _rms(x, gpre_ref[...]).astype(BF16)
    q = _dot(xn, wq_ref[...])
    heads = []
    for h in range(CA_HEADS):
        cols = slice(h * CA_HEAD_DIM, (h + 1) * CA_HEAD_DIM)
        s = _dot_nt(q[:, cols].astype(BF16), mk_ref[:, cols].astype(BF16)) * (CA_HEAD_DIM ** -0.5)
        p = jnp.exp(s - jnp.max(s, axis=-1, keepdims=True))
        den = jnp.sum(p, axis=-1, keepdims=True)
        heads.append(_dot(p.astype(BF16), mv_ref[:, cols].astype(BF16)) / den)
    o = jnp.concatenate(heads, axis=-1).astype(BF16)
    o_ref[...] = x + _rms(_dot(o, wo_ref[...]), gpost_ref[...])


def _cross_attention_prompt(x, g_pre, g_post, w_q, w_o, mk, mv, layer, seq):
    n = x.shape[0]
    tm = CA_TOKENS
    per_seq = seq // tm
    mem_spec = pl.BlockSpec((None, None, N_MEM, D_MODEL), lambda i: (layer, i // per_seq, 0, 0))
    w_spec = pl.BlockSpec((None, D_MODEL, D_MODEL), lambda i: (layer, 0, 0))
    return pl.pallas_call(
        _ca_kernel,
        grid=(n // tm,),
        in_specs=[
            pl.BlockSpec((tm, D_MODEL), lambda i: (i, 0)),
            _const_spec((1, D_MODEL)),
            _const_spec((1, D_MODEL)),
            w_spec,
            w_spec,
            mem_spec,
            mem_spec,
        ],
        out_specs=pl.BlockSpec((tm, D_MODEL), lambda i: (i, 0)),
        out_shape=jax.ShapeDtypeStruct((n, D_MODEL), F32),
        compiler_params=_params("parallel"),
        name="cross_attn_prompt",
    )(x, g_pre, g_post, w_q, w_o, mk, mv)


def _stack_rows(pieces):
    n, w = len(pieces), pieces[0].shape[1]
    sub = lax.broadcasted_iota(jnp.int32, (n, w), 0)
    out = jnp.broadcast_to(pieces[0], (n, w))
    for r in range(1, n):
        out = jnp.where(sub == r, jnp.broadcast_to(pieces[r], (n, w)), out)
    return out


def _ca_sample_kernel(x_ref, gpre_ref, gpost_ref, wq_ref, wo_ref, mk_ref, mv_ref, o_ref, q_ref, att_ref):
    i = pl.program_id(0)

    @pl.when(i == 0)
    def _():
        xn = _rms(x_ref[...], gpre_ref[...]).astype(BF16)
        q_ref[...] = _dot(xn, wq_ref[...]) * (CA_HEAD_DIM ** -0.5)

    for b in range(SAMPLE_CA_BLOCK):
        row = i * SAMPLE_CA_BLOCK + b
        q = q_ref[pl.ds(row, 1), :]
        q4 = _stack_rows([q[:, h * CA_HEAD_DIM:(h + 1) * CA_HEAD_DIM] for h in range(CA_HEADS)])
        s = jnp.sum(mk_ref[b] * q4[None], axis=-1, keepdims=True)
        p = jnp.exp(s - jnp.max(s, axis=0, keepdims=True))
        den = jnp.sum(p, axis=0)
        o4 = jnp.sum(p * mv_ref[b], axis=0) / den
        att_ref[pl.ds(row, 1), :] = jnp.concatenate([o4[h:h + 1, :] for h in range(CA_HEADS)], axis=-1)

    @pl.when(i == pl.num_programs(0) - 1)
    def _():
        y = _dot(att_ref[...].astype(BF16), wo_ref[...])
        o_ref[...] = x_ref[...] + _rms(y, gpost_ref[...])


def _cross_attention_sample(x, g_pre, g_post, w_q, w_o, mk, mv, layer):
    n = x.shape[0]
    mem_spec = pl.BlockSpec((None, SAMPLE_CA_BLOCK, N_MEM, CA_HEADS, CA_HEAD_DIM), lambda i: (layer, i, 0, 0, 0))
    w_spec = pl.BlockSpec((None, D_MODEL, D_MODEL), lambda i: (layer, 0, 0))
    return pl.pallas_call(
        _ca_sample_kernel,
        grid=(n // SAMPLE_CA_BLOCK,),
        in_specs=[
            _const_spec((n, D_MODEL)),
            _const_spec((1, D_MODEL)),
            _const_spec((1, D_MODEL)),
            w_spec,
            w_spec,
            mem_spec,
            mem_spec,
        ],
        out_specs=_const_spec((n, D_MODEL)),
        out_shape=jax.ShapeDtypeStruct((n, D_MODEL), F32),
        scratch_shapes=[pltpu.VMEM((n, D_MODEL), F32), pltpu.VMEM((n, D_MODEL), F32)],
        compiler_params=_params("arbitrary"),
        name="cross_attn_sample",
    )(x, g_pre, g_post, w_q, w_o, mk, mv)


def _swa_kernel(sink_ref, x_ref, gpre_ref, gpost_ref, wqkv_ref, bqkv_ref, wo_ref,
                o_ref, kout_ref, vout_ref, kprev_ref, vprev_ref, q_ref):
    t = pl.program_id(1)
    tm = x_ref.shape[0]
    x = x_ref[...]
    xn = _rms(x, gpre_ref[...]).astype(BF16)
    qkv = _dot(xn, wqkv_ref[...]) + bqkv_ref[...]
    q_ref[...] = (qkv[:, :N_HEADS * HEAD_DIM] * (HEAD_DIM ** -0.5)).astype(BF16)
    k = qkv[:, N_HEADS * HEAD_DIM:N_HEADS * HEAD_DIM + KV_WIDTH]
    v = qkv[:, N_HEADS * HEAD_DIM + KV_WIDTH:]

    cur, nxt = t % 2, (t + 1) % 2

    @pl.when(t == 0)
    def _():
        kprev_ref[0] = jnp.zeros((WINDOW, KV_WIDTH), BF16)
        vprev_ref[0] = jnp.zeros((WINDOW, KV_WIDTH), BF16)

    kall = jnp.concatenate([kprev_ref[cur], k.astype(BF16)], axis=0)
    vall = jnp.concatenate([vprev_ref[cur], v.astype(BF16)], axis=0)

    row = lax.broadcasted_iota(jnp.int32, (GQA * WINDOW, 2 * WINDOW), 0)
    qi = row & (WINDOW - 1)
    kj = lax.broadcasted_iota(jnp.int32, (GQA * WINDOW, 2 * WINDOW), 1)
    band = (kj >= qi) & (kj <= qi + WINDOW)
    row_g = lax.broadcasted_iota(jnp.int32, (GQA * WINDOW, 1), 0) // WINDOW

    for n in range(tm // WINDOW):
        rows = slice(n * WINDOW, (n + 1) * WINDOW)
        kk = kall[n * WINDOW:(n + 2) * WINDOW, :]
        vv = vall[n * WINDOW:(n + 2) * WINDOW, :]
        valid = band & (kj >= WINDOW - (t * tm + n * WINDOW)) if n == 0 else band
        kv_cols = [slice(kh * HEAD_DIM, (kh + 1) * HEAD_DIM) for kh in range(N_KV_HEADS)]
        scores, sinks = [], []
        for kh in range(N_KV_HEADS):
            q4 = jnp.concatenate(
                [q_ref[rows, (kh * GQA + g) * HEAD_DIM:(kh * GQA + g + 1) * HEAD_DIM] for g in range(GQA)],
                axis=0)
            sink = jnp.full((GQA * WINDOW, 1), sink_ref[kh * GQA], F32)
            for g in range(1, GQA):
                sink = jnp.where(row_g == g, sink_ref[kh * GQA + g], sink)
            sinks.append(sink)
            scores.append(jnp.where(valid, _dot_nt(q4, kk[:, kv_cols[kh]]), NEG))
        maxes = [jnp.maximum(jnp.max(s, axis=-1, keepdims=True), sink) for s, sink in zip(scores, sinks)]
        probs = [jnp.exp(s - m) for s, m in zip(scores, maxes)]
        dens = [jnp.sum(p, axis=-1, keepdims=True) + jnp.exp(sink - m) for p, sink, m in zip(probs, sinks, maxes)]
        outs = [_dot(p.astype(BF16), vv[:, kv_cols[kh]]) for kh, p in enumerate(probs)]
        heads = []
        for kh in range(N_KV_HEADS):
            o4 = outs[kh] / dens[kh]
            heads += [o4[g * WINDOW:(g + 1) * WINDOW] for g in range(GQA)]
        att = jnp.concatenate(heads, axis=1).astype(BF16)
        o_ref[rows, :] = x_ref[rows, :] + _rms(_dot(att, wo_ref[...]), gpost_ref[...])

    kprev_ref[nxt] = k[tm - WINDOW:, :].astype(BF16)
    vprev_ref[nxt] = v[tm - WINDOW:, :].astype(BF16)

    kout_ref[...] = k[tm - WINDOW:, :]
    vout_ref[...] = v[tm - WINDOW:, :]


def _window_attention_prompt(x, g_pre, g_post, w_qkv, b_qkv, w_o, sinks, layer, bsz, seq):
    tm = SWA_TOKENS
    per_seq = seq // tm
    win_spec = pl.BlockSpec((None, WINDOW, KV_WIDTH), lambda b, t, *_: (b, 0, 0))
    win_shape = jax.ShapeDtypeStruct((bsz, WINDOW, KV_WIDTH), F32)
    grid_spec = pltpu.PrefetchScalarGridSpec(
        num_scalar_prefetch=1,
        grid=(bsz, per_seq),
        in_specs=[
            pl.BlockSpec((tm, D_MODEL), lambda b, t, *_: (b * per_seq + t, 0)),
            _const_spec((1, D_MODEL)),
            _const_spec((1, D_MODEL)),
            pl.BlockSpec((None, D_MODEL, QKV_WIDTH), lambda b, t, *_: (layer, 0, 0)),
            _const_spec((1, QKV_WIDTH)),
            pl.BlockSpec((None, N_HEADS * HEAD_DIM, D_MODEL), lambda b, t, *_: (layer, 0, 0)),
        ],
        out_specs=[pl.BlockSpec((tm, D_MODEL), lambda b, t, *_: (b * per_seq + t, 0)), win_spec, win_spec],
        scratch_shapes=[
            pltpu.VMEM((2, WINDOW, KV_WIDTH), BF16),
            pltpu.VMEM((2, WINDOW, KV_WIDTH), BF16),
            pltpu.VMEM((tm, N_HEADS * HEAD_DIM), BF16),
        ],
    )
    return pl.pallas_call(
        _swa_kernel,
        grid_spec=grid_spec,
        out_shape=[jax.ShapeDtypeStruct(x.shape, F32), win_shape, win_shape],
        compiler_params=_params("arbitrary", "arbitrary"),
        name="window_attn_prompt",
    )(sinks, x, g_pre, g_post, w_qkv, b_qkv, w_o)


def _swa_sample_kernel(sink_ref, x_ref, gpre_ref, gpost_ref, wqkv_ref, bqkv_ref, wo_ref, ck_ref, cv_ref,
                       o_ref, nk_ref, nv_ref, qkvt_ref, attt_ref, blkt_ref):
    i = pl.program_id(0)
    n = x_ref.shape[0]
    k0 = N_HEADS * HEAD_DIM
    v0 = k0 + KV_WIDTH

    @pl.when(i == 0)
    def _():
        xn = _rms(x_ref[...], gpre_ref[...]).astype(BF16)
        qkvt_ref[...] = (_dot(xn, wqkv_ref[...]) + bqkv_ref[...]).T
        attt_ref[...] = jnp.zeros_like(attt_ref)
        blkt_ref[...] = jnp.zeros_like(blkt_ref)

    base = i * SAMPLE_SWA_BLOCK
    qkvt = pltpu.roll(qkvt_ref[...], (n - base) % n, axis=1)
    newest = lax.broadcasted_iota(jnp.int32, (KV_WIDTH, WINDOW), 1) == WINDOW - 1
    heads3 = lambda a: a.reshape(N_HEADS, HEAD_DIM, a.shape[-1])
    per_q_head = lambda a: jnp.concatenate(
        [a[(h // GQA) * HEAD_DIM:(h // GQA + 1) * HEAD_DIM] for h in range(N_HEADS)], axis=0)
    sink = sink_ref[...]
    for b in range(SAMPLE_SWA_BLOCK):
        col = qkvt[:, b:b + 1]
        q = col[0:k0] * (HEAD_DIM ** -0.5)
        k_new, v_new = col[k0:v0], col[v0:]
        kt = ck_ref[b].reshape(KV_WIDTH, WINDOW)
        vt = cv_ref[b].reshape(KV_WIDTH, WINDOW)
        s = jnp.sum(heads3(per_q_head(kt) * q), axis=1, keepdims=True)
        s_new = jnp.sum(heads3(per_q_head(k_new) * q), axis=1, keepdims=True)
        m = jnp.maximum(jnp.maximum(jnp.max(s, axis=2, keepdims=True), s_new), sink)
        p = jnp.exp(s - m)
        p_new = jnp.exp(s_new - m)
        den = jnp.sum(p, axis=2, keepdims=True) + p_new + jnp.exp(sink - m)
        o = jnp.sum(heads3(per_q_head(vt)) * p, axis=2, keepdims=True)
        o = (o + p_new * heads3(per_q_head(v_new))) / den
        blkt_ref[:, b:b + 1] = o.reshape(N_HEADS * HEAD_DIM, 1)
        nk_ref[b] = jnp.where(newest, k_new, pltpu.roll(kt, WINDOW - 1, axis=1)).reshape(nk_ref.shape[1:])
        nv_ref[b] = jnp.where(newest, v_new, pltpu.roll(vt, WINDOW - 1, axis=1)).reshape(nv_ref.shape[1:])

    lane = lax.broadcasted_iota(jnp.int32, attt_ref.shape, 1)
    mine = (lane >= base) & (lane < base + SAMPLE_SWA_BLOCK)
    attt_ref[...] = jnp.where(mine, pltpu.roll(blkt_ref[...], base, axis=1), attt_ref[...])

    @pl.when(i == pl.num_programs(0) - 1)
    def _():
        y = _dot(attt_ref[...].T.astype(BF16), wo_ref[...])
        o_ref[...] = x_ref[...] + _rms(y, gpost_ref[...])


def _window_attention_sample(x, g_pre, g_post, w_qkv, b_qkv, w_o, sinks, cache_k, cache_v, layer):
    n = x.shape[0]
    assert n == LANES, "the sample kernel keeps one sample per lane"
    blk = SAMPLE_SWA_BLOCK
    cache_spec = pl.BlockSpec((None, blk, N_KV_HEADS, HEAD_DIM, WINDOW), lambda i, *_: (layer, i, 0, 0, 0))
    win_spec = pl.BlockSpec((blk, N_KV_HEADS, HEAD_DIM, WINDOW), lambda i, *_: (i, 0, 0, 0))
    win_shape = jax.ShapeDtypeStruct((n, N_KV_HEADS, HEAD_DIM, WINDOW), F32)
    return pl.pallas_call(
        _swa_sample_kernel,
        grid=(n // blk,),
        in_specs=[
            _const_spec((N_HEADS, 1, 1)),
            _const_spec((n, D_MODEL)),
            _const_spec((1, D_MODEL)),
            _const_spec((1, D_MODEL)),
            pl.BlockSpec((None, D_MODEL, QKV_WIDTH), lambda i, *_: (layer, 0, 0)),
            _const_spec((1, QKV_WIDTH)),
            pl.BlockSpec((None, N_HEADS * HEAD_DIM, D_MODEL), lambda i, *_: (layer, 0, 0)),
            cache_spec,
            cache_spec,
        ],
        out_specs=[_const_spec((n, D_MODEL)), win_spec, win_spec],
        out_shape=[jax.ShapeDtypeStruct(x.shape, F32), win_shape, win_shape],
        scratch_shapes=[pltpu.VMEM((QKV_WIDTH, n), F32), pltpu.VMEM((N_HEADS * HEAD_DIM, n), F32),
                        pltpu.VMEM((N_HEADS * HEAD_DIM, n), F32)],
        compiler_params=_params("arbitrary"),
        name="window_attn_sample",
    )(sinks.reshape(N_HEADS, 1, 1), x, g_pre, g_post, w_qkv, b_qkv, w_o, cache_k, cache_v)


def _ssm_prep_kernel(ar_ref, ai_ref, dt_ref, br_ref, bi_ref, wr_ref, wi_ref, lam_ref):
    ar, ai, dt = ar_ref[...], ai_ref[...], jnp.exp(dt_ref[...])
    mag = jnp.exp(ar * dt)
    lr, li = mag * jnp.cos(ai * dt), mag * jnp.sin(ai * dt)
    den = ar * ar + ai * ai
    nr, ni = lr - 1.0, li
    zr = (nr * ar + ni * ai) / den
    zi = (ni * ar - nr * ai) / den
    br, bi = br_ref[...], bi_ref[...]
    wr = zr * br - zi * bi
    wi = zr * bi + zi * br
    pr, pi = lr, li
    for k in range(SSM_LAGS):
        wr_ref[k] = wr
        wi_ref[k] = wi
        wr, wi = lr * wr - li * wi, lr * wi + li * wr
        if k > 0:
            pr, pi = lr * pr - li * pi, lr * pi + li * pr
    lam_ref[0] = lr
    lam_ref[1] = li
    lam_ref[2] = pr
    lam_ref[3] = pi


def _ssm_tables(a_re, a_im, log_dt, b_re, b_im, c_re, c_im):
    G, P, GS, R, NT = SSM_GROUPS, SSM_STATE, SSM_GROUP, SSM_LAGS, SSM_TILES
    rows = G * P
    dense = (rows * GS // LANES, LANES)
    spread = lambda a: jnp.broadcast_to(a.astype(F32)[:, :, None], (G, P, GS)).reshape(dense)
    log_dt = jnp.broadcast_to(log_dt.astype(F32)[:, None], (G, P))
    wr, wi, lam = pl.pallas_call(
        _ssm_prep_kernel,
        out_shape=[jax.ShapeDtypeStruct((R,) + dense, F32), jax.ShapeDtypeStruct((R,) + dense, F32),
                   jax.ShapeDtypeStruct((4,) + dense, F32)],
        name="ssm_prep",
    )(spread(a_re), spread(a_im), spread(log_dt), b_re.astype(F32).reshape(dense), b_im.astype(F32).reshape(dense))
    wr, wi = wr.reshape(R, rows, GS), wi.reshape(R, rows, GS)
    lam = lam.reshape(4, rows, GS)[:, :, 0]
    NQ, PG = SSM_TILE_PAIRS, SSM_PAIR_GROUPS
    iota = lambda n: jnp.arange(n, dtype=jnp.int32)
    w = jnp.stack([wr, wi]).reshape(2, R, NT, NQ, PG, P, GS).transpose(2, 3, 1, 4, 6, 0, 5)
    w = jnp.tile(w.reshape(NT * NQ * R * PG * GS, 2, P).astype(BF16), (1, 1, PG))
    row_group = (iota(NT * NQ * R * PG * GS) // GS) % PG
    col_group = iota(PG * P) // P
    w_pair = jnp.where((row_group[:, None] == col_group[None, :])[:, None, :], w, 0).reshape(
        NT, NQ, R * SSM_PAIR_CH, SSM_PAIR_WIDTH)
    w0 = jnp.tile(w_pair[:, :, :SSM_PAIR_CH, None, :], (1, 1, 1, NQ, 1))
    w0 = jnp.where((iota(NQ)[:, None, None, None] == iota(NQ)[None, None, :, None])[None], w0, 0).reshape(
        NT, LANES, NQ * SSM_PAIR_WIDTH)
    ct = jnp.stack([c_re, c_im]).astype(BF16).reshape(2, NT, NQ, PG, GS, P).transpose(1, 2, 0, 3, 5, 4)
    ct = jnp.tile(ct.reshape(NT, NQ, SSM_PAIR_WIDTH, GS), (1, 1, 1, LANES // GS))
    want_slot = iota(NQ)[:, None] * PG + ((iota(SSM_PAIR_WIDTH) // P) % PG)[None, :]
    c_pair = jnp.where(want_slot[:, :, None] == (iota(LANES) // GS)[None, None, :], ct, 0)

    def lam_rows(v):
        v = v.reshape(NT, NQ, SSM_PAIR_STATE)
        return jnp.concatenate([v, v], axis=-1).reshape(NT, 1, NQ * SSM_PAIR_WIDTH)

    return (w_pair, w0, c_pair, lam_rows(lam[0]), lam_rows(lam[1]), lam_rows(lam[2]), lam_rows(lam[3]))


def _glu_tail(x, u, y, d_ref, wglu_ref, bglu_ref, gpost_ref):
    y = y + d_ref[...] * u
    y = 0.5 * y * (1.0 + lax.erf(y * (2.0 ** -0.5)))
    z = _dot(y.astype(BF16), wglu_ref[...]) + bglu_ref[...]
    out = z[:, :D_MODEL] * jax.nn.sigmoid(z[:, D_MODEL:])
    return x + _rms(out, gpost_ref[...])


def _swap_re_im(a):
    tiles = a.shape[-1] // SSM_PAIR_STATE
    return jnp.concatenate(
        [a[:, (j ^ 1) * SSM_PAIR_STATE:((j ^ 1) + 1) * SSM_PAIR_STATE] for j in range(tiles)], axis=1)


def _ssm_kernel(x_ref, gpre_ref, gpost_ref, wp_ref, cp_ref, lamr_ref, lami_ref, d_ref, wglu_ref, bglu_ref,
                o_ref, hout_ref, ubuf_ref, uprev_ref, h2_ref, y_ref, carry_ref):
    t = pl.program_id(1)
    tm = x_ref.shape[0]
    NQ, PW = SSM_TILE_PAIRS, SSM_PAIR_WIDTH
    S2 = NQ * PW
    cur, nxt = t % 2, (t + 1) % 2

    @pl.when(t == 0)
    def _():
        uprev_ref[0] = jnp.zeros((SSM_LAGS, D_MODEL), F32)
        carry_ref[...] = jnp.zeros_like(carry_ref)

    x = x_ref[...]
    u = _rms(x, gpre_ref[...])
    ubuf_ref[0:SSM_LAGS, :] = uprev_ref[cur]
    ubuf_ref[SSM_LAGS:, :] = u

    slot = lax.broadcasted_iota(jnp.int32, (tm, LANES), 1) // SSM_PAIR_CH
    im_lane = (lax.broadcasted_iota(jnp.int32, (1, S2), 1) // SSM_PAIR_STATE) % 2 == 1
    sign = jnp.where(im_lane, -1.0, 1.0)

    for c in range(SSM_TILES):
        cols = slice(c * LANES, (c + 1) * LANES)
        h_ref = h2_ref.at[c % 2]
        lagged = [ubuf_ref[SSM_LAGS - k:SSM_LAGS - k + tm, cols] for k in range(SSM_LAGS)]
        for q in range(NQ):
            halves = []
            for half in range(SSM_LAGS // NQ):
                acc = None
                for m in range(NQ):
                    piece = lagged[half * NQ + m]
                    if m != q:
                        piece = pltpu.roll(piece, (SSM_PAIR_CH * (m - q)) % LANES, axis=1)
                    acc = piece if acc is None else jnp.where(slot == m, piece, acc)
                halves.append(acc)
            lhs = jnp.concatenate(halves, axis=1).astype(BF16)
            h_ref[:, q * PW:(q + 1) * PW] = _dot(lhs, wp_ref[c, q])
        lr = jnp.broadcast_to(lamr_ref[c], (SUBLANES, S2))
        li = jnp.broadcast_to(jnp.where(im_lane, lami_ref[c], -lami_ref[c]), (SUBLANES, S2))

        def slab(m, carry):
            r0 = pl.multiple_of(m * SUBLANES, SUBLANES)
            new = h_ref[pl.ds(r0, SUBLANES), :] + (lr * carry + li * _swap_re_im(carry))
            h_ref[pl.ds(r0, SUBLANES), :] = new
            return new

        carry_ref[c] = lax.fori_loop(0, tm // SUBLANES, slab, carry_ref[c], unroll=4)
        y = None
        for q in range(NQ):
            hq = (h_ref[:, q * PW:(q + 1) * PW] * sign[:, q * PW:(q + 1) * PW]).astype(BF16)
            yq = _dot(hq, cp_ref[c, q])
            y = yq if y is None else y + yq
        y_ref[:, cols] = y

    o_ref[...] = _glu_tail(x, u, y_ref[...], d_ref, wglu_ref, bglu_ref, gpost_ref)
    uprev_ref[nxt] = u[tm - SSM_LAGS:, :]

    @pl.when(t == pl.num_programs(1) - 1)
    def _():
        hout_ref[...] = carry_ref[...]


def _ssm_prompt(x, g_pre, g_post, tables, d_skip, w_glu, b_glu, layer, bsz, seq):
    w_pair, _, c_pair, _, _, lamk_re, lamk_im = tables
    tm = SSM_TOKENS
    per_seq = seq // tm
    S2 = 2 * SSM_TILE_STATE
    once = pl.Buffered(1)
    out, h_last = pl.pallas_call(
        _ssm_kernel,
        grid=(bsz, per_seq),
        in_specs=[
            pl.BlockSpec((tm, D_MODEL), lambda b, t: (b * per_seq + t, 0)),
            _const_spec((1, D_MODEL)),
            _const_spec((1, D_MODEL)),
            pl.BlockSpec(w_pair.shape, lambda b, t: (0, 0, 0, 0), pipeline_mode=once),
            pl.BlockSpec(c_pair.shape, lambda b, t: (0, 0, 0, 0), pipeline_mode=once),
            _const_spec(lamk_re.shape),
            _const_spec(lamk_im.shape),
            _const_spec((1, D_MODEL)),
            pl.BlockSpec((None, D_MODEL, 2 * D_MODEL), lambda b, t: (layer, 0, 0), pipeline_mode=once),
            _const_spec((1, 2 * D_MODEL)),
        ],
        out_specs=[
            pl.BlockSpec((tm, D_MODEL), lambda b, t: (b * per_seq + t, 0)),
            pl.BlockSpec((None, SSM_TILES, SUBLANES, S2), lambda b, t: (b, 0, 0, 0)),
        ],
        out_shape=[jax.ShapeDtypeStruct(x.shape, F32),
                   jax.ShapeDtypeStruct((bsz, SSM_TILES, SUBLANES, S2), F32)],
        scratch_shapes=[
            pltpu.VMEM((tm + SSM_LAGS, D_MODEL), F32),
            pltpu.VMEM((2, SSM_LAGS, D_MODEL), F32),
            pltpu.VMEM((2, tm, S2), F32),
            pltpu.VMEM((tm, D_MODEL), F32),
            pltpu.VMEM((SSM_TILES, SUBLANES, S2), F32),
        ],
        compiler_params=_params("arbitrary", "arbitrary"),
        name="ssm_prompt",
    )(x, g_pre, g_post, w_pair, c_pair, lamk_re, lamk_im, d_skip, w_glu, b_glu)
    h_last = h_last[:, :, SUBLANES - 1, :].reshape(bsz, SSM_TILES, SSM_TILE_PAIRS, 2, SSM_PAIR_STATE)
    shape = (bsz, SSM_GROUPS, SSM_STATE)
    return out, h_last[:, :, :, 0, :].reshape(shape), h_last[:, :, :, 1, :].reshape(shape)


def _ssm_sample_kernel(x_ref, sre_ref, sim_ref, gpre_ref, gpost_ref, w0_ref, cp_ref, lamr_ref, lami_ref,
                       d_ref, wglu_ref, bglu_ref, o_ref, nre_ref, nim_ref, y_ref):
    PS, PW = SSM_PAIR_STATE, SSM_PAIR_WIDTH
    x = x_ref[...]
    u = _rms(x, gpre_ref[...])
    for c in range(SSM_TILES):
        cols = slice(c * LANES, (c + 1) * LANES)
        bu = _dot(u[:, cols].astype(BF16), w0_ref[c])
        y = None
        for q in range(SSM_TILE_PAIRS):
            st = slice(c * SSM_TILE_STATE + q * PS, c * SSM_TILE_STATE + (q + 1) * PS)
            lr = lamr_ref[c][:, q * PW:q * PW + PS]
            li = lami_ref[c][:, q * PW:q * PW + PS]
            h0r, h0i = sre_ref[:, st], sim_ref[:, st]
            hr = bu[:, q * PW:q * PW + PS] + (lr * h0r - li * h0i)
            hi = bu[:, q * PW + PS:(q + 1) * PW] + (lr * h0i + li * h0r)
            nre_ref[:, st] = hr
            nim_ref[:, st] = hi
            yq = _dot(jnp.concatenate([hr, -hi], axis=1).astype(BF16), cp_ref[c, q])
            y = yq if y is None else y + yq
        y_ref[:, cols] = y
    o_ref[...] = _glu_tail(x, u, y_ref[...], d_ref, wglu_ref, bglu_ref, gpost_ref)


def _ssm_sample(x, state_re, state_im, g_pre, g_post, tables, d_skip, w_glu, b_glu, layer):
    _, w0, c_pair, lam1_re, lam1_im, _, _ = tables
    n = x.shape[0]
    flat = (n, SSM_GROUPS * SSM_STATE)
    st = jax.ShapeDtypeStruct(flat, F32)
    S2 = 2 * SSM_TILE_STATE
    out, nre, nim = pl.pallas_call(
        _ssm_sample_kernel,
        grid=(1,),
        in_specs=[
            _const_spec((n, D_MODEL)),
            _const_spec(flat),
            _const_spec(flat),
            _const_spec((1, D_MODEL)),
            _const_spec((1, D_MODEL)),
            _const_spec(w0.shape),
            _const_spec(c_pair.shape),
            _const_spec(lam1_re.shape),
            _const_spec(lam1_im.shape),
            _const_spec((1, D_MODEL)),
            pl.BlockSpec((None, D_MODEL, 2 * D_MODEL), lambda i: (layer, 0, 0)),
            _const_spec((1, 2 * D_MODEL)),
        ],
        out_specs=[_const_spec((n, D_MODEL)), _const_spec(flat), _const_spec(flat)],
        out_shape=[jax.ShapeDtypeStruct(x.shape, F32), st, st],
        scratch_shapes=[pltpu.VMEM((n, D_MODEL), F32)],
        compiler_params=_params("arbitrary"),
        name="ssm_sample",
    )(x, state_re.reshape(flat), state_im.reshape(flat), g_pre, g_post, w0, c_pair, lam1_re, lam1_im,
      d_skip, w_glu, b_glu)
    shape = (n, SSM_GROUPS, SSM_STATE)
    return out, nre.reshape(shape), nim.reshape(shape)


def kernel(x_prompt, x_sample, mem_prompt, state_ssm_re, state_ssm_im, cache_win_k, cache_win_v, cache_mem_k, cache_mem_v, norm_g, mem_norm_g, ffn_w_in, ffn_w_out, ssm_a_re, ssm_a_im, ssm_log_dt, ssm_b_re, ssm_b_im, ssm_c_re, ssm_c_im, ssm_d, ssm_w_glu, ssm_b_glu, attn_w_qkv, attn_b_qkv, attn_w_o, attn_sinks, ca_w_q, ca_w_kv, ca_w_o):
    bp, seq, _ = x_prompt.shape
    bs = x_sample.shape[0]
    xp = x_prompt.reshape(bp * seq, D_MODEL)
    xs = x_sample.reshape(bs, D_MODEL)

    gain = lambda i, r: norm_g[i, r].astype(F32).reshape(1, D_MODEL)
    ffn_w_in_b, ffn_w_out_b = ffn_w_in.astype(BF16), ffn_w_out.astype(BF16)
    ssm_w_glu_b = ssm_w_glu.astype(BF16)
    attn_w_qkv_b, attn_w_o_b = attn_w_qkv.astype(BF16), attn_w_o.astype(BF16)
    ca_w_q_b, ca_w_o_b = ca_w_q.astype(BF16), ca_w_o.astype(BF16)

    mem_k, mem_v = _mem_kv(mem_prompt, mem_norm_g.astype(F32), ca_w_kv.astype(BF16))
    cache_win_kt = cache_win_k.transpose(0, 1, 3, 4, 2)
    cache_win_vt = cache_win_v.transpose(0, 1, 3, 4, 2)

    ssm_re_p, ssm_im_p, ssm_re_s, ssm_im_s = [], [], [], []
    wk_p, wv_p, wk_s, wv_s = [], [], [], []
    for i in range(DEPTH):
        li = i // N_MIXERS
        xp = _half_ffn(xp, gain(i, 0), gain(i, 1), ffn_w_in_b, ffn_w_out_b, i, 0, FFN_TOKENS)
        xs = _half_ffn(xs, gain(i, 0), gain(i, 1), ffn_w_in_b, ffn_w_out_b, i, 0, FFN_TOKENS)
        if i % N_MIXERS == 0:
            tables = _ssm_tables(ssm_a_re[li], ssm_a_im[li], ssm_log_dt[li], ssm_b_re[li], ssm_b_im[li],
                                 ssm_c_re[li], ssm_c_im[li])
            d_skip = ssm_d[li].astype(F32).reshape(1, D_MODEL)
            b_glu = ssm_b_glu[li].astype(F32).reshape(1, 2 * D_MODEL)
            xp, hr_p, hi_p = _ssm_prompt(xp, gain(i, 2), gain(i, 3), tables, d_skip, ssm_w_glu_b, b_glu, li,
                                         bp, seq)
            xs, hr_s, hi_s = _ssm_sample(xs, state_ssm_re[li], state_ssm_im[li], gain(i, 2), gain(i, 3), tables,
                                         d_skip, ssm_w_glu_b, b_glu, li)
            ssm_re_p.append(hr_p); ssm_im_p.append(hi_p)
            ssm_re_s.append(hr_s); ssm_im_s.append(hi_s)
        else:
            b_qkv = attn_b_qkv[li].astype(F32).reshape(1, QKV_WIDTH)
            sinks = attn_sinks[li].astype(F32)
            xp, bk_p, bv_p = _window_attention_prompt(xp, gain(i, 2), gain(i, 3), attn_w_qkv_b, b_qkv,
                                                      attn_w_o_b, sinks, li, bp, seq)
            xs, bk_s, bv_s = _window_attention_sample(xs, gain(i, 2), gain(i, 3), attn_w_qkv_b, b_qkv, attn_w_o_b,
                                                      sinks, cache_win_kt, cache_win_vt, li)
            wk_p.append(bk_p.reshape(bp, WINDOW, N_KV_HEADS, HEAD_DIM))
            wv_p.append(bv_p.reshape(bp, WINDOW, N_KV_HEADS, HEAD_DIM))
            wk_s.append(bk_s); wv_s.append(bv_s)
        xp = _cross_attention_prompt(xp, gain(i, 4), gain(i, 5), ca_w_q_b, ca_w_o_b, mem_k, mem_v, i, seq)
        xs = _cross_attention_sample(xs, gain(i, 4), gain(i, 5), ca_w_q_b, ca_w_o_b, cache_mem_k, cache_mem_v, i)
        xp = _half_ffn(xp, gain(i, 6), gain(i, 7), ffn_w_in_b, ffn_w_out_b, i, 1, FFN_TOKENS)
        xs = _half_ffn(xs, gain(i, 6), gain(i, 7), ffn_w_in_b, ffn_w_out_b, i, 1, FFN_TOKENS)

    mem_shape = (DEPTH, bp, N_MEM, CA_HEADS, CA_HEAD_DIM)
    return (xp.reshape(bp, seq, D_MODEL), xs.reshape(bs, 1, D_MODEL),
            jnp.stack(ssm_re_p), jnp.stack(ssm_im_p), jnp.stack(wk_p), jnp.stack(wv_p),
            mem_k.reshape(mem_shape), mem_v.reshape(mem_shape),
            jnp.stack(ssm_re_s), jnp.stack(ssm_im_s),
            jnp.stack(wk_s).transpose(0, 1, 4, 2, 3), jnp.stack(wv_s).transpose(0, 1, 4, 2, 3))
```

```python
import functools
import math

import jax
import jax.numpy as jnp
from jax import lax
from jax.experimental import pallas as pl
from jax.experimental.pallas import tpu as pltpu

F32 = jnp.float32
BF16 = jnp.bfloat16

D_MODEL = 1024
DEPTH = 4
N_MIXERS = 2
SSM_GROUP = 16
SSM_GROUPS = D_MODEL // SSM_GROUP
SSM_STATE = 64
HEAD_DIM = 64
N_HEADS = D_MODEL // HEAD_DIM
N_KV_HEADS = 4
GQA = N_HEADS // N_KV_HEADS
WINDOW = 128
KV_WIDTH = N_KV_HEADS * HEAD_DIM
QKV_WIDTH = (N_HEADS + 2 * N_KV_HEADS) * HEAD_DIM
N_MEM = 256
CA_HEADS = 4
CA_HEAD_DIM = D_MODEL // CA_HEADS
CA_DIM_TILES = CA_HEAD_DIM // 128
CA_ROWS = CA_HEADS * CA_DIM_TILES
D_FF = ((8 * D_MODEL // 3 + 127) // 128) * 128
FFN_RES = 0.5
EPS = 1e-6
NEG = -1e30

SUBLANES = 8
LANES = 128
VMEM_LIMIT_BYTES = 56 * 1024 * 1024

SSM_LAGS = SUBLANES
SSM_TILE_GROUPS = LANES // SSM_GROUP
SSM_TILES = SSM_GROUPS // SSM_TILE_GROUPS
SSM_TILE_STATE = SSM_TILE_GROUPS * SSM_STATE
SSM_PAIR_GROUPS = 4
SSM_TILE_PAIRS = SSM_TILE_GROUPS // SSM_PAIR_GROUPS
SSM_PAIR_CH = SSM_PAIR_GROUPS * SSM_GROUP
SSM_PAIR_STATE = SSM_PAIR_GROUPS * SSM_STATE
SSM_PAIR_WIDTH = 2 * SSM_PAIR_STATE

FFN_CHUNK = 256
FFN_TOKENS = 1024
CA_TOKENS = 512
SWA_TOKENS = 512
SSM_TOKENS = 512
SAMPLE_CA_BLOCK = 4
SAMPLE_SWA_BLOCK = 16


def _params(*sem):
    return pltpu.CompilerParams(dimension_semantics=sem, vmem_limit_bytes=VMEM_LIMIT_BYTES)


def _rms(x, g):
    r = lax.rsqrt(jnp.mean(x * x, axis=-1, keepdims=True) + EPS)
    return x * r * g


def _dot(a, b):
    return jnp.dot(a, b, preferred_element_type=F32)


def _dot_nt(a, b):
    return lax.dot_general(a, b, (((1,), (1,)), ((), ())), preferred_element_type=F32)


def _const_spec(shape):
    zeros = (0,) * len(shape)
    return pl.BlockSpec(shape, lambda *_: zeros)


def _ffn_kernel(x_ref, gpre_ref, gpost_ref, wg_ref, wu_ref, wo_ref, o_ref, xn_ref, acc_ref):
    c = pl.program_id(1)

    @pl.when(c == 0)
    def _():
        xn_ref[...] = _rms(x_ref[...], gpre_ref[...]).astype(BF16)
        acc_ref[...] = jnp.zeros_like(acc_ref)

    xn = xn_ref[...]
    gate = _dot(xn, wg_ref[...])
    up = _dot(xn, wu_ref[...])
    h = (gate * jax.nn.sigmoid(gate) * up).astype(BF16)
    acc_ref[...] += _dot(h, wo_ref[...])

    @pl.when(c == pl.num_programs(1) - 1)
    def _():
        o_ref[...] = x_ref[...] + FFN_RES * _rms(acc_ref[...], gpost_ref[...])


def _half_ffn(x, g_pre, g_post, w_in, w_out, layer, half, tokens):
    n = x.shape[0]
    tm = min(tokens, n)
    n_chunks = D_FF // FFN_CHUNK
    return pl.pallas_call(
        _ffn_kernel,
        grid=(n // tm, n_chunks),
        in_specs=[
            pl.BlockSpec((tm, D_MODEL), lambda i, c: (i, 0)),
            _const_spec((1, D_MODEL)),
            _const_spec((1, D_MODEL)),
            pl.BlockSpec((None, None, D_MODEL, FFN_CHUNK), lambda i, c: (layer, half, 0, c)),
            pl.BlockSpec((None, None, D_MODEL, FFN_CHUNK), lambda i, c: (layer, half, 0, c + n_chunks)),
            pl.BlockSpec((None, None, FFN_CHUNK, D_MODEL), lambda i, c: (layer, half, c, 0)),
        ],
        out_specs=pl.BlockSpec((tm, D_MODEL), lambda i, c: (i, 0)),
        out_shape=jax.ShapeDtypeStruct((n, D_MODEL), F32),
        scratch_shapes=[pltpu.VMEM((tm, D_MODEL), BF16), pltpu.VMEM((tm, D_MODEL), F32)],
        compiler_params=_params("parallel", "arbitrary"),
        name="half_ffn",
    )(x, g_pre, g_post, w_in, w_in, w_out)


def _mem_kv_kernel(mem_ref, g_ref, w_ref, k_ref, v_ref):
    mn = _rms(mem_ref[...], g_ref[...]).astype(BF16)
    kv = _dot(mn, w_ref[...])
    k_ref[...] = kv[:, :D_MODEL]
    v_ref[...] = kv[:, D_MODEL:]


def _mem_kv(mem, g_mem, w_kv):
    bsz = mem.shape[0]
    out = jax.ShapeDtypeStruct((DEPTH, bsz, N_MEM, D_MODEL), F32)
    out_spec = pl.BlockSpec((None, None, N_MEM, D_MODEL), lambda l, b: (l, b, 0, 0))
    return pl.pallas_call(
        _mem_kv_kernel,
        grid=(DEPTH, bsz),
        in_specs=[
            pl.BlockSpec((None, N_MEM, D_MODEL), lambda l, b: (b, 0, 0)),
            pl.BlockSpec((None, 1, D_MODEL), lambda l, b: (l, 0, 0)),
            pl.BlockSpec((None, D_MODEL, 2 * D_MODEL), lambda l, b: (l, 0, 0)),
        ],
        out_specs=[out_spec, out_spec],
        out_shape=[out, out],
        compiler_params=_params("parallel", "parallel"),
        name="mem_kv",
    )(mem, g_mem.reshape(DEPTH, 1, D_MODEL), w_kv)


def _ca_kernel(x_ref, gpre_ref, gpost_ref, wq_ref, wo_ref, mk_ref, mv_ref, o_ref):
    x = x_ref[...]
    xn = _rms(x, gpre_ref[...]).astype(BF16)
    q = _dot(xn, wq_ref[...])
    heads = []
    for h in range(CA_HEADS):
        cols = slice(h * CA_HEAD_DIM, (h + 1) * CA_HEAD_DIM)
        s = _dot_nt(q[:, cols].astype(BF16), mk_ref[:, cols].astype(BF16)) * (CA_HEAD_DIM ** -0.5)
        p = jnp.exp(s - jnp.max(s, axis=-1, keepdims=True))
        den = jnp.sum(p, axis=-1, keepdims=True)
        heads.append(_dot(p.astype(BF16), mv_ref[:, cols].astype(BF16)) / den)
    o = jnp.concatenate(heads, axis=-1).astype(BF16)
    o_ref[...] = x + _rms(_dot(o, wo_ref[...]), gpost_ref[...])


def _cross_attention_prompt(x, g_pre, g_post, w_q, w_o, mk, mv, layer, seq):
    n = x.shape[0]
    tm = CA_TOKENS
    per_seq = seq // tm
    mem_spec = pl.BlockSpec((None, None, N_MEM, D_MODEL), lambda i: (layer, i // per_seq, 0, 0))
    w_spec = pl.BlockSpec((None, D_MODEL, D_MODEL), lambda i: (layer, 0, 0))
    return pl.pallas_call(
        _ca_kernel,
        grid=(n // tm,),
        in_specs=[
            pl.BlockSpec((tm, D_MODEL), lambda i: (i, 0)),
            _const_spec((1, D_MODEL)),
            _const_spec((1, D_MODEL)),
            w_spec,
            w_spec,
            mem_spec,
            mem_spec,
        ],
        out_specs=pl.BlockSpec((tm, D_MODEL), lambda i: (i, 0)),
        out_shape=jax.ShapeDtypeStruct((n, D_MODEL), F32),
        compiler_params=_params("parallel"),
        name="cross_attn_prompt",
    )(x, g_pre, g_post, w_q, w_o, mk, mv)


def _stack_rows(pieces):
    n, w = len(pieces), pieces[0].shape[1]
    sub = lax.broadcasted_iota(jnp.int32, (n, w), 0)
    out = jnp.broadcast_to(pieces[0], (n, w))
    for r in range(1, n):
        out = jnp.where(sub == r, jnp.broadcast_to(pieces[r], (n, w)), out)
    return out


def _ca_sample_kernel(x_ref, gpre_ref, gpost_ref, wq_ref, wo_ref, mk_ref, mv_ref, o_ref, q_ref, att_ref):
    i = pl.program_id(0)

    @pl.when(i == 0)
    def _():
        xn = _rms(x_ref[...], gpre_ref[...]).astype(BF16)
        q_ref[...] = _dot(xn, wq_ref[...]) * (CA_HEAD_DIM ** -0.5)

    for b in range(SAMPLE_CA_BLOCK):
        row = i * SAMPLE_CA_BLOCK + b
        q = q_ref[pl.ds(row, 1), :]
        piece = lambda a, r: a[:, (r % CA_HEADS) * CA_HEAD_DIM + (r // CA_HEADS) * LANES:][:, :LANES]
        q8 = _stack_rows([piece(q, r) for r in range(CA_ROWS)])
        part = jnp.sum(mk_ref[b] * q8[None], axis=-1, keepdims=True)
        s = part
        for j in range(1, CA_DIM_TILES):
            s = s + pltpu.roll(part, j * CA_HEADS, axis=1)
        p = jnp.exp(s - jnp.max(s, axis=0, keepdims=True))
        den = jnp.sum(p, axis=0)
        o8 = jnp.sum(p * mv_ref[b], axis=0) / den
        att_ref[pl.ds(row, 1), :] = jnp.concatenate(
            [o8[j * CA_HEADS + h:j * CA_HEADS + h + 1, :] for h in range(CA_HEADS) for j in range(CA_DIM_TILES)],
            axis=-1)

    @pl.when(i == pl.num_programs(0) - 1)
    def _():
        y = _dot(att_ref[...].astype(BF16), wo_ref[...])
        o_ref[...] = x_ref[...] + _rms(y, gpost_ref[...])


def _cross_attention_sample(x, g_pre, g_post, w_q, w_o, mk, mv, layer):
    n = x.shape[0]
    mem_spec = pl.BlockSpec((None, SAMPLE_CA_BLOCK, N_MEM, CA_ROWS, LANES), lambda i: (layer, i, 0, 0, 0))
    w_spec = pl.BlockSpec((None, D_MODEL, D_MODEL), lambda i: (layer, 0, 0))
    return pl.pallas_call(
        _ca_sample_kernel,
        grid=(n // SAMPLE_CA_BLOCK,),
        in_specs=[
            _const_spec((n, D_MODEL)),
            _const_spec((1, D_MODEL)),
            _const_spec((1, D_MODEL)),
            w_spec,
            w_spec,
            mem_spec,
            mem_spec,
        ],
        out_specs=_const_spec((n, D_MODEL)),
        out_shape=jax.ShapeDtypeStruct((n, D_MODEL), F32),
        scratch_shapes=[pltpu.VMEM((n, D_MODEL), F32), pltpu.VMEM((n, D_MODEL), F32)],
        compiler_params=_params("arbitrary"),
        name="cross_attn_sample",
    )(x, g_pre, g_post, w_q, w_o, mk, mv)


def _swa_kernel(sink_ref, x_ref, gpre_ref, gpost_ref, wqkv_ref, bqkv_ref, wo_ref,
                o_ref, kout_ref, vout_ref, kprev_ref, vprev_ref, q_ref):
    t = pl.program_id(1)
    tm = x_ref.shape[0]
    x = x_ref[...]
    xn = _rms(x, gpre_ref[...]).astype(BF16)
    qkv = _dot(xn, wqkv_ref[...]) + bqkv_ref[...]
    q_ref[...] = (qkv[:, :N_HEADS * HEAD_DIM] * (HEAD_DIM ** -0.5)).astype(BF16)
    k = qkv[:, N_HEADS * HEAD_DIM:N_HEADS * HEAD_DIM + KV_WIDTH]
    v = qkv[:, N_HEADS * HEAD_DIM + KV_WIDTH:]

    cur, nxt = t % 2, (t + 1) % 2

    @pl.when(t == 0)
    def _():
        kprev_ref[0] = jnp.zeros((WINDOW, KV_WIDTH), BF16)
        vprev_ref[0] = jnp.zeros((WINDOW, KV_WIDTH), BF16)

    kall = jnp.concatenate([kprev_ref[cur], k.astype(BF16)], axis=0)
    vall = jnp.concatenate([vprev_ref[cur], v.astype(BF16)], axis=0)

    row = lax.broadcasted_iota(jnp.int32, (GQA * WINDOW, 2 * WINDOW), 0)
    qi = row & (WINDOW - 1)
    kj = lax.broadcasted_iota(jnp.int32, (GQA * WINDOW, 2 * WINDOW), 1)
    band = (kj >= qi) & (kj <= qi + WINDOW)
    row_g = lax.broadcasted_iota(jnp.int32, (GQA * WINDOW, 1), 0) // WINDOW

    for n in range(tm // WINDOW):
        rows = slice(n * WINDOW, (n + 1) * WINDOW)
        kk = kall[n * WINDOW:(n + 2) * WINDOW, :]
        vv = vall[n * WINDOW:(n + 2) * WINDOW, :]
        valid = band & (kj >= WINDOW - (t * tm + n * WINDOW)) if n == 0 else band
        kv_cols = [slice(kh * HEAD_DIM, (kh + 1) * HEAD_DIM) for kh in range(N_KV_HEADS)]
        scores, sinks = [], []
        for kh in range(N_KV_HEADS):
            q4 = jnp.concatenate(
                [q_ref[rows, (kh * GQA + g) * HEAD_DIM:(kh * GQA + g + 1) * HEAD_DIM] for g in range(GQA)],
                axis=0)
            sink = jnp.full((GQA * WINDOW, 1), sink_ref[kh * GQA], F32)
            for g in range(1, GQA):
                sink = jnp.where(row_g == g, sink_ref[kh * GQA + g], sink)
            sinks.append(sink)
            scores.append(jnp.where(valid, _dot_nt(q4, kk[:, kv_cols[kh]]), NEG))
        maxes = [jnp.maximum(jnp.max(s, axis=-1, keepdims=True), sink) for s, sink in zip(scores, sinks)]
        probs = [jnp.exp(s - m) for s, m in zip(scores, maxes)]
        dens = [jnp.sum(p, axis=-1, keepdims=True) + jnp.exp(sink - m) for p, sink, m in zip(probs, sinks, maxes)]
        outs = [_dot(p.astype(BF16), vv[:, kv_cols[kh]]) for kh, p in enumerate(probs)]
        heads = []
        for kh in range(N_KV_HEADS):
            o4 = outs[kh] / dens[kh]
            heads += [o4[g * WINDOW:(g + 1) * WINDOW] for g in range(GQA)]
        att = jnp.concatenate(heads, axis=1).astype(BF16)
        o_ref[rows, :] = x_ref[rows, :] + _rms(_dot(att, wo_ref[...]), gpost_ref[...])

    kprev_ref[nxt] = k[tm - WINDOW:, :].astype(BF16)
    vprev_ref[nxt] = v[tm - WINDOW:, :].astype(BF16)

    kout_ref[...] = k[tm - WINDOW:, :]
    vout_ref[...] = v[tm - WINDOW:, :]


def _window_attention_prompt(x, g_pre, g_post, w_qkv, b_qkv, w_o, sinks, layer, bsz, seq):
    tm = SWA_TOKENS
    per_seq = seq // tm
    win_spec = pl.BlockSpec((None, WINDOW, KV_WIDTH), lambda b, t, *_: (b, 0, 0))
    win_shape = jax.ShapeDtypeStruct((bsz, WINDOW, KV_WIDTH), F32)
    grid_spec = pltpu.PrefetchScalarGridSpec(
        num_scalar_prefetch=1,
        grid=(bsz, per_seq),
        in_specs=[
            pl.BlockSpec((tm, D_MODEL), lambda b, t, *_: (b * per_seq + t, 0)),
            _const_spec((1, D_MODEL)),
            _const_spec((1, D_MODEL)),
            pl.BlockSpec((None, D_MODEL, QKV_WIDTH), lambda b, t, *_: (layer, 0, 0)),
            _const_spec((1, QKV_WIDTH)),
            pl.BlockSpec((None, N_HEADS * HEAD_DIM, D_MODEL), lambda b, t, *_: (layer, 0, 0)),
        ],
        out_specs=[pl.BlockSpec((tm, D_MODEL), lambda b, t, *_: (b * per_seq + t, 0)), win_spec, win_spec],
        scratch_shapes=[
            pltpu.VMEM((2, WINDOW, KV_WIDTH), BF16),
            pltpu.VMEM((2, WINDOW, KV_WIDTH), BF16),
            pltpu.VMEM((tm, N_HEADS * HEAD_DIM), BF16),
        ],
    )
    return pl.pallas_call(
        _swa_kernel,
        grid_spec=grid_spec,
        out_shape=[jax.ShapeDtypeStruct(x.shape, F32), win_shape, win_shape],
        compiler_params=_params("arbitrary", "arbitrary"),
        name="window_attn_prompt",
    )(sinks, x, g_pre, g_post, w_qkv, b_qkv, w_o)


def _swa_sample_kernel(sink_ref, x_ref, gpre_ref, gpost_ref, wqkv_ref, bqkv_ref, wo_ref, ck_ref, cv_ref,
                       o_ref, nk_ref, nv_ref, qkvt_ref, attt_ref, blkt_ref):
    i = pl.program_id(0)
    n = x_ref.shape[0]
    k0 = N_HEADS * HEAD_DIM
    v0 = k0 + KV_WIDTH

    @pl.when(i == 0)
    def _():
        xn = _rms(x_ref[...], gpre_ref[...]).astype(BF16)
        qkvt_ref[...] = (_dot(xn, wqkv_ref[...]) + bqkv_ref[...]).T
        attt_ref[...] = jnp.zeros_like(attt_ref)
        blkt_ref[...] = jnp.zeros_like(blkt_ref)

    base = i * SAMPLE_SWA_BLOCK
    qkvt = pltpu.roll(qkvt_ref[...], (n - base) % n, axis=1)
    newest = lax.broadcasted_iota(jnp.int32, (KV_WIDTH, WINDOW), 1) == WINDOW - 1
    heads3 = lambda a: a.reshape(N_HEADS, HEAD_DIM, a.shape[-1])
    per_q_head = lambda a: jnp.concatenate(
        [a[(h // GQA) * HEAD_DIM:(h // GQA + 1) * HEAD_DIM] for h in range(N_HEADS)], axis=0)
    sink = sink_ref[...]
    for b in range(SAMPLE_SWA_BLOCK):
        col = qkvt[:, b:b + 1]
        q = col[0:k0] * (HEAD_DIM ** -0.5)
        k_new, v_new = col[k0:v0], col[v0:]
        kt = ck_ref[b].reshape(KV_WIDTH, WINDOW)
        vt = cv_ref[b].reshape(KV_WIDTH, WINDOW)
        s = jnp.sum(heads3(per_q_head(kt) * q), axis=1, keepdims=True)
        s_new = jnp.sum(heads3(per_q_head(k_new) * q), axis=1, keepdims=True)
        m = jnp.maximum(jnp.maximum(jnp.max(s, axis=2, keepdims=True), s_new), sink)
        p = jnp.exp(s - m)
        p_new = jnp.exp(s_new - m)
        den = jnp.sum(p, axis=2, keepdims=True) + p_new + jnp.exp(sink - m)
        o = jnp.sum(heads3(per_q_head(vt)) * p, axis=2, keepdims=True)
        o = (o + p_new * heads3(per_q_head(v_new))) / den
        blkt_ref[:, b:b + 1] = o.reshape(N_HEADS * HEAD_DIM, 1)
        nk_ref[b] = jnp.where(newest, k_new, pltpu.roll(kt, WINDOW - 1, axis=1)).reshape(nk_ref.shape[1:])
        nv_ref[b] = jnp.where(newest, v_new, pltpu.roll(vt, WINDOW - 1, axis=1)).reshape(nv_ref.shape[1:])

    lane = lax.broadcasted_iota(jnp.int32, attt_ref.shape, 1)
    mine = (lane >= base) & (lane < base + SAMPLE_SWA_BLOCK)
    attt_ref[...] = jnp.where(mine, pltpu.roll(blkt_ref[...], base, axis=1), attt_ref[...])

    @pl.when(i == pl.num_programs(0) - 1)
    def _():
        y = _dot(attt_ref[...].T.astype(BF16), wo_ref[...])
        o_ref[...] = x_ref[...] + _rms(y, gpost_ref[...])


def _window_attention_sample(x, g_pre, g_post, w_qkv, b_qkv, w_o, sinks, cache_k, cache_v, layer):
    n = x.shape[0]
    assert n == LANES, "the sample kernel keeps one sample per lane"
    blk = SAMPLE_SWA_BLOCK
    cache_spec = pl.BlockSpec((None, blk, N_KV_HEADS, HEAD_DIM, WINDOW), lambda i, *_: (layer, i, 0, 0, 0))
    win_spec = pl.BlockSpec((blk, N_KV_HEADS, HEAD_DIM, WINDOW), lambda i, *_: (i, 0, 0, 0))
    win_shape = jax.ShapeDtypeStruct((n, N_KV_HEADS, HEAD_DIM, WINDOW), F32)
    return pl.pallas_call(
        _swa_sample_kernel,
        grid=(n // blk,),
        in_specs=[
            _const_spec((N_HEADS, 1, 1)),
            _const_spec((n, D_MODEL)),
            _const_spec((1, D_MODEL)),
            _const_spec((1, D_MODEL)),
            pl.BlockSpec((None, D_MODEL, QKV_WIDTH), lambda i, *_: (layer, 0, 0)),
            _const_spec((1, QKV_WIDTH)),
            pl.BlockSpec((None, N_HEADS * HEAD_DIM, D_MODEL), lambda i, *_: (layer, 0, 0)),
            cache_spec,
            cache_spec,
        ],
        out_specs=[_const_spec((n, D_MODEL)), win_spec, win_spec],
        out_shape=[jax.ShapeDtypeStruct(x.shape, F32), win_shape, win_shape],
        scratch_shapes=[pltpu.VMEM((QKV_WIDTH, n), F32), pltpu.VMEM((N_HEADS * HEAD_DIM, n), F32),
                        pltpu.VMEM((N_HEADS * HEAD_DIM, n), F32)],
        compiler_params=_params("arbitrary"),
        name="window_attn_sample",
    )(sinks.reshape(N_HEADS, 1, 1), x, g_pre, g_post, w_qkv, b_qkv, w_o, cache_k, cache_v)


def _ssm_prep_kernel(ar_ref, ai_ref, dt_ref, br_ref, bi_ref, wr_ref, wi_ref, lam_ref):
    ar, ai, dt = ar_ref[...], ai_ref[...], jnp.exp(dt_ref[...])
    mag = jnp.exp(ar * dt)
    lr, li = mag * jnp.cos(ai * dt), mag * jnp.sin(ai * dt)
    den = ar * ar + ai * ai
    nr, ni = lr - 1.0, li
    zr = (nr * ar + ni * ai) / den
    zi = (ni * ar - nr * ai) / den
    br, bi = br_ref[...], bi_ref[...]
    wr = zr * br - zi * bi
    wi = zr * bi + zi * br
    pr, pi = lr, li
    for k in range(SSM_LAGS):
        wr_ref[k] = wr
        wi_ref[k] = wi
        wr, wi = lr * wr - li * wi, lr * wi + li * wr
        if k > 0:
            pr, pi = lr * pr - li * pi, lr * pi + li * pr
    lam_ref[0] = lr
    lam_ref[1] = li
    lam_ref[2] = pr
    lam_ref[3] = pi


def _ssm_expand_kernel(w_ref, o_ref):
    rows = w_ref.shape[0]
    P, PG = SSM_STATE, SSM_PAIR_GROUPS
    w = w_ref[...]
    row_group = (lax.broadcasted_iota(jnp.int32, (rows, PG * P), 0) // SSM_GROUP) % PG
    col_group = lax.broadcasted_iota(jnp.int32, (rows, PG * P), 1) // P
    own = row_group == col_group
    parts = [jnp.where(own, jnp.concatenate([w[:, part * P:(part + 1) * P]] * PG, axis=1), 0.0) for part in range(2)]
    o_ref[...] = jnp.concatenate(parts, axis=1).astype(BF16)


def _ssm_tables(a_re, a_im, log_dt, b_re, b_im, c_re, c_im):
    G, P, GS, R, NT = SSM_GROUPS, SSM_STATE, SSM_GROUP, SSM_LAGS, SSM_TILES
    rows = G * P
    dense = (rows * GS // LANES, LANES)
    spread = lambda a: jnp.broadcast_to(a.astype(F32)[:, :, None], (G, P, GS)).reshape(dense)
    log_dt = jnp.broadcast_to(log_dt.astype(F32)[:, None], (G, P))
    wr, wi, lam = pl.pallas_call(
        _ssm_prep_kernel,
        out_shape=[jax.ShapeDtypeStruct((R,) + dense, F32), jax.ShapeDtypeStruct((R,) + dense, F32),
                   jax.ShapeDtypeStruct((4,) + dense, F32)],
        name="ssm_prep",
    )(spread(a_re), spread(a_im), spread(log_dt), b_re.astype(F32).reshape(dense), b_im.astype(F32).reshape(dense))
    wr, wi = wr.reshape(R, rows, GS), wi.reshape(R, rows, GS)
    lam = lam.reshape(4, rows, GS)[:, :, 0]
    NQ, PG = SSM_TILE_PAIRS, SSM_PAIR_GROUPS
    iota = lambda n: jnp.arange(n, dtype=jnp.int32)
    w = jnp.stack([wr, wi]).reshape(2, R, NT, NQ, PG, P, GS).transpose(2, 3, 1, 4, 6, 0, 5)
    w_pair = pl.pallas_call(
        _ssm_expand_kernel,
        grid=(NT * NQ,),
        in_specs=[pl.BlockSpec((None, R * SSM_PAIR_CH, 2 * P), lambda i: (i, 0, 0))],
        out_specs=pl.BlockSpec((None, R * SSM_PAIR_CH, SSM_PAIR_WIDTH), lambda i: (i, 0, 0)),
        out_shape=jax.ShapeDtypeStruct((NT * NQ, R * SSM_PAIR_CH, SSM_PAIR_WIDTH), BF16),
        compiler_params=_params("parallel"),
        name="ssm_expand",
    )(w.reshape(NT * NQ, R * SSM_PAIR_CH, 2 * P)).reshape(NT, NQ, R * SSM_PAIR_CH, SSM_PAIR_WIDTH)
    w0 = jnp.tile(w_pair[:, :, :SSM_PAIR_CH, None, :], (1, 1, 1, NQ, 1))
    w0 = jnp.where((iota(NQ)[:, None, None, None] == iota(NQ)[None, None, :, None])[None], w0, 0).reshape(
        NT, LANES, NQ * SSM_PAIR_WIDTH)
    ct = jnp.stack([c_re, c_im]).astype(BF16).reshape(2, NT, NQ, PG, GS, P).transpose(1, 2, 0, 3, 5, 4)
    ct = jnp.tile(ct.reshape(NT, NQ, SSM_PAIR_WIDTH, GS), (1, 1, 1, LANES // GS))
    want_slot = iota(NQ)[:, None] * PG + ((iota(SSM_PAIR_WIDTH) // P) % PG)[None, :]
    c_pair = jnp.where(want_slot[:, :, None] == (iota(LANES) // GS)[None, None, :], ct, 0)

    def lam_rows(v):
        v = v.reshape(NT, NQ, SSM_PAIR_STATE)
        return jnp.concatenate([v, v], axis=-1).reshape(NT, 1, NQ * SSM_PAIR_WIDTH)

    return (w_pair, w0, c_pair, lam_rows(lam[0]), lam_rows(lam[1]), lam_rows(lam[2]), lam_rows(lam[3]))


def _glu_tail(x, u, y, d_ref, wglu_ref, bglu_ref, gpost_ref):
    y = y + d_ref[...] * u
    y = 0.5 * y * (1.0 + lax.erf(y * (2.0 ** -0.5)))
    z = _dot(y.astype(BF16), wglu_ref[...]) + bglu_ref[...]
    out = z[:, :D_MODEL] * jax.nn.sigmoid(z[:, D_MODEL:])
    return x + _rms(out, gpost_ref[...])


def _swap_re_im(a):
    tiles = a.shape[-1] // SSM_PAIR_STATE
    return jnp.concatenate(
        [a[:, (j ^ 1) * SSM_PAIR_STATE:((j ^ 1) + 1) * SSM_PAIR_STATE] for j in range(tiles)], axis=1)


def _ssm_kernel(x_ref, gpre_ref, gpost_ref, wp_ref, cp_ref, lamr_ref, lami_ref, d_ref, wglu_ref, bglu_ref,
                o_ref, hout_ref, ubuf_ref, uprev_ref, h2_ref, y_ref, carry_ref):
    t = pl.program_id(1)
    tm = x_ref.shape[0]
    NQ, PS, PW = SSM_TILE_PAIRS, SSM_PAIR_STATE, SSM_PAIR_WIDTH
    S2 = NQ * PW
    cur, nxt = t % 2, (t + 1) % 2

    @pl.when(t == 0)
    def _():
        uprev_ref[0] = jnp.zeros((SSM_LAGS, D_MODEL), F32)
        carry_ref[...] = jnp.zeros_like(carry_ref)

    x = x_ref[...]
    u = _rms(x, gpre_ref[...])
    ubuf_ref[0:SSM_LAGS, :] = uprev_ref[cur]
    ubuf_ref[SSM_LAGS:, :] = u

    slot = lax.broadcasted_iota(jnp.int32, (tm, LANES), 1) // SSM_PAIR_CH
    im_lane = (lax.broadcasted_iota(jnp.int32, (1, S2), 1) // SSM_PAIR_STATE) % 2 == 1

    for c in range(SSM_TILES):
        cols = slice(c * LANES, (c + 1) * LANES)
        h_ref = h2_ref.at[c % 2]
        lagged = [ubuf_ref[SSM_LAGS - k:SSM_LAGS - k + tm, cols] for k in range(SSM_LAGS)]
        for q in range(NQ):
            halves = []
            for half in range(SSM_LAGS // NQ):
                acc = None
                for m in range(NQ):
                    piece = lagged[half * NQ + m]
                    if m != q:
                        piece = pltpu.roll(piece, (SSM_PAIR_CH * (m - q)) % LANES, axis=1)
                    acc = piece if acc is None else jnp.where(slot == m, piece, acc)
                halves.append(acc)
            lhs = jnp.concatenate(halves, axis=1).astype(BF16)
            h_ref[:, q * PW:(q + 1) * PW] = _dot(lhs, wp_ref[c, q])
        lr = jnp.broadcast_to(lamr_ref[c], (SUBLANES, S2))
        li = jnp.broadcast_to(jnp.where(im_lane, lami_ref[c], -lami_ref[c]), (SUBLANES, S2))

        def slab(m, carry):
            r0 = pl.multiple_of(m * SUBLANES, SUBLANES)
            new = h_ref[pl.ds(r0, SUBLANES), :] + (lr * carry + li * _swap_re_im(carry))
            h_ref[pl.ds(r0, SUBLANES), :] = new
            return new

        carry_ref[c] = lax.fori_loop(0, tm // SUBLANES, slab, carry_ref[c], unroll=True)
        y = None
        for q in range(NQ):
            h_re = h_ref[:, q * PW:q * PW + PS].astype(BF16)
            h_im = h_ref[:, q * PW + PS:(q + 1) * PW].astype(BF16)
            yq = _dot(h_re, cp_ref[c, q, 0:PS, :]) - _dot(h_im, cp_ref[c, q, PS:PW, :])
            y = yq if y is None else y + yq
        y_ref[:, cols] = y

    o_ref[...] = _glu_tail(x, u, y_ref[...], d_ref, wglu_ref, bglu_ref, gpost_ref)
    uprev_ref[nxt] = u[tm - SSM_LAGS:, :]

    @pl.when(t == pl.num_programs(1) - 1)
    def _():
        hout_ref[...] = carry_ref[...]


def _ssm_prompt(x, g_pre, g_post, tables, d_skip, w_glu, b_glu, layer, bsz, seq):
    w_pair, _, c_pair, _, _, lamk_re, lamk_im = tables
    tm = SSM_TOKENS
    per_seq = seq // tm
    S2 = 2 * SSM_TILE_STATE
    once = pl.Buffered(1)
    out, h_last = pl.pallas_call(
        _ssm_kernel,
        grid=(bsz, per_seq),
        in_specs=[
            pl.BlockSpec((tm, D_MODEL), lambda b, t: (b * per_seq + t, 0)),
            _const_spec((1, D_MODEL)),
            _const_spec((1, D_MODEL)),
            pl.BlockSpec(w_pair.shape, lambda b, t: (0, 0, 0, 0), pipeline_mode=once),
            pl.BlockSpec(c_pair.shape, lambda b, t: (0, 0, 0, 0), pipeline_mode=once),
            _const_spec(lamk_re.shape),
            _const_spec(lamk_im.shape),
            _const_spec((1, D_MODEL)),
            pl.BlockSpec((None, D_MODEL, 2 * D_MODEL), lambda b, t: (layer, 0, 0), pipeline_mode=once),
            _const_spec((1, 2 * D_MODEL)),
        ],
        out_specs=[
            pl.BlockSpec((tm, D_MODEL), lambda b, t: (b * per_seq + t, 0)),
            pl.BlockSpec((None, SSM_TILES, SUBLANES, S2), lambda b, t: (b, 0, 0, 0)),
        ],
        out_shape=[jax.ShapeDtypeStruct(x.shape, F32),
                   jax.ShapeDtypeStruct((bsz, SSM_TILES, SUBLANES, S2), F32)],
        scratch_shapes=[
            pltpu.VMEM((tm + SSM_LAGS, D_MODEL), F32),
            pltpu.VMEM((2, SSM_LAGS, D_MODEL), F32),
            pltpu.VMEM((2, tm, S2), F32),
            pltpu.VMEM((tm, D_MODEL), F32),
            pltpu.VMEM((SSM_TILES, SUBLANES, S2), F32),
        ],
        compiler_params=_params("arbitrary", "arbitrary"),
        name="ssm_prompt",
    )(x, g_pre, g_post, w_pair, c_pair, lamk_re, lamk_im, d_skip, w_glu, b_glu)
    h_last = h_last[:, :, SUBLANES - 1, :].reshape(bsz, SSM_TILES, SSM_TILE_PAIRS, 2, SSM_PAIR_STATE)
    shape = (bsz, SSM_GROUPS, SSM_STATE)
    return out, h_last[:, :, :, 0, :].reshape(shape), h_last[:, :, :, 1, :].reshape(shape)


def _ssm_sample_kernel(x_ref, sre_ref, sim_ref, gpre_ref, gpost_ref, w0_ref, cp_ref, lamr_ref, lami_ref,
                       d_ref, wglu_ref, bglu_ref, o_ref, nre_ref, nim_ref, y_ref):
    PS, PW = SSM_PAIR_STATE, SSM_PAIR_WIDTH
    x = x_ref[...]
    u = _rms(x, gpre_ref[...])
    for c in range(SSM_TILES):
        cols = slice(c * LANES, (c + 1) * LANES)
        bu = _dot(u[:, cols].astype(BF16), w0_ref[c])
        y = None
        for q in range(SSM_TILE_PAIRS):
            st = slice(c * SSM_TILE_STATE + q * PS, c * SSM_TILE_STATE + (q + 1) * PS)
            lr = lamr_ref[c][:, q * PW:q * PW + PS]
            li = lami_ref[c][:, q * PW:q * PW + PS]
            h0r, h0i = sre_ref[:, st], sim_ref[:, st]
            hr = bu[:, q * PW:q * PW + PS] + (lr * h0r - li * h0i)
            hi = bu[:, q * PW + PS:(q + 1) * PW] + (lr * h0i + li * h0r)
            nre_ref[:, st] = hr
            nim_ref[:, st] = hi
            yq = _dot(jnp.concatenate([hr, -hi], axis=1).astype(BF16), cp_ref[c, q])
            y = yq if y is None else y + yq
        y_ref[:, cols] = y
    o_ref[...] = _glu_tail(x, u, y_ref[...], d_ref, wglu_ref, bglu_ref, gpost_ref)


def _ssm_sample(x, state_re, state_im, g_pre, g_post, tables, d_skip, w_glu, b_glu, layer):
    _, w0, c_pair, lam1_re, lam1_im, _, _ = tables
    n = x.shape[0]
    flat = (n, SSM_GROUPS * SSM_STATE)
    st = jax.ShapeDtypeStruct(flat, F32)
    S2 = 2 * SSM_TILE_STATE
    out, nre, nim = pl.pallas_call(
        _ssm_sample_kernel,
        grid=(1,),
        in_specs=[
            _const_spec((n, D_MODEL)),
            _const_spec(flat),
            _const_spec(flat),
            _const_spec((1, D_MODEL)),
            _const_spec((1, D_MODEL)),
            _const_spec(w0.shape),
            _const_spec(c_pair.shape),
            _const_spec(lam1_re.shape),
            _const_spec(lam1_im.shape),
            _const_spec((1, D_MODEL)),
            pl.BlockSpec((None, D_MODEL, 2 * D_MODEL), lambda i: (layer, 0, 0)),
            _const_spec((1, 2 * D_MODEL)),
        ],
        out_specs=[_const_spec((n, D_MODEL)), _const_spec(flat), _const_spec(flat)],
        out_shape=[jax.ShapeDtypeStruct(x.shape, F32), st, st],
        scratch_shapes=[pltpu.VMEM((n, D_MODEL), F32)],
        compiler_params=_params("arbitrary"),
        name="ssm_sample",
    )(x, state_re.reshape(flat), state_im.reshape(flat), g_pre, g_post, w0, c_pair, lam1_re, lam1_im,
      d_skip, w_glu, b_glu)
    shape = (n, SSM_GROUPS, SSM_STATE)
    return out, nre.reshape(shape), nim.reshape(shape)


def kernel(x_prompt, x_sample, mem_prompt, state_ssm_re, state_ssm_im, cache_win_k, cache_win_v, cache_mem_k, cache_mem_v, norm_g, mem_norm_g, ffn_w_in, ffn_w_out, ssm_a_re, ssm_a_im, ssm_log_dt, ssm_b_re, ssm_b_im, ssm_c_re, ssm_c_im, ssm_d, ssm_w_glu, ssm_b_glu, attn_w_qkv, attn_b_qkv, attn_w_o, attn_sinks, ca_w_q, ca_w_kv, ca_w_o):
    bp, seq, _ = x_prompt.shape
    bs = x_sample.shape[0]
    xp = x_prompt.reshape(bp * seq, D_MODEL)
    xs = x_sample.reshape(bs, D_MODEL)

    gain = lambda i, r: norm_g[i, r].astype(F32).reshape(1, D_MODEL)
    ffn_w_in_b, ffn_w_out_b = ffn_w_in.astype(BF16), ffn_w_out.astype(BF16)
    ssm_w_glu_b = ssm_w_glu.astype(BF16)
    attn_w_qkv_b, attn_w_o_b = attn_w_qkv.astype(BF16), attn_w_o.astype(BF16)
    ca_w_q_b, ca_w_o_b = ca_w_q.astype(BF16), ca_w_o.astype(BF16)

    mem_k, mem_v = _mem_kv(mem_prompt, mem_norm_g.astype(F32), ca_w_kv.astype(BF16))
    cache_win_kt = cache_win_k.transpose(0, 1, 3, 4, 2)
    cache_win_vt = cache_win_v.transpose(0, 1, 3, 4, 2)
    rows8 = lambda a: a.reshape(DEPTH, bs, N_MEM, CA_HEADS, CA_DIM_TILES, LANES).transpose(0, 1, 2, 4, 3, 5).reshape(
        DEPTH, bs, N_MEM, CA_ROWS, LANES)
    cache_mem_k8, cache_mem_v8 = rows8(cache_mem_k), rows8(cache_mem_v)

    ssm_re_p, ssm_im_p, ssm_re_s, ssm_im_s = [], [], [], []
    wk_p, wv_p, wk_s, wv_s = [], [], [], []
    for i in range(DEPTH):
        li = i // N_MIXERS
        xp = _half_ffn(xp, gain(i, 0), gain(i, 1), ffn_w_in_b, ffn_w_out_b, i, 0, FFN_TOKENS)
        xs = _half_ffn(xs, gain(i, 0), gain(i, 1), ffn_w_in_b, ffn_w_out_b, i, 0, FFN_TOKENS)
        if i % N_MIXERS == 0:
            tables = _ssm_tables(ssm_a_re[li], ssm_a_im[li], ssm_log_dt[li], ssm_b_re[li], ssm_b_im[li],
                                 ssm_c_re[li], ssm_c_im[li])
            d_skip = ssm_d[li].astype(F32).reshape(1, D_MODEL)
            b_glu = ssm_b_glu[li].astype(F32).reshape(1, 2 * D_MODEL)
            xp, hr_p, hi_p = _ssm_prompt(xp, gain(i, 2), gain(i, 3), tables, d_skip, ssm_w_glu_b, b_glu, li,
                                         bp, seq)
            xs, hr_s, hi_s = _ssm_sample(xs, state_ssm_re[li], state_ssm_im[li], gain(i, 2), gain(i, 3), tables,
                                         d_skip, ssm_w_glu_b, b_glu, li)
            ssm_re_p.append(hr_p); ssm_im_p.append(hi_p)
            ssm_re_s.append(hr_s); ssm_im_s.append(hi_s)
        else:
            b_qkv = attn_b_qkv[li].astype(F32).reshape(1, QKV_WIDTH)
            sinks = attn_sinks[li].astype(F32)
            xp, bk_p, bv_p = _window_attention_prompt(xp, gain(i, 2), gain(i, 3), attn_w_qkv_b, b_qkv,
                                                      attn_w_o_b, sinks, li, bp, seq)
            xs, bk_s, bv_s = _window_attention_sample(xs, gain(i, 2), gain(i, 3), attn_w_qkv_b, b_qkv, attn_w_o_b,
                                                      sinks, cache_win_kt, cache_win_vt, li)
            wk_p.append(bk_p.reshape(bp, WINDOW, N_KV_HEADS, HEAD_DIM))
            wv_p.append(bv_p.reshape(bp, WINDOW, N_KV_HEADS, HEAD_DIM))
            wk_s.append(bk_s); wv_s.append(bv_s)
        xp = _cross_attention_prompt(xp, gain(i, 4), gain(i, 5), ca_w_q_b, ca_w_o_b, mem_k, mem_v, i, seq)
        xs = _cross_attention_sample(xs, gain(i, 4), gain(i, 5), ca_w_q_b, ca_w_o_b, cache_mem_k8, cache_mem_v8, i)
        xp = _half_ffn(xp, gain(i, 6), gain(i, 7), ffn_w_in_b, ffn_w_out_b, i, 1, FFN_TOKENS)
        xs = _half_ffn(xs, gain(i, 6), gain(i, 7), ffn_w_in_b, ffn_w_out_b, i, 1, FFN_TOKENS)

    mem_shape = (DEPTH, bp, N_MEM, CA_HEADS, CA_HEAD_DIM)
    return (xp.reshape(bp, seq, D_MODEL), xs.reshape(bs, 1, D_MODEL),
            jnp.stack(ssm_re_p), jnp.stack(ssm_im_p), jnp.stack(wk_p), jnp.stack(wv_p),
            mem_k.reshape(mem_shape), mem_v.reshape(mem_shape),
            jnp.stack(ssm_re_s), jnp.stack(ssm_im_s),
            jnp.stack(wk_s).transpose(0, 1, 4, 2, 3), jnp.stack(wv_s).transpose(0, 1, 4, 2, 3))
```

```python
import functools
import math

import jax
import jax.numpy as jnp
from jax import lax
from jax.experimental import pallas as pl
from jax.experimental.pallas import tpu as pltpu

F32 = jnp.float32
BF16 = jnp.bfloat16

D_MODEL = 1024
DEPTH = 4
N_MIXERS = 2
SSM_GROUP = 16
SSM_GROUPS = D_MODEL // SSM_GROUP
SSM_STATE = 64
HEAD_DIM = 64
N_HEADS = D_MODEL // HEAD_DIM
N_KV_HEADS = 4
GQA = N_HEADS // N_KV_HEADS
WINDOW = 128
KV_WIDTH = N_KV_HEADS * HEAD_DIM
QKV_WIDTH = (N_HEADS + 2 * N_KV_HEADS) * HEAD_DIM
N_MEM = 256
CA_HEADS = 4
CA_HEAD_DIM = D_MODEL // CA_HEADS
CA_DIM_TILES = CA_HEAD_DIM // 128
CA_ROWS = CA_HEADS * CA_DIM_TILES
D_FF = ((8 * D_MODEL // 3 + 127) // 128) * 128
FFN_RES = 0.5
EPS = 1e-6
NEG = -1e30

SUBLANES = 8
LANES = 128
VMEM_LIMIT_BYTES = 56 * 1024 * 1024

SSM_LAGS = SUBLANES
SSM_TILE_GROUPS = LANES // SSM_GROUP
SSM_TILES = SSM_GROUPS // SSM_TILE_GROUPS
SSM_TILE_STATE = SSM_TILE_GROUPS * SSM_STATE
SSM_PAIR_GROUPS = 4
SSM_TILE_PAIRS = SSM_TILE_GROUPS // SSM_PAIR_GROUPS
SSM_PAIR_CH = SSM_PAIR_GROUPS * SSM_GROUP
SSM_PAIR_STATE = SSM_PAIR_GROUPS * SSM_STATE
SSM_PAIR_WIDTH = 2 * SSM_PAIR_STATE

FFN_CHUNK = 256
FFN_TOKENS = 1024
CA_TOKENS = 512
SWA_TOKENS = 1024
SSM_TOKENS = 512
SAMPLE_CA_BLOCK = 4
SAMPLE_SWA_BLOCK = 16


def _params(*sem):
    return pltpu.CompilerParams(dimension_semantics=sem, vmem_limit_bytes=VMEM_LIMIT_BYTES)


def _rms(x, g):
    r = lax.rsqrt(jnp.mean(x * x, axis=-1, keepdims=True) + EPS)
    return x * r * g


def _dot(a, b):
    return jnp.dot(a, b, preferred_element_type=F32)


def _dot_nt(a, b):
    return lax.dot_general(a, b, (((1,), (1,)), ((), ())), preferred_element_type=F32)


def _const_spec(shape):
    zeros = (0,) * len(shape)
    return pl.BlockSpec(shape, lambda *_: zeros)


def _ffn_kernel(x_ref, gpre_ref, gpost_ref, wg_ref, wu_ref, wo_ref, o_ref, xn_ref, acc_ref):
    c = pl.program_id(1)

    @pl.when(c == 0)
    def _():
        xn_ref[...] = _rms(x_ref[...], gpre_ref[...]).astype(BF16)
        acc_ref[...] = jnp.zeros_like(acc_ref)

    xn = xn_ref[...]
    gate = _dot(xn, wg_ref[...])
    up = _dot(xn, wu_ref[...])
    h = (gate * jax.nn.sigmoid(gate) * up).astype(BF16)
    acc_ref[...] += _dot(h, wo_ref[...])

    @pl.when(c == pl.num_programs(1) - 1)
    def _():
        o_ref[...] = x_ref[...] + FFN_RES * _rms(acc_ref[...], gpost_ref[...])


def _half_ffn(x, g_pre, g_post, w_in, w_out, layer, half, tokens):
    n = x.shape[0]
    tm = min(tokens, n)
    n_chunks = D_FF // FFN_CHUNK
    return pl.pallas_call(
        _ffn_kernel,
        grid=(n // tm, n_chunks),
        in_specs=[
            pl.BlockSpec((tm, D_MODEL), lambda i, c: (i, 0)),
            _const_spec((1, D_MODEL)),
            _const_spec((1, D_MODEL)),
            pl.BlockSpec((None, None, D_MODEL, FFN_CHUNK), lambda i, c: (layer, half, 0, c)),
            pl.BlockSpec((None, None, D_MODEL, FFN_CHUNK), lambda i, c: (layer, half, 0, c + n_chunks)),
            pl.BlockSpec((None, None, FFN_CHUNK, D_MODEL), lambda i, c: (layer, half, c, 0)),
        ],
        out_specs=pl.BlockSpec((tm, D_MODEL), lambda i, c: (i, 0)),
        out_shape=jax.ShapeDtypeStruct((n, D_MODEL), F32),
        scratch_shapes=[pltpu.VMEM((tm, D_MODEL), BF16), pltpu.VMEM((tm, D_MODEL), F32)],
        compiler_params=_params("parallel", "arbitrary"),
        name="half_ffn",
    )(x, g_pre, g_post, w_in, w_in, w_out)


def _mem_kv_kernel(mem_ref, g_ref, w_ref, k_ref, v_ref):
    mn = _rms(mem_ref[...], g_ref[...]).astype(BF16)
    kv = _dot(mn, w_ref[...])
    k_ref[...] = kv[:, :D_MODEL]
    v_ref[...] = kv[:, D_MODEL:]


def _mem_kv(mem, g_mem, w_kv):
    bsz = mem.shape[0]
    out = jax.ShapeDtypeStruct((DEPTH, bsz, N_MEM, D_MODEL), F32)
    out_spec = pl.BlockSpec((None, None, N_MEM, D_MODEL), lambda l, b: (l, b, 0, 0))
    return pl.pallas_call(
        _mem_kv_kernel,
        grid=(DEPTH, bsz),
        in_specs=[
            pl.BlockSpec((None, N_MEM, D_MODEL), lambda l, b: (b, 0, 0)),
            pl.BlockSpec((None, 1, D_MODEL), lambda l, b: (l, 0, 0)),
            pl.BlockSpec((None, D_MODEL, 2 * D_MODEL), lambda l, b: (l, 0, 0)),
        ],
        out_specs=[out_spec, out_spec],
        out_shape=[out, out],
        compiler_params=_params("parallel", "parallel"),
        name="mem_kv",
    )(mem, g_mem.reshape(DEPTH, 1, D_MODEL), w_kv)


def _ca_kernel(x_ref, gpre_ref, gpost_ref, wq_ref, wo_ref, mk_ref, mv_ref, o_ref):
    x = x_ref[...]
    xn = _rms(x, gpre_ref[...]).astype(BF16)
    q = _dot(xn, wq_ref[...])
    heads = []
    for h in range(CA_HEADS):
        cols = slice(h * CA_HEAD_DIM, (h + 1) * CA_HEAD_DIM)
        s = _dot_nt(q[:, cols].astype(BF16), mk_ref[:, cols].astype(BF16)) * (CA_HEAD_DIM ** -0.5)
        p = jnp.exp(s - jnp.max(s, axis=-1, keepdims=True))
        den = jnp.sum(p, axis=-1, keepdims=True)
        heads.append(_dot(p.astype(BF16), mv_ref[:, cols].astype(BF16)) / den)
    o = jnp.concatenate(heads, axis=-1).astype(BF16)
    o_ref[...] = x + _rms(_dot(o, wo_ref[...]), gpost_ref[...])


def _cross_attention_prompt(x, g_pre, g_post, w_q, w_o, mk, mv, layer, seq):
    n = x.shape[0]
    tm = CA_TOKENS
    per_seq = seq // tm
    mem_spec = pl.BlockSpec((None, None, N_MEM, D_MODEL), lambda i: (layer, i // per_seq, 0, 0))
    w_spec = pl.BlockSpec((None, D_MODEL, D_MODEL), lambda i: (layer, 0, 0))
    return pl.pallas_call(
        _ca_kernel,
        grid=(n // tm,),
        in_specs=[
            pl.BlockSpec((tm, D_MODEL), lambda i: (i, 0)),
            _const_spec((1, D_MODEL)),
            _const_spec((1, D_MODEL)),
            w_spec,
            w_spec,
            mem_spec,
            mem_spec,
        ],
        out_specs=pl.BlockSpec((tm, D_MODEL), lambda i: (i, 0)),
        out_shape=jax.ShapeDtypeStruct((n, D_MODEL), F32),
        compiler_params=_params("parallel"),
        name="cross_attn_prompt",
    )(x, g_pre, g_post, w_q, w_o, mk, mv)


def _stack_rows(pieces):
    n, w = len(pieces), pieces[0].shape[1]
    sub = lax.broadcasted_iota(jnp.int32, (n, w), 0)
    out = jnp.broadcast_to(pieces[0], (n, w))
    for r in range(1, n):
        out = jnp.where(sub == r, jnp.broadcast_to(pieces[r], (n, w)), out)
    return out


def _ca_sample_kernel(x_ref, gpre_ref, gpost_ref, wq_ref, wo_ref, mk_ref, mv_ref, o_ref, q_ref, att_ref):
    i = pl.program_id(0)

    @pl.when(i == 0)
    def _():
        xn = _rms(x_ref[...], gpre_ref[...]).astype(BF16)
        q_ref[...] = _dot(xn, wq_ref[...]) * (CA_HEAD_DIM ** -0.5)

    for b in range(SAMPLE_CA_BLOCK):
        row = i * SAMPLE_CA_BLOCK + b
        q = q_ref[pl.ds(row, 1), :]
        piece = lambda a, r: a[:, (r % CA_HEADS) * CA_HEAD_DIM + (r // CA_HEADS) * LANES:][:, :LANES]
        q8 = _stack_rows([piece(q, r) for r in range(CA_ROWS)])
        part = jnp.sum(mk_ref[b] * q8[None], axis=-1, keepdims=True)
        s = part
        for j in range(1, CA_DIM_TILES):
            s = s + pltpu.roll(part, j * CA_HEADS, axis=1)
        p = jnp.exp(s - jnp.max(s, axis=0, keepdims=True))
        den = jnp.sum(p, axis=0)
        o8 = jnp.sum(p * mv_ref[b], axis=0) / den
        att_ref[pl.ds(row, 1), :] = jnp.concatenate(
            [o8[j * CA_HEADS + h:j * CA_HEADS + h + 1, :] for h in range(CA_HEADS) for j in range(CA_DIM_TILES)],
            axis=-1)

    @pl.when(i == pl.num_programs(0) - 1)
    def _():
        y = _dot(att_ref[...].astype(BF16), wo_ref[...])
        o_ref[...] = x_ref[...] + _rms(y, gpost_ref[...])


def _cross_attention_sample(x, g_pre, g_post, w_q, w_o, mk, mv, layer):
    n = x.shape[0]
    mem_spec = pl.BlockSpec((None, SAMPLE_CA_BLOCK, N_MEM, CA_ROWS, LANES), lambda i: (layer, i, 0, 0, 0))
    w_spec = pl.BlockSpec((None, D_MODEL, D_MODEL), lambda i: (layer, 0, 0))
    return pl.pallas_call(
        _ca_sample_kernel,
        grid=(n // SAMPLE_CA_BLOCK,),
        in_specs=[
            _const_spec((n, D_MODEL)),
            _const_spec((1, D_MODEL)),
            _const_spec((1, D_MODEL)),
            w_spec,
            w_spec,
            mem_spec,
            mem_spec,
        ],
        out_specs=_const_spec((n, D_MODEL)),
        out_shape=jax.ShapeDtypeStruct((n, D_MODEL), F32),
        scratch_shapes=[pltpu.VMEM((n, D_MODEL), F32), pltpu.VMEM((n, D_MODEL), F32)],
        compiler_params=_params("arbitrary"),
        name="cross_attn_sample",
    )(x, g_pre, g_post, w_q, w_o, mk, mv)


def _swa_kernel(sink_ref, x_ref, gpre_ref, gpost_ref, wqkv_ref, bqkv_ref, wo_ref,
                o_ref, kout_ref, vout_ref, kprev_ref, vprev_ref, q_ref):
    t = pl.program_id(1)
    tm = x_ref.shape[0]
    x = x_ref[...]
    xn = _rms(x, gpre_ref[...]).astype(BF16)
    qkv = _dot(xn, wqkv_ref[...]) + bqkv_ref[...]
    q_ref[...] = (qkv[:, :N_HEADS * HEAD_DIM] * (HEAD_DIM ** -0.5)).astype(BF16)
    k = qkv[:, N_HEADS * HEAD_DIM:N_HEADS * HEAD_DIM + KV_WIDTH]
    v = qkv[:, N_HEADS * HEAD_DIM + KV_WIDTH:]

    cur, nxt = t % 2, (t + 1) % 2

    @pl.when(t == 0)
    def _():
        kprev_ref[0] = jnp.zeros((WINDOW, KV_WIDTH), BF16)
        vprev_ref[0] = jnp.zeros((WINDOW, KV_WIDTH), BF16)

    kall = jnp.concatenate([kprev_ref[cur], k.astype(BF16)], axis=0)
    vall = jnp.concatenate([vprev_ref[cur], v.astype(BF16)], axis=0)

    row = lax.broadcasted_iota(jnp.int32, (GQA * WINDOW, 2 * WINDOW), 0)
    qi = row & (WINDOW - 1)
    kj = lax.broadcasted_iota(jnp.int32, (GQA * WINDOW, 2 * WINDOW), 1)
    band = (kj >= qi) & (kj <= qi + WINDOW)
    row_g = lax.broadcasted_iota(jnp.int32, (GQA * WINDOW, 1), 0) // WINDOW

    for n in range(tm // WINDOW):
        rows = slice(n * WINDOW, (n + 1) * WINDOW)
        kk = kall[n * WINDOW:(n + 2) * WINDOW, :]
        vv = vall[n * WINDOW:(n + 2) * WINDOW, :]
        valid = band & (kj >= WINDOW - (t * tm + n * WINDOW)) if n == 0 else band
        kv_cols = [slice(kh * HEAD_DIM, (kh + 1) * HEAD_DIM) for kh in range(N_KV_HEADS)]
        scores, sinks = [], []
        for kh in range(N_KV_HEADS):
            q4 = jnp.concatenate(
                [q_ref[rows, (kh * GQA + g) * HEAD_DIM:(kh * GQA + g + 1) * HEAD_DIM] for g in range(GQA)],
                axis=0)
            sink = jnp.full((GQA * WINDOW, 1), sink_ref[kh * GQA], F32)
            for g in range(1, GQA):
                sink = jnp.where(row_g == g, sink_ref[kh * GQA + g], sink)
            sinks.append(sink)
            scores.append(jnp.where(valid, _dot_nt(q4, kk[:, kv_cols[kh]]), NEG))
        maxes = [jnp.maximum(jnp.max(s, axis=-1, keepdims=True), sink) for s, sink in zip(scores, sinks)]
        probs = [jnp.exp(s - m).astype(BF16) for s, m in zip(scores, maxes)]
        ones = jnp.ones((2 * WINDOW, HEAD_DIM), BF16)
        outs = [_dot(p, jnp.concatenate([vv[:, kv_cols[kh]], ones], axis=1)) for kh, p in enumerate(probs)]
        heads = []
        for kh in range(N_KV_HEADS):
            den = outs[kh][:, HEAD_DIM:HEAD_DIM + 1] + jnp.exp(sinks[kh] - maxes[kh])
            o4 = outs[kh][:, :HEAD_DIM] / den
            heads += [o4[g * WINDOW:(g + 1) * WINDOW] for g in range(GQA)]
        att = jnp.concatenate(heads, axis=1).astype(BF16)
        o_ref[rows, :] = x_ref[rows, :] + _rms(_dot(att, wo_ref[...]), gpost_ref[...])

    kprev_ref[nxt] = k[tm - WINDOW:, :].astype(BF16)
    vprev_ref[nxt] = v[tm - WINDOW:, :].astype(BF16)

    kout_ref[...] = k[tm - WINDOW:, :]
    vout_ref[...] = v[tm - WINDOW:, :]


def _window_attention_prompt(x, g_pre, g_post, w_qkv, b_qkv, w_o, sinks, layer, bsz, seq):
    tm = SWA_TOKENS
    per_seq = seq // tm
    win_spec = pl.BlockSpec((None, WINDOW, KV_WIDTH), lambda b, t, *_: (b, 0, 0))
    win_shape = jax.ShapeDtypeStruct((bsz, WINDOW, KV_WIDTH), F32)
    grid_spec = pltpu.PrefetchScalarGridSpec(
        num_scalar_prefetch=1,
        grid=(bsz, per_seq),
        in_specs=[
            pl.BlockSpec((tm, D_MODEL), lambda b, t, *_: (b * per_seq + t, 0)),
            _const_spec((1, D_MODEL)),
            _const_spec((1, D_MODEL)),
            pl.BlockSpec((None, D_MODEL, QKV_WIDTH), lambda b, t, *_: (layer, 0, 0)),
            _const_spec((1, QKV_WIDTH)),
            pl.BlockSpec((None, N_HEADS * HEAD_DIM, D_MODEL), lambda b, t, *_: (layer, 0, 0)),
        ],
        out_specs=[pl.BlockSpec((tm, D_MODEL), lambda b, t, *_: (b * per_seq + t, 0)), win_spec, win_spec],
        scratch_shapes=[
            pltpu.VMEM((2, WINDOW, KV_WIDTH), BF16),
            pltpu.VMEM((2, WINDOW, KV_WIDTH), BF16),
            pltpu.VMEM((tm, N_HEADS * HEAD_DIM), BF16),
        ],
    )
    return pl.pallas_call(
        _swa_kernel,
        grid_spec=grid_spec,
        out_shape=[jax.ShapeDtypeStruct(x.shape, F32), win_shape, win_shape],
        compiler_params=_params("arbitrary", "arbitrary"),
        name="window_attn_prompt",
    )(sinks, x, g_pre, g_post, w_qkv, b_qkv, w_o)


def _swa_sample_kernel(sink_ref, x_ref, gpre_ref, gpost_ref, wqkv_ref, bqkv_ref, wo_ref, ck_ref, cv_ref,
                       o_ref, nk_ref, nv_ref, qkvt_ref, attt_ref, blkt_ref):
    i = pl.program_id(0)
    n = x_ref.shape[0]
    k0 = N_HEADS * HEAD_DIM
    v0 = k0 + KV_WIDTH

    @pl.when(i == 0)
    def _():
        xn = _rms(x_ref[...], gpre_ref[...]).astype(BF16)
        qkvt_ref[...] = (_dot(xn, wqkv_ref[...]) + bqkv_ref[...]).T
        attt_ref[...] = jnp.zeros_like(attt_ref)
        blkt_ref[...] = jnp.zeros_like(blkt_ref)

    base = i * SAMPLE_SWA_BLOCK
    qkvt = pltpu.roll(qkvt_ref[...], (n - base) % n, axis=1)
    newest = lax.broadcasted_iota(jnp.int32, (KV_WIDTH, WINDOW), 1) == WINDOW - 1
    heads3 = lambda a: a.reshape(N_HEADS, HEAD_DIM, a.shape[-1])
    per_q_head = lambda a: jnp.concatenate(
        [a[(h // GQA) * HEAD_DIM:(h // GQA + 1) * HEAD_DIM] for h in range(N_HEADS)], axis=0)
    sink = sink_ref[...]
    for b in range(SAMPLE_SWA_BLOCK):
        col = qkvt[:, b:b + 1]
        q = col[0:k0] * (HEAD_DIM ** -0.5)
        k_new, v_new = col[k0:v0], col[v0:]
        kt = ck_ref[b].reshape(KV_WIDTH, WINDOW)
        vt = cv_ref[b].reshape(KV_WIDTH, WINDOW)
        s = jnp.sum(heads3(per_q_head(kt) * q), axis=1, keepdims=True)
        s_new = jnp.sum(heads3(per_q_head(k_new) * q), axis=1, keepdims=True)
        m = jnp.maximum(jnp.maximum(jnp.max(s, axis=2, keepdims=True), s_new), sink)
        p = jnp.exp(s - m)
        p_new = jnp.exp(s_new - m)
        den = jnp.sum(p, axis=2, keepdims=True) + p_new + jnp.exp(sink - m)
        o = jnp.sum(heads3(per_q_head(vt)) * p, axis=2, keepdims=True)
        o = (o + p_new * heads3(per_q_head(v_new))) / den
        blkt_ref[:, b:b + 1] = o.reshape(N_HEADS * HEAD_DIM, 1)
        nk_ref[b] = jnp.where(newest, k_new, pltpu.roll(kt, WINDOW - 1, axis=1)).reshape(nk_ref.shape[1:])
        nv_ref[b] = jnp.where(newest, v_new, pltpu.roll(vt, WINDOW - 1, axis=1)).reshape(nv_ref.shape[1:])

    lane = lax.broadcasted_iota(jnp.int32, attt_ref.shape, 1)
    mine = (lane >= base) & (lane < base + SAMPLE_SWA_BLOCK)
    attt_ref[...] = jnp.where(mine, pltpu.roll(blkt_ref[...], base, axis=1), attt_ref[...])

    @pl.when(i == pl.num_programs(0) - 1)
    def _():
        y = _dot(attt_ref[...].T.astype(BF16), wo_ref[...])
        o_ref[...] = x_ref[...] + _rms(y, gpost_ref[...])


def _window_attention_sample(x, g_pre, g_post, w_qkv, b_qkv, w_o, sinks, cache_k, cache_v, layer):
    n = x.shape[0]
    assert n == LANES, "the sample kernel keeps one sample per lane"
    blk = SAMPLE_SWA_BLOCK
    cache_spec = pl.BlockSpec((None, blk, N_KV_HEADS, HEAD_DIM, WINDOW), lambda i, *_: (layer, i, 0, 0, 0))
    win_spec = pl.BlockSpec((blk, N_KV_HEADS, HEAD_DIM, WINDOW), lambda i, *_: (i, 0, 0, 0))
    win_shape = jax.ShapeDtypeStruct((n, N_KV_HEADS, HEAD_DIM, WINDOW), F32)
    return pl.pallas_call(
        _swa_sample_kernel,
        grid=(n // blk,),
        in_specs=[
            _const_spec((N_HEADS, 1, 1)),
            _const_spec((n, D_MODEL)),
            _const_spec((1, D_MODEL)),
            _const_spec((1, D_MODEL)),
            pl.BlockSpec((None, D_MODEL, QKV_WIDTH), lambda i, *_: (layer, 0, 0)),
            _const_spec((1, QKV_WIDTH)),
            pl.BlockSpec((None, N_HEADS * HEAD_DIM, D_MODEL), lambda i, *_: (layer, 0, 0)),
            cache_spec,
            cache_spec,
        ],
        out_specs=[_const_spec((n, D_MODEL)), win_spec, win_spec],
        out_shape=[jax.ShapeDtypeStruct(x.shape, F32), win_shape, win_shape],
        scratch_shapes=[pltpu.VMEM((QKV_WIDTH, n), F32), pltpu.VMEM((N_HEADS * HEAD_DIM, n), F32),
                        pltpu.VMEM((N_HEADS * HEAD_DIM, n), F32)],
        compiler_params=_params("arbitrary"),
        name="window_attn_sample",
    )(sinks.reshape(N_HEADS, 1, 1), x, g_pre, g_post, w_qkv, b_qkv, w_o, cache_k, cache_v)


def _ssm_table_kernel(ar_ref, ai_ref, ldt_ref, br_ref, bi_ref, w_ref, lam_ref):
    P, PG, GS = SSM_STATE, SSM_PAIR_GROUPS, SSM_GROUP
    ar, ai, dt = ar_ref[...], ai_ref[...], jnp.exp(ldt_ref[...])
    mag = jnp.exp(ar * dt)
    lr, li = mag * jnp.cos(ai * dt), mag * jnp.sin(ai * dt)
    den = ar * ar + ai * ai
    nr, ni = lr - 1.0, li
    zr = (nr * ar + ni * ai) / den
    zi = (ni * ar - nr * ai) / den
    br, bi = br_ref[...], bi_ref[...]
    wr = zr * br - zi * bi
    wi = zr * bi + zi * br
    pr, pi = lr, li
    rows = PG * GS
    own = (lax.broadcasted_iota(jnp.int32, (rows, PG * P), 0) // GS
           == lax.broadcasted_iota(jnp.int32, (rows, PG * P), 1) // P)
    for k in range(SSM_LAGS):
        parts = [jnp.where(own, jnp.concatenate([w] * PG, axis=1), 0.0) for w in (wr, wi)]
        w_ref[k * rows:(k + 1) * rows, :] = jnp.concatenate(parts, axis=1).astype(BF16)
        wr, wi = lr * wr - li * wi, lr * wi + li * wr
        if k > 0:
            pr, pi = lr * pr - li * pi, lr * pi + li * pr
    lam_ref[0] = lr
    lam_ref[1] = li
    lam_ref[2] = pr
    lam_ref[3] = pi


def _ssm_tables(a_re, a_im, log_dt, b_re, b_im, c_re, c_im):
    G, P, GS, R, NT = SSM_GROUPS, SSM_STATE, SSM_GROUP, SSM_LAGS, SSM_TILES
    NQ, PG = SSM_TILE_PAIRS, SSM_PAIR_GROUPS
    iota = lambda n: jnp.arange(n, dtype=jnp.int32)
    clusters = NT * NQ
    per_chan = lambda a: jnp.broadcast_to(a.astype(F32).reshape(clusters, PG, 1, P), (clusters, PG, GS, P)).reshape(
        clusters, SSM_PAIR_CH, P)
    b_t = lambda b: b.astype(F32).transpose(0, 2, 1).reshape(clusters, SSM_PAIR_CH, P)
    in_spec = pl.BlockSpec((None, SSM_PAIR_CH, P), lambda i: (i, 0, 0))
    w_pair, lam = pl.pallas_call(
        _ssm_table_kernel,
        grid=(clusters,),
        in_specs=[in_spec] * 5,
        out_specs=[pl.BlockSpec((None, R * SSM_PAIR_CH, SSM_PAIR_WIDTH), lambda i: (i, 0, 0)),
                   pl.BlockSpec((None, 4, SSM_PAIR_CH, P), lambda i: (i, 0, 0, 0))],
        out_shape=[jax.ShapeDtypeStruct((clusters, R * SSM_PAIR_CH, SSM_PAIR_WIDTH), BF16),
                   jax.ShapeDtypeStruct((clusters, 4, SSM_PAIR_CH, P), F32)],
        compiler_params=_params("parallel"),
        name="ssm_tables",
    )(per_chan(a_re), per_chan(a_im), per_chan(jnp.broadcast_to(log_dt[:, None], (G, P))), b_t(b_re), b_t(b_im))
    w_pair = w_pair.reshape(NT, NQ, R * SSM_PAIR_CH, SSM_PAIR_WIDTH)
    lam = lam[:, :, ::GS, :].transpose(1, 0, 2, 3).reshape(4, G * P)
    w0 = jnp.tile(w_pair[:, :, :SSM_PAIR_CH, None, :], (1, 1, 1, NQ, 1))
    w0 = jnp.where((iota(NQ)[:, None, None, None] == iota(NQ)[None, None, :, None])[None], w0, 0).reshape(
        NT, LANES, NQ * SSM_PAIR_WIDTH)
    ct = jnp.stack([c_re, c_im]).astype(BF16).reshape(2, NT, NQ, PG, GS, P).transpose(1, 2, 0, 3, 5, 4)
    ct = jnp.tile(ct.reshape(NT, NQ, SSM_PAIR_WIDTH, GS), (1, 1, 1, LANES // GS))
    want_slot = iota(NQ)[:, None] * PG + ((iota(SSM_PAIR_WIDTH) // P) % PG)[None, :]
    c_pair = jnp.where(want_slot[:, :, None] == (iota(LANES) // GS)[None, None, :], ct, 0)

    def lam_rows(v):
        v = v.reshape(NT, NQ, SSM_PAIR_STATE)
        return jnp.concatenate([v, v], axis=-1).reshape(NT, 1, NQ * SSM_PAIR_WIDTH)

    return (w_pair, w0, c_pair, lam_rows(lam[0]), lam_rows(lam[1]), lam_rows(lam[2]), lam_rows(lam[3]))


def _glu_tail(x, u, y, d_ref, wglu_ref, bglu_ref, gpost_ref):
    y = y + d_ref[...] * u
    y = 0.5 * y * (1.0 + lax.erf(y * (2.0 ** -0.5)))
    z = _dot(y.astype(BF16), wglu_ref[...]) + bglu_ref[...]
    out = z[:, :D_MODEL] * jax.nn.sigmoid(z[:, D_MODEL:])
    return x + _rms(out, gpost_ref[...])


def _swap_re_im(a):
    tiles = a.shape[-1] // SSM_PAIR_STATE
    return jnp.concatenate(
        [a[:, (j ^ 1) * SSM_PAIR_STATE:((j ^ 1) + 1) * SSM_PAIR_STATE] for j in range(tiles)], axis=1)


def _ssm_kernel(x_ref, gpre_ref, gpost_ref, wp_ref, cp_ref, lamr_ref, lami_ref, d_ref, wglu_ref, bglu_ref,
                o_ref, hout_ref, ubuf_ref, uprev_ref, h2_ref, y_ref, carry_ref):
    t = pl.program_id(1)
    tm = x_ref.shape[0]
    NQ, PS, PW = SSM_TILE_PAIRS, SSM_PAIR_STATE, SSM_PAIR_WIDTH
    S2 = NQ * PW
    cur, nxt = t % 2, (t + 1) % 2

    @pl.when(t == 0)
    def _():
        uprev_ref[0] = jnp.zeros((SSM_LAGS, D_MODEL), F32)
        carry_ref[...] = jnp.zeros_like(carry_ref)

    x = x_ref[...]
    u = _rms(x, gpre_ref[...])
    ubuf_ref[0:SSM_LAGS, :] = uprev_ref[cur]
    ubuf_ref[SSM_LAGS:, :] = u

    slot = lax.broadcasted_iota(jnp.int32, (tm, LANES), 1) // SSM_PAIR_CH
    im_lane = (lax.broadcasted_iota(jnp.int32, (1, S2), 1) // SSM_PAIR_STATE) % 2 == 1

    for c in range(SSM_TILES):
        cols = slice(c * LANES, (c + 1) * LANES)
        h_ref = h2_ref.at[c % 2]
        lagged = [ubuf_ref[SSM_LAGS - k:SSM_LAGS - k + tm, cols] for k in range(SSM_LAGS)]
        for q in range(NQ):
            halves = []
            for half in range(SSM_LAGS // NQ):
                acc = None
                for m in range(NQ):
                    piece = lagged[half * NQ + m]
                    if m != q:
                        piece = pltpu.roll(piece, (SSM_PAIR_CH * (m - q)) % LANES, axis=1)
                    acc = piece if acc is None else jnp.where(slot == m, piece, acc)
                halves.append(acc)
            lhs = jnp.concatenate(halves, axis=1).astype(BF16)
            h_ref[:, q * PW:(q + 1) * PW] = _dot(lhs, wp_ref[c, q])
        lr = jnp.broadcast_to(lamr_ref[c], (SUBLANES, S2))
        li = jnp.broadcast_to(jnp.where(im_lane, lami_ref[c], -lami_ref[c]), (SUBLANES, S2))

        def slab(m, carry):
            r0 = pl.multiple_of(m * SUBLANES, SUBLANES)
            new = h_ref[pl.ds(r0, SUBLANES), :] + (lr * carry + li * _swap_re_im(carry))
            h_ref[pl.ds(r0, SUBLANES), :] = new
            return new

        carry_ref[c] = lax.fori_loop(0, tm // SUBLANES, slab, carry_ref[c], unroll=True)
        y = None
        for q in range(NQ):
            h_re = h_ref[:, q * PW:q * PW + PS].astype(BF16)
            h_im = h_ref[:, q * PW + PS:(q + 1) * PW].astype(BF16)
            yq = _dot(h_re, cp_ref[c, q, 0:PS, :]) - _dot(h_im, cp_ref[c, q, PS:PW, :])
            y = yq if y is None else y + yq
        y_ref[:, cols] = y

    o_ref[...] = _glu_tail(x, u, y_ref[...], d_ref, wglu_ref, bglu_ref, gpost_ref)
    uprev_ref[nxt] = u[tm - SSM_LAGS:, :]

    @pl.when(t == pl.num_programs(1) - 1)
    def _():
        hout_ref[...] = carry_ref[...]


def _ssm_prompt(x, g_pre, g_post, tables, d_skip, w_glu, b_glu, layer, bsz, seq):
    w_pair, _, c_pair, _, _, lamk_re, lamk_im = tables
    tm = SSM_TOKENS
    per_seq = seq // tm
    S2 = 2 * SSM_TILE_STATE
    once = pl.Buffered(1)
    out, h_last = pl.pallas_call(
        _ssm_kernel,
        grid=(bsz, per_seq),
        in_specs=[
            pl.BlockSpec((tm, D_MODEL), lambda b, t: (b * per_seq + t, 0)),
            _const_spec((1, D_MODEL)),
            _const_spec((1, D_MODEL)),
            pl.BlockSpec(w_pair.shape, lambda b, t: (0, 0, 0, 0), pipeline_mode=once),
            pl.BlockSpec(c_pair.shape, lambda b, t: (0, 0, 0, 0), pipeline_mode=once),
            _const_spec(lamk_re.shape),
            _const_spec(lamk_im.shape),
            _const_spec((1, D_MODEL)),
            pl.BlockSpec((None, D_MODEL, 2 * D_MODEL), lambda b, t: (layer, 0, 0), pipeline_mode=once),
            _const_spec((1, 2 * D_MODEL)),
        ],
        out_specs=[
            pl.BlockSpec((tm, D_MODEL), lambda b, t: (b * per_seq + t, 0)),
            pl.BlockSpec((None, SSM_TILES, SUBLANES, S2), lambda b, t: (b, 0, 0, 0)),
        ],
        out_shape=[jax.ShapeDtypeStruct(x.shape, F32),
                   jax.ShapeDtypeStruct((bsz, SSM_TILES, SUBLANES, S2), F32)],
        scratch_shapes=[
            pltpu.VMEM((tm + SSM_LAGS, D_MODEL), F32),
            pltpu.VMEM((2, SSM_LAGS, D_MODEL), F32),
            pltpu.VMEM((2, tm, S2), F32),
            pltpu.VMEM((tm, D_MODEL), F32),
            pltpu.VMEM((SSM_TILES, SUBLANES, S2), F32),
        ],
        compiler_params=_params("arbitrary", "arbitrary"),
        name="ssm_prompt",
    )(x, g_pre, g_post, w_pair, c_pair, lamk_re, lamk_im, d_skip, w_glu, b_glu)
    h_last = h_last[:, :, SUBLANES - 1, :].reshape(bsz, SSM_TILES, SSM_TILE_PAIRS, 2, SSM_PAIR_STATE)
    shape = (bsz, SSM_GROUPS, SSM_STATE)
    return out, h_last[:, :, :, 0, :].reshape(shape), h_last[:, :, :, 1, :].reshape(shape)


def _ssm_sample_kernel(x_ref, sre_ref, sim_ref, gpre_ref, gpost_ref, w0_ref, cp_ref, lamr_ref, lami_ref,
                       d_ref, wglu_ref, bglu_ref, o_ref, nre_ref, nim_ref, y_ref):
    PS, PW = SSM_PAIR_STATE, SSM_PAIR_WIDTH
    x = x_ref[...]
    u = _rms(x, gpre_ref[...])
    for c in range(SSM_TILES):
        cols = slice(c * LANES, (c + 1) * LANES)
        bu = _dot(u[:, cols].astype(BF16), w0_ref[c])
        y = None
        for q in range(SSM_TILE_PAIRS):
            st = slice(c * SSM_TILE_STATE + q * PS, c * SSM_TILE_STATE + (q + 1) * PS)
            lr = lamr_ref[c][:, q * PW:q * PW + PS]
            li = lami_ref[c][:, q * PW:q * PW + PS]
            h0r, h0i = sre_ref[:, st], sim_ref[:, st]
            hr = bu[:, q * PW:q * PW + PS] + (lr * h0r - li * h0i)
            hi = bu[:, q * PW + PS:(q + 1) * PW] + (lr * h0i + li * h0r)
            nre_ref[:, st] = hr
            nim_ref[:, st] = hi
            yq = _dot(jnp.concatenate([hr, -hi], axis=1).astype(BF16), cp_ref[c, q])
            y = yq if y is None else y + yq
        y_ref[:, cols] = y
    o_ref[...] = _glu_tail(x, u, y_ref[...], d_ref, wglu_ref, bglu_ref, gpost_ref)


def _ssm_sample(x, state_re, state_im, g_pre, g_post, tables, d_skip, w_glu, b_glu, layer):
    _, w0, c_pair, lam1_re, lam1_im, _, _ = tables
    n = x.shape[0]
    flat = (n, SSM_GROUPS * SSM_STATE)
    st = jax.ShapeDtypeStruct(flat, F32)
    S2 = 2 * SSM_TILE_STATE
    out, nre, nim = pl.pallas_call(
        _ssm_sample_kernel,
        grid=(1,),
        in_specs=[
            _const_spec((n, D_MODEL)),
            _const_spec(flat),
            _const_spec(flat),
            _const_spec((1, D_MODEL)),
            _const_spec((1, D_MODEL)),
            _const_spec(w0.shape),
            _const_spec(c_pair.shape),
            _const_spec(lam1_re.shape),
            _const_spec(lam1_im.shape),
            _const_spec((1, D_MODEL)),
            pl.BlockSpec((None, D_MODEL, 2 * D_MODEL), lambda i: (layer, 0, 0)),
            _const_spec((1, 2 * D_MODEL)),
        ],
        out_specs=[_const_spec((n, D_MODEL)), _const_spec(flat), _const_spec(flat)],
        out_shape=[jax.ShapeDtypeStruct(x.shape, F32), st, st],
        scratch_shapes=[pltpu.VMEM((n, D_MODEL), F32)],
        compiler_params=_params("arbitrary"),
        name="ssm_sample",
    )(x, state_re.reshape(flat), state_im.reshape(flat), g_pre, g_post, w0, c_pair, lam1_re, lam1_im,
      d_skip, w_glu, b_glu)
    shape = (n, SSM_GROUPS, SSM_STATE)
    return out, nre.reshape(shape), nim.reshape(shape)


def kernel(x_prompt, x_sample, mem_prompt, state_ssm_re, state_ssm_im, cache_win_k, cache_win_v, cache_mem_k, cache_mem_v, norm_g, mem_norm_g, ffn_w_in, ffn_w_out, ssm_a_re, ssm_a_im, ssm_log_dt, ssm_b_re, ssm_b_im, ssm_c_re, ssm_c_im, ssm_d, ssm_w_glu, ssm_b_glu, attn_w_qkv, attn_b_qkv, attn_w_o, attn_sinks, ca_w_q, ca_w_kv, ca_w_o):
    bp, seq, _ = x_prompt.shape
    bs = x_sample.shape[0]
    xp = x_prompt.reshape(bp * seq, D_MODEL)
    xs = x_sample.reshape(bs, D_MODEL)

    gain = lambda i, r: norm_g[i, r].astype(F32).reshape(1, D_MODEL)
    ffn_w_in_b, ffn_w_out_b = ffn_w_in.astype(BF16), ffn_w_out.astype(BF16)
    ssm_w_glu_b = ssm_w_glu.astype(BF16)
    attn_w_qkv_b, attn_w_o_b = attn_w_qkv.astype(BF16), attn_w_o.astype(BF16)
    ca_w_q_b, ca_w_o_b = ca_w_q.astype(BF16), ca_w_o.astype(BF16)

    mem_k, mem_v = _mem_kv(mem_prompt, mem_norm_g.astype(F32), ca_w_kv.astype(BF16))
    cache_win_kt = cache_win_k.transpose(0, 1, 3, 4, 2)
    cache_win_vt = cache_win_v.transpose(0, 1, 3, 4, 2)
    rows8 = lambda a: a.reshape(DEPTH, bs, N_MEM, CA_HEADS, CA_DIM_TILES, LANES).transpose(0, 1, 2, 4, 3, 5).reshape(
        DEPTH, bs, N_MEM, CA_ROWS, LANES)
    cache_mem_k8, cache_mem_v8 = rows8(cache_mem_k), rows8(cache_mem_v)

    ssm_re_p, ssm_im_p, ssm_re_s, ssm_im_s = [], [], [], []
    wk_p, wv_p, wk_s, wv_s = [], [], [], []
    for i in range(DEPTH):
        li = i // N_MIXERS
        xp = _half_ffn(xp, gain(i, 0), gain(i, 1), ffn_w_in_b, ffn_w_out_b, i, 0, FFN_TOKENS)
        xs = _half_ffn(xs, gain(i, 0), gain(i, 1), ffn_w_in_b, ffn_w_out_b, i, 0, FFN_TOKENS)
        if i % N_MIXERS == 0:
            tables = _ssm_tables(ssm_a_re[li], ssm_a_im[li], ssm_log_dt[li], ssm_b_re[li], ssm_b_im[li],
                                 ssm_c_re[li], ssm_c_im[li])
            d_skip = ssm_d[li].astype(F32).reshape(1, D_MODEL)
            b_glu = ssm_b_glu[li].astype(F32).reshape(1, 2 * D_MODEL)
            xp, hr_p, hi_p = _ssm_prompt(xp, gain(i, 2), gain(i, 3), tables, d_skip, ssm_w_glu_b, b_glu, li,
                                         bp, seq)
            xs, hr_s, hi_s = _ssm_sample(xs, state_ssm_re[li], state_ssm_im[li], gain(i, 2), gain(i, 3), tables,
                                         d_skip, ssm_w_glu_b, b_glu, li)
            ssm_re_p.append(hr_p); ssm_im_p.append(hi_p)
            ssm_re_s.append(hr_s); ssm_im_s.append(hi_s)
        else:
            b_qkv = attn_b_qkv[li].astype(F32).reshape(1, QKV_WIDTH)
            sinks = attn_sinks[li].astype(F32)
            xp, bk_p, bv_p = _window_attention_prompt(xp, gain(i, 2), gain(i, 3), attn_w_qkv_b, b_qkv,
                                                      attn_w_o_b, sinks, li, bp, seq)
            xs, bk_s, bv_s = _window_attention_sample(xs, gain(i, 2), gain(i, 3), attn_w_qkv_b, b_qkv, attn_w_o_b,
                                                      sinks, cache_win_kt, cache_win_vt, li)
            wk_p.append(bk_p.reshape(bp, WINDOW, N_KV_HEADS, HEAD_DIM))
            wv_p.append(bv_p.reshape(bp, WINDOW, N_KV_HEADS, HEAD_DIM))
            wk_s.append(bk_s); wv_s.append(bv_s)
        xp = _cross_attention_prompt(xp, gain(i, 4), gain(i, 5), ca_w_q_b, ca_w_o_b, mem_k, mem_v, i, seq)
        xs = _cross_attention_sample(xs, gain(i, 4), gain(i, 5), ca_w_q_b, ca_w_o_b, cache_mem_k8, cache_mem_v8, i)
        xp = _half_ffn(xp, gain(i, 6), gain(i, 7), ffn_w_in_b, ffn_w_out_b, i, 1, FFN_TOKENS)
        xs = _half_ffn(xs, gain(i, 6), gain(i, 7), ffn_w_in_b, ffn_w_out_b, i, 1, FFN_TOKENS)

    mem_shape = (DEPTH, bp, N_MEM, CA_HEADS, CA_HEAD_DIM)
    return (xp.reshape(bp, seq, D_MODEL), xs.reshape(bs, 1, D_MODEL),
            jnp.stack(ssm_re_p), jnp.stack(ssm_im_p), jnp.stack(wk_p), jnp.stack(wv_p),
            mem_k.reshape(mem_shape), mem_v.reshape(mem_shape),
            jnp.stack(ssm_re_s), jnp.stack(ssm_im_s),
            jnp.stack(wk_s).transpose(0, 1, 4, 2, 3), jnp.stack(wv_s).transpose(0, 1, 4, 2, 3))
```

```python
import functools
import math

import jax
import jax.numpy as jnp
from jax import lax
from jax.experimental import pallas as pl
from jax.experimental.pallas import tpu as pltpu

F32 = jnp.float32
BF16 = jnp.bfloat16

D_MODEL = 1024
DEPTH = 4
N_MIXERS = 2
SSM_GROUP = 16
SSM_GROUPS = D_MODEL // SSM_GROUP
SSM_STATE = 64
HEAD_DIM = 64
N_HEADS = D_MODEL // HEAD_DIM
N_KV_HEADS = 4
GQA = N_HEADS // N_KV_HEADS
WINDOW = 128
KV_WIDTH = N_KV_HEADS * HEAD_DIM
QKV_WIDTH = (N_HEADS + 2 * N_KV_HEADS) * HEAD_DIM
N_MEM = 256
CA_HEADS = 4
CA_HEAD_DIM = D_MODEL // CA_HEADS
CA_DIM_TILES = CA_HEAD_DIM // 128
CA_ROWS = CA_HEADS * CA_DIM_TILES
D_FF = ((8 * D_MODEL // 3 + 127) // 128) * 128
FFN_RES = 0.5
EPS = 1e-6
NEG = -1e30

SUBLANES = 8
LANES = 128
VMEM_LIMIT_BYTES = 56 * 1024 * 1024

SSM_LAGS = SUBLANES
SSM_TILE_GROUPS = LANES // SSM_GROUP
SSM_TILES = SSM_GROUPS // SSM_TILE_GROUPS
SSM_TILE_STATE = SSM_TILE_GROUPS * SSM_STATE
SSM_PAIR_GROUPS = 4
SSM_TILE_PAIRS = SSM_TILE_GROUPS // SSM_PAIR_GROUPS
SSM_PAIR_CH = SSM_PAIR_GROUPS * SSM_GROUP
SSM_PAIR_STATE = SSM_PAIR_GROUPS * SSM_STATE
SSM_PAIR_WIDTH = 2 * SSM_PAIR_STATE

FFN_CHUNK = 256
FFN_TOKENS = 1024
CA_TOKENS = 1024
SWA_TOKENS = 1024
SSM_TOKENS = 512
SAMPLE_CA_BLOCK = 4
SAMPLE_SWA_BLOCK = 16


def _params(*sem):
    return pltpu.CompilerParams(dimension_semantics=sem, vmem_limit_bytes=VMEM_LIMIT_BYTES)


def _rms(x, g):
    r = lax.rsqrt(jnp.mean(x * x, axis=-1, keepdims=True) + EPS)
    return x * r * g


def _dot(a, b):
    return jnp.dot(a, b, preferred_element_type=F32)


def _dot_nt(a, b):
    return lax.dot_general(a, b, (((1,), (1,)), ((), ())), preferred_element_type=F32)


def _const_spec(shape):
    zeros = (0,) * len(shape)
    return pl.BlockSpec(shape, lambda *_: zeros)


def _ffn_kernel(x_ref, gpre_ref, gpost_ref, wg_ref, wu_ref, wo_ref, o_ref, xn_ref, acc_ref):
    c = pl.program_id(1)

    @pl.when(c == 0)
    def _():
        xn_ref[...] = _rms(x_ref[...], gpre_ref[...]).astype(BF16)
        acc_ref[...] = jnp.zeros_like(acc_ref)

    xn = xn_ref[...]
    gate = _dot(xn, wg_ref[...])
    up = _dot(xn, wu_ref[...])
    h = (gate * jax.nn.sigmoid(gate) * up).astype(BF16)
    acc_ref[...] += _dot(h, wo_ref[...])

    @pl.when(c == pl.num_programs(1) - 1)
    def _():
        o_ref[...] = x_ref[...] + FFN_RES * _rms(acc_ref[...], gpost_ref[...])


def _half_ffn(x, g_pre, g_post, w_in, w_out, layer, half, tokens):
    n = x.shape[0]
    tm = min(tokens, n)
    n_chunks = D_FF // FFN_CHUNK
    return pl.pallas_call(
        _ffn_kernel,
        grid=(n // tm, n_chunks),
        in_specs=[
            pl.BlockSpec((tm, D_MODEL), lambda i, c: (i, 0)),
            _const_spec((1, D_MODEL)),
            _const_spec((1, D_MODEL)),
            pl.BlockSpec((None, None, D_MODEL, FFN_CHUNK), lambda i, c: (layer, half, 0, c)),
            pl.BlockSpec((None, None, D_MODEL, FFN_CHUNK), lambda i, c: (layer, half, 0, c + n_chunks)),
            pl.BlockSpec((None, None, FFN_CHUNK, D_MODEL), lambda i, c: (layer, half, c, 0)),
        ],
        out_specs=pl.BlockSpec((tm, D_MODEL), lambda i, c: (i, 0)),
        out_shape=jax.ShapeDtypeStruct((n, D_MODEL), F32),
        scratch_shapes=[pltpu.VMEM((tm, D_MODEL), BF16), pltpu.VMEM((tm, D_MODEL), F32)],
        compiler_params=_params("parallel", "arbitrary"),
        name="half_ffn",
    )(x, g_pre, g_post, w_in, w_in, w_out)


def _mem_kv_kernel(mem_ref, g_ref, w_ref, k_ref, v_ref):
    mn = _rms(mem_ref[...], g_ref[...]).astype(BF16)
    kv = _dot(mn, w_ref[...])
    k_ref[...] = kv[:, :D_MODEL]
    v_ref[...] = kv[:, D_MODEL:]


def _mem_kv(mem, g_mem, w_kv):
    bsz = mem.shape[0]
    out = jax.ShapeDtypeStruct((DEPTH, bsz, N_MEM, D_MODEL), F32)
    out_spec = pl.BlockSpec((None, None, N_MEM, D_MODEL), lambda l, b: (l, b, 0, 0))
    return pl.pallas_call(
        _mem_kv_kernel,
        grid=(DEPTH, bsz),
        in_specs=[
            pl.BlockSpec((None, N_MEM, D_MODEL), lambda l, b: (b, 0, 0)),
            pl.BlockSpec((None, 1, D_MODEL), lambda l, b: (l, 0, 0)),
            pl.BlockSpec((None, D_MODEL, 2 * D_MODEL), lambda l, b: (l, 0, 0)),
        ],
        out_specs=[out_spec, out_spec],
        out_shape=[out, out],
        compiler_params=_params("parallel", "parallel"),
        name="mem_kv",
    )(mem, g_mem.reshape(DEPTH, 1, D_MODEL), w_kv)


def _ca_kernel(x_ref, gpre_ref, gpost_ref, wq_ref, wo_ref, mk_ref, mv_ref, o_ref):
    x = x_ref[...]
    xn = _rms(x, gpre_ref[...]).astype(BF16)
    q = (_dot(xn, wq_ref[...]) * (CA_HEAD_DIM ** -0.5)).astype(BF16)
    cols = [slice(h * CA_HEAD_DIM, (h + 1) * CA_HEAD_DIM) for h in range(CA_HEADS)]
    scores = [_dot_nt(q[:, c], mk_ref[:, c].astype(BF16)) for c in cols]
    probs = [jnp.exp(s - jnp.max(s, axis=-1, keepdims=True)) for s in scores]
    dens = [jnp.sum(p, axis=-1, keepdims=True) for p in probs]
    outs = [_dot(p.astype(BF16), mv_ref[:, c].astype(BF16)) for p, c in zip(probs, cols)]
    o = jnp.concatenate([a / d for a, d in zip(outs, dens)], axis=-1).astype(BF16)
    o_ref[...] = x + _rms(_dot(o, wo_ref[...]), gpost_ref[...])


def _cross_attention_prompt(x, g_pre, g_post, w_q, w_o, mk, mv, layer, seq):
    n = x.shape[0]
    tm = CA_TOKENS
    per_seq = seq // tm
    mem_spec = pl.BlockSpec((None, None, N_MEM, D_MODEL), lambda i: (layer, i // per_seq, 0, 0))
    w_spec = pl.BlockSpec((None, D_MODEL, D_MODEL), lambda i: (layer, 0, 0))
    return pl.pallas_call(
        _ca_kernel,
        grid=(n // tm,),
        in_specs=[
            pl.BlockSpec((tm, D_MODEL), lambda i: (i, 0)),
            _const_spec((1, D_MODEL)),
            _const_spec((1, D_MODEL)),
            w_spec,
            w_spec,
            mem_spec,
            mem_spec,
        ],
        out_specs=pl.BlockSpec((tm, D_MODEL), lambda i: (i, 0)),
        out_shape=jax.ShapeDtypeStruct((n, D_MODEL), F32),
        compiler_params=_params("parallel"),
        name="cross_attn_prompt",
    )(x, g_pre, g_post, w_q, w_o, mk, mv)


def _stack_rows(pieces):
    n, w = len(pieces), pieces[0].shape[1]
    sub = lax.broadcasted_iota(jnp.int32, (n, w), 0)
    out = jnp.broadcast_to(pieces[0], (n, w))
    for r in range(1, n):
        out = jnp.where(sub == r, jnp.broadcast_to(pieces[r], (n, w)), out)
    return out


def _ca_sample_kernel(x_ref, gpre_ref, gpost_ref, wq_ref, wo_ref, mk_ref, mv_ref, o_ref, q_ref, att_ref):
    i = pl.program_id(0)

    @pl.when(i == 0)
    def _():
        xn = _rms(x_ref[...], gpre_ref[...]).astype(BF16)
        q_ref[...] = _dot(xn, wq_ref[...]) * (CA_HEAD_DIM ** -0.5)

    for b in range(SAMPLE_CA_BLOCK):
        row = i * SAMPLE_CA_BLOCK + b
        q = q_ref[pl.ds(row, 1), :]
        piece = lambda a, r: a[:, (r % CA_HEADS) * CA_HEAD_DIM + (r // CA_HEADS) * LANES:][:, :LANES]
        q8 = _stack_rows([piece(q, r) for r in range(CA_ROWS)])
        prod = (mk_ref[b] * q8[None]).reshape(N_MEM * CA_ROWS, LANES).astype(BF16)
        part = _dot(prod, jnp.ones((LANES, LANES), BF16)).reshape(N_MEM, CA_ROWS, LANES)
        s = part
        for j in range(1, CA_DIM_TILES):
            s = s + pltpu.roll(part, j * CA_HEADS, axis=1)
        p = jnp.exp(s - jnp.max(s, axis=0, keepdims=True))
        den = jnp.sum(p, axis=0)
        o8 = jnp.sum(p * mv_ref[b], axis=0) / den
        att_ref[pl.ds(row, 1), :] = jnp.concatenate(
            [o8[j * CA_HEADS + h:j * CA_HEADS + h + 1, :] for h in range(CA_HEADS) for j in range(CA_DIM_TILES)],
            axis=-1)

    @pl.when(i == pl.num_programs(0) - 1)
    def _():
        y = _dot(att_ref[...].astype(BF16), wo_ref[...])
        o_ref[...] = x_ref[...] + _rms(y, gpost_ref[...])


def _cross_attention_sample(x, g_pre, g_post, w_q, w_o, mk, mv, layer):
    n = x.shape[0]
    mem_spec = pl.BlockSpec((None, SAMPLE_CA_BLOCK, N_MEM, CA_ROWS, LANES), lambda i: (layer, i, 0, 0, 0))
    w_spec = pl.BlockSpec((None, D_MODEL, D_MODEL), lambda i: (layer, 0, 0))
    return pl.pallas_call(
        _ca_sample_kernel,
        grid=(n // SAMPLE_CA_BLOCK,),
        in_specs=[
            _const_spec((n, D_MODEL)),
            _const_spec((1, D_MODEL)),
            _const_spec((1, D_MODEL)),
            w_spec,
            w_spec,
            mem_spec,
            mem_spec,
        ],
        out_specs=_const_spec((n, D_MODEL)),
        out_shape=jax.ShapeDtypeStruct((n, D_MODEL), F32),
        scratch_shapes=[pltpu.VMEM((n, D_MODEL), F32), pltpu.VMEM((n, D_MODEL), F32)],
        compiler_params=_params("arbitrary"),
        name="cross_attn_sample",
    )(x, g_pre, g_post, w_q, w_o, mk, mv)


def _swa_kernel(sink_ref, x_ref, gpre_ref, gpost_ref, wqkv_ref, bqkv_ref, wo_ref,
                o_ref, kout_ref, vout_ref, kprev_ref, vprev_ref, q_ref):
    t = pl.program_id(1)
    tm = x_ref.shape[0]
    x = x_ref[...]
    xn = _rms(x, gpre_ref[...]).astype(BF16)
    qkv = _dot(xn, wqkv_ref[...]) + bqkv_ref[...]
    q_ref[...] = (qkv[:, :N_HEADS * HEAD_DIM] * (HEAD_DIM ** -0.5)).astype(BF16)
    k = qkv[:, N_HEADS * HEAD_DIM:N_HEADS * HEAD_DIM + KV_WIDTH]
    v = qkv[:, N_HEADS * HEAD_DIM + KV_WIDTH:]

    cur, nxt = t % 2, (t + 1) % 2

    @pl.when(t == 0)
    def _():
        kprev_ref[0] = jnp.zeros((WINDOW, KV_WIDTH), BF16)
        vprev_ref[0] = jnp.zeros((WINDOW, KV_WIDTH), BF16)

    kall = jnp.concatenate([kprev_ref[cur], k.astype(BF16)], axis=0)
    vall = jnp.concatenate([vprev_ref[cur], v.astype(BF16)], axis=0)

    row = lax.broadcasted_iota(jnp.int32, (GQA * WINDOW, 2 * WINDOW), 0)
    qi = row & (WINDOW - 1)
    kj = lax.broadcasted_iota(jnp.int32, (GQA * WINDOW, 2 * WINDOW), 1)
    band = (kj >= qi) & (kj <= qi + WINDOW)
    row_g = lax.broadcasted_iota(jnp.int32, (GQA * WINDOW, 1), 0) // WINDOW

    for n in range(tm // WINDOW):
        rows = slice(n * WINDOW, (n + 1) * WINDOW)
        kk = kall[n * WINDOW:(n + 2) * WINDOW, :]
        vv = vall[n * WINDOW:(n + 2) * WINDOW, :]
        valid = band & (kj >= WINDOW - (t * tm + n * WINDOW)) if n == 0 else band
        kv_cols = [slice(kh * HEAD_DIM, (kh + 1) * HEAD_DIM) for kh in range(N_KV_HEADS)]
        scores, sinks = [], []
        for kh in range(N_KV_HEADS):
            q4 = jnp.concatenate(
                [q_ref[rows, (kh * GQA + g) * HEAD_DIM:(kh * GQA + g + 1) * HEAD_DIM] for g in range(GQA)],
                axis=0)
            sink = jnp.full((GQA * WINDOW, 1), sink_ref[kh * GQA], F32)
            for g in range(1, GQA):
                sink = jnp.where(row_g == g, sink_ref[kh * GQA + g], sink)
            sinks.append(sink)
            scores.append(jnp.where(valid, _dot_nt(q4, kk[:, kv_cols[kh]]), NEG))
        maxes = [jnp.maximum(jnp.max(s, axis=-1, keepdims=True), sink) for s, sink in zip(scores, sinks)]
        probs = [jnp.exp(s - m).astype(BF16) for s, m in zip(scores, maxes)]
        ones = jnp.ones((2 * WINDOW, HEAD_DIM), BF16)
        outs = [_dot(p, jnp.concatenate([vv[:, kv_cols[kh]], ones], axis=1)) for kh, p in enumerate(probs)]
        heads = []
        for kh in range(N_KV_HEADS):
            den = outs[kh][:, HEAD_DIM:HEAD_DIM + 1] + jnp.exp(sinks[kh] - maxes[kh])
            o4 = outs[kh][:, :HEAD_DIM] / den
            heads += [o4[g * WINDOW:(g + 1) * WINDOW] for g in range(GQA)]
        att = jnp.concatenate(heads, axis=1).astype(BF16)
        o_ref[rows, :] = x_ref[rows, :] + _rms(_dot(att, wo_ref[...]), gpost_ref[...])

    kprev_ref[nxt] = k[tm - WINDOW:, :].astype(BF16)
    vprev_ref[nxt] = v[tm - WINDOW:, :].astype(BF16)

    kout_ref[...] = k[tm - WINDOW:, :]
    vout_ref[...] = v[tm - WINDOW:, :]


def _window_attention_prompt(x, g_pre, g_post, w_qkv, b_qkv, w_o, sinks, layer, bsz, seq):
    tm = SWA_TOKENS
    per_seq = seq // tm
    win_spec = pl.BlockSpec((None, WINDOW, KV_WIDTH), lambda b, t, *_: (b, 0, 0))
    win_shape = jax.ShapeDtypeStruct((bsz, WINDOW, KV_WIDTH), F32)
    grid_spec = pltpu.PrefetchScalarGridSpec(
        num_scalar_prefetch=1,
        grid=(bsz, per_seq),
        in_specs=[
            pl.BlockSpec((tm, D_MODEL), lambda b, t, *_: (b * per_seq + t, 0)),
            _const_spec((1, D_MODEL)),
            _const_spec((1, D_MODEL)),
            pl.BlockSpec((None, D_MODEL, QKV_WIDTH), lambda b, t, *_: (layer, 0, 0)),
            _const_spec((1, QKV_WIDTH)),
            pl.BlockSpec((None, N_HEADS * HEAD_DIM, D_MODEL), lambda b, t, *_: (layer, 0, 0)),
        ],
        out_specs=[pl.BlockSpec((tm, D_MODEL), lambda b, t, *_: (b * per_seq + t, 0)), win_spec, win_spec],
        scratch_shapes=[
            pltpu.VMEM((2, WINDOW, KV_WIDTH), BF16),
            pltpu.VMEM((2, WINDOW, KV_WIDTH), BF16),
            pltpu.VMEM((tm, N_HEADS * HEAD_DIM), BF16),
        ],
    )
    return pl.pallas_call(
        _swa_kernel,
        grid_spec=grid_spec,
        out_shape=[jax.ShapeDtypeStruct(x.shape, F32), win_shape, win_shape],
        compiler_params=_params("arbitrary", "arbitrary"),
        name="window_attn_prompt",
    )(sinks, x, g_pre, g_post, w_qkv, b_qkv, w_o)


def _swa_sample_kernel(sink_ref, x_ref, gpre_ref, gpost_ref, wqkv_ref, bqkv_ref, wo_ref, ck_ref, cv_ref,
                       o_ref, nk_ref, nv_ref, qkvt_ref, attt_ref, blkt_ref):
    i = pl.program_id(0)
    n = x_ref.shape[0]
    k0 = N_HEADS * HEAD_DIM
    v0 = k0 + KV_WIDTH

    @pl.when(i == 0)
    def _():
        xn = _rms(x_ref[...], gpre_ref[...]).astype(BF16)
        qkvt_ref[...] = (_dot(xn, wqkv_ref[...]) + bqkv_ref[...]).T
        attt_ref[...] = jnp.zeros_like(attt_ref)
        blkt_ref[...] = jnp.zeros_like(blkt_ref)

    base = i * SAMPLE_SWA_BLOCK
    qkvt = pltpu.roll(qkvt_ref[...], (n - base) % n, axis=1)
    newest = lax.broadcasted_iota(jnp.int32, (KV_WIDTH, WINDOW), 1) == WINDOW - 1
    heads3 = lambda a: a.reshape(N_HEADS, HEAD_DIM, a.shape[-1])
    per_q_head = lambda a: jnp.concatenate(
        [a[(h // GQA) * HEAD_DIM:(h // GQA + 1) * HEAD_DIM] for h in range(N_HEADS)], axis=0)
    sink = sink_ref[...]
    for b in range(SAMPLE_SWA_BLOCK):
        col = qkvt[:, b:b + 1]
        q = col[0:k0] * (HEAD_DIM ** -0.5)
        k_new, v_new = col[k0:v0], col[v0:]
        kt = ck_ref[b].reshape(KV_WIDTH, WINDOW)
        vt = cv_ref[b].reshape(KV_WIDTH, WINDOW)
        s = jnp.sum(heads3(per_q_head(kt) * q), axis=1, keepdims=True)
        s_new = jnp.sum(heads3(per_q_head(k_new) * q), axis=1, keepdims=True)
        m = jnp.maximum(jnp.maximum(jnp.max(s, axis=2, keepdims=True), s_new), sink)
        p = jnp.exp(s - m)
        p_new = jnp.exp(s_new - m)
        den = jnp.sum(p, axis=2, keepdims=True) + p_new + jnp.exp(sink - m)
        o = jnp.sum(heads3(per_q_head(vt)) * p, axis=2, keepdims=True)
        o = (o + p_new * heads3(per_q_head(v_new))) / den
        blkt_ref[:, b:b + 1] = o.reshape(N_HEADS * HEAD_DIM, 1)
        nk_ref[b] = jnp.where(newest, k_new, pltpu.roll(kt, WINDOW - 1, axis=1)).reshape(nk_ref.shape[1:])
        nv_ref[b] = jnp.where(newest, v_new, pltpu.roll(vt, WINDOW - 1, axis=1)).reshape(nv_ref.shape[1:])

    lane = lax.broadcasted_iota(jnp.int32, attt_ref.shape, 1)
    mine = (lane >= base) & (lane < base + SAMPLE_SWA_BLOCK)
    attt_ref[...] = jnp.where(mine, pltpu.roll(blkt_ref[...], base, axis=1), attt_ref[...])

    @pl.when(i == pl.num_programs(0) - 1)
    def _():
        y = _dot(attt_ref[...].T.astype(BF16), wo_ref[...])
        o_ref[...] = x_ref[...] + _rms(y, gpost_ref[...])


def _window_attention_sample(x, g_pre, g_post, w_qkv, b_qkv, w_o, sinks, cache_k, cache_v, layer):
    n = x.shape[0]
    assert n == LANES, "the sample kernel keeps one sample per lane"
    blk = SAMPLE_SWA_BLOCK
    cache_spec = pl.BlockSpec((None, blk, N_KV_HEADS, HEAD_DIM, WINDOW), lambda i, *_: (layer, i, 0, 0, 0))
    win_spec = pl.BlockSpec((blk, N_KV_HEADS, HEAD_DIM, WINDOW), lambda i, *_: (i, 0, 0, 0))
    win_shape = jax.ShapeDtypeStruct((n, N_KV_HEADS, HEAD_DIM, WINDOW), F32)
    return pl.pallas_call(
        _swa_sample_kernel,
        grid=(n // blk,),
        in_specs=[
            _const_spec((N_HEADS, 1, 1)),
            _const_spec((n, D_MODEL)),
            _const_spec((1, D_MODEL)),
            _const_spec((1, D_MODEL)),
            pl.BlockSpec((None, D_MODEL, QKV_WIDTH), lambda i, *_: (layer, 0, 0)),
            _const_spec((1, QKV_WIDTH)),
            pl.BlockSpec((None, N_HEADS * HEAD_DIM, D_MODEL), lambda i, *_: (layer, 0, 0)),
            cache_spec,
            cache_spec,
        ],
        out_specs=[_const_spec((n, D_MODEL)), win_spec, win_spec],
        out_shape=[jax.ShapeDtypeStruct(x.shape, F32), win_shape, win_shape],
        scratch_shapes=[pltpu.VMEM((QKV_WIDTH, n), F32), pltpu.VMEM((N_HEADS * HEAD_DIM, n), F32),
                        pltpu.VMEM((N_HEADS * HEAD_DIM, n), F32)],
        compiler_params=_params("arbitrary"),
        name="window_attn_sample",
    )(sinks.reshape(N_HEADS, 1, 1), x, g_pre, g_post, w_qkv, b_qkv, w_o, cache_k, cache_v)


def _ssm_table_kernel(ar_ref, ai_ref, ldt_ref, br_ref, bi_ref, w_ref, lam_ref):
    P, PG, GS = SSM_STATE, SSM_PAIR_GROUPS, SSM_GROUP
    ar, ai, dt = ar_ref[...], ai_ref[...], jnp.exp(ldt_ref[...])
    mag = jnp.exp(ar * dt)
    lr, li = mag * jnp.cos(ai * dt), mag * jnp.sin(ai * dt)
    den = ar * ar + ai * ai
    nr, ni = lr - 1.0, li
    zr = (nr * ar + ni * ai) / den
    zi = (ni * ar - nr * ai) / den
    br, bi = br_ref[...], bi_ref[...]
    wr = zr * br - zi * bi
    wi = zr * bi + zi * br
    pr, pi = lr, li
    rows = PG * GS
    own = (lax.broadcasted_iota(jnp.int32, (rows, PG * P), 0) // GS
           == lax.broadcasted_iota(jnp.int32, (rows, PG * P), 1) // P)
    for k in range(SSM_LAGS):
        parts = [jnp.where(own, jnp.concatenate([w] * PG, axis=1), 0.0) for w in (wr, wi)]
        w_ref[k * rows:(k + 1) * rows, :] = jnp.concatenate(parts, axis=1).astype(BF16)
        wr, wi = lr * wr - li * wi, lr * wi + li * wr
        if k > 0:
            pr, pi = lr * pr - li * pi, lr * pi + li * pr
    lam_ref[0] = lr
    lam_ref[1] = li
    lam_ref[2] = pr
    lam_ref[3] = pi


def _ssm_tables(a_re, a_im, log_dt, b_re, b_im, c_re, c_im):
    G, P, GS, R, NT = SSM_GROUPS, SSM_STATE, SSM_GROUP, SSM_LAGS, SSM_TILES
    NQ, PG = SSM_TILE_PAIRS, SSM_PAIR_GROUPS
    iota = lambda n: jnp.arange(n, dtype=jnp.int32)
    clusters = NT * NQ
    per_chan = lambda a: jnp.broadcast_to(a.astype(F32).reshape(clusters, PG, 1, P), (clusters, PG, GS, P)).reshape(
        clusters, SSM_PAIR_CH, P)
    b_t = lambda b: b.astype(F32).transpose(0, 2, 1).reshape(clusters, SSM_PAIR_CH, P)
    in_spec = pl.BlockSpec((None, SSM_PAIR_CH, P), lambda i: (i, 0, 0))
    w_pair, lam = pl.pallas_call(
        _ssm_table_kernel,
        grid=(clusters,),
        in_specs=[in_spec] * 5,
        out_specs=[pl.BlockSpec((None, R * SSM_PAIR_CH, SSM_PAIR_WIDTH), lambda i: (i, 0, 0)),
                   pl.BlockSpec((None, 4, SSM_PAIR_CH, P), lambda i: (i, 0, 0, 0))],
        out_shape=[jax.ShapeDtypeStruct((clusters, R * SSM_PAIR_CH, SSM_PAIR_WIDTH), BF16),
                   jax.ShapeDtypeStruct((clusters, 4, SSM_PAIR_CH, P), F32)],
        compiler_params=_params("parallel"),
        name="ssm_tables",
    )(per_chan(a_re), per_chan(a_im), per_chan(jnp.broadcast_to(log_dt[:, None], (G, P))), b_t(b_re), b_t(b_im))
    w_pair = w_pair.reshape(NT, NQ, R * SSM_PAIR_CH, SSM_PAIR_WIDTH)
    lam = lam[:, :, ::GS, :].transpose(1, 0, 2, 3).reshape(4, G * P)
    w0 = jnp.tile(w_pair[:, :, :SSM_PAIR_CH, None, :], (1, 1, 1, NQ, 1))
    w0 = jnp.where((iota(NQ)[:, None, None, None] == iota(NQ)[None, None, :, None])[None], w0, 0).reshape(
        NT, LANES, NQ * SSM_PAIR_WIDTH)
    ct = jnp.stack([c_re, c_im]).astype(BF16).reshape(2, NT, NQ, PG, GS, P).transpose(1, 2, 0, 3, 5, 4)
    ct = jnp.tile(ct.reshape(NT, NQ, SSM_PAIR_WIDTH, GS), (1, 1, 1, LANES // GS))
    want_slot = iota(NQ)[:, None] * PG + ((iota(SSM_PAIR_WIDTH) // P) % PG)[None, :]
    c_pair = jnp.where(want_slot[:, :, None] == (iota(LANES) // GS)[None, None, :], ct, 0)

    def lam_rows(v):
        v = v.reshape(NT, NQ, SSM_PAIR_STATE)
        return jnp.concatenate([v, v], axis=-1).reshape(NT, 1, NQ * SSM_PAIR_WIDTH)

    return (w_pair, w0, c_pair, lam_rows(lam[0]), lam_rows(lam[1]), lam_rows(lam[2]), lam_rows(lam[3]))


def _glu_tail(x, u, y, d_ref, wglu_ref, bglu_ref, gpost_ref):
    y = y + d_ref[...] * u
    y = 0.5 * y * (1.0 + lax.erf(y * (2.0 ** -0.5)))
    z = _dot(y.astype(BF16), wglu_ref[...]) + bglu_ref[...]
    out = z[:, :D_MODEL] * jax.nn.sigmoid(z[:, D_MODEL:])
    return x + _rms(out, gpost_ref[...])


def _swap_re_im(a):
    tiles = a.shape[-1] // SSM_PAIR_STATE
    return jnp.concatenate(
        [a[:, (j ^ 1) * SSM_PAIR_STATE:((j ^ 1) + 1) * SSM_PAIR_STATE] for j in range(tiles)], axis=1)


def _ssm_kernel(x_ref, gpre_ref, gpost_ref, wp_ref, cp_ref, lamr_ref, lami_ref, d_ref, wglu_ref, bglu_ref,
                o_ref, hout_ref, ubuf_ref, uprev_ref, h2_ref, y_ref, carry_ref):
    t = pl.program_id(1)
    tm = x_ref.shape[0]
    NQ, PS, PW = SSM_TILE_PAIRS, SSM_PAIR_STATE, SSM_PAIR_WIDTH
    S2 = NQ * PW
    cur, nxt = t % 2, (t + 1) % 2

    @pl.when(t == 0)
    def _():
        uprev_ref[0] = jnp.zeros((SSM_LAGS, D_MODEL), F32)
        carry_ref[...] = jnp.zeros_like(carry_ref)

    x = x_ref[...]
    u = _rms(x, gpre_ref[...])
    ubuf_ref[0:SSM_LAGS, :] = uprev_ref[cur]
    ubuf_ref[SSM_LAGS:, :] = u

    slot = lax.broadcasted_iota(jnp.int32, (tm, LANES), 1) // SSM_PAIR_CH
    im_lane = (lax.broadcasted_iota(jnp.int32, (1, S2), 1) // SSM_PAIR_STATE) % 2 == 1

    for c in range(SSM_TILES):
        cols = slice(c * LANES, (c + 1) * LANES)
        h_ref = h2_ref.at[c % 2]
        lagged = [ubuf_ref[SSM_LAGS - k:SSM_LAGS - k + tm, cols] for k in range(SSM_LAGS)]
        for q in range(NQ):
            halves = []
            for half in range(SSM_LAGS // NQ):
                acc = None
                for m in range(NQ):
                    piece = lagged[half * NQ + m]
                    if m != q:
                        piece = pltpu.roll(piece, (SSM_PAIR_CH * (m - q)) % LANES, axis=1)
                    acc = piece if acc is None else jnp.where(slot == m, piece, acc)
                halves.append(acc)
            lhs = jnp.concatenate(halves, axis=1).astype(BF16)
            h_ref[:, q * PW:(q + 1) * PW] = _dot(lhs, wp_ref[c, q])
        lr = jnp.broadcast_to(lamr_ref[c], (SUBLANES, S2))
        li = jnp.broadcast_to(jnp.where(im_lane, lami_ref[c], -lami_ref[c]), (SUBLANES, S2))

        def slab(m, carry):
            r0 = pl.multiple_of(m * SUBLANES, SUBLANES)
            new = h_ref[pl.ds(r0, SUBLANES), :] + (lr * carry + li * _swap_re_im(carry))
            h_ref[pl.ds(r0, SUBLANES), :] = new
            return new

        carry_ref[c] = lax.fori_loop(0, tm // SUBLANES, slab, carry_ref[c], unroll=True)
        y = None
        for q in range(NQ):
            h_re = h_ref[:, q * PW:q * PW + PS].astype(BF16)
            h_im = h_ref[:, q * PW + PS:(q + 1) * PW].astype(BF16)
            yq = _dot(h_re, cp_ref[c, q, 0:PS, :]) - _dot(h_im, cp_ref[c, q, PS:PW, :])
            y = yq if y is None else y + yq
        y_ref[:, cols] = y

    o_ref[...] = _glu_tail(x, u, y_ref[...], d_ref, wglu_ref, bglu_ref, gpost_ref)
    uprev_ref[nxt] = u[tm - SSM_LAGS:, :]

    @pl.when(t == pl.num_programs(1) - 1)
    def _():
        hout_ref[...] = carry_ref[...]


def _ssm_prompt(x, g_pre, g_post, tables, d_skip, w_glu, b_glu, layer, bsz, seq):
    w_pair, _, c_pair, _, _, lamk_re, lamk_im = tables
    tm = SSM_TOKENS
    per_seq = seq // tm
    S2 = 2 * SSM_TILE_STATE
    once = pl.Buffered(1)
    out, h_last = pl.pallas_call(
        _ssm_kernel,
        grid=(bsz, per_seq),
        in_specs=[
            pl.BlockSpec((tm, D_MODEL), lambda b, t: (b * per_seq + t, 0)),
            _const_spec((1, D_MODEL)),
            _const_spec((1, D_MODEL)),
            pl.BlockSpec(w_pair.shape, lambda b, t: (0, 0, 0, 0), pipeline_mode=once),
            pl.BlockSpec(c_pair.shape, lambda b, t: (0, 0, 0, 0), pipeline_mode=once),
            _const_spec(lamk_re.shape),
            _const_spec(lamk_im.shape),
            _const_spec((1, D_MODEL)),
            pl.BlockSpec((None, D_MODEL, 2 * D_MODEL), lambda b, t: (layer, 0, 0), pipeline_mode=once),
            _const_spec((1, 2 * D_MODEL)),
        ],
        out_specs=[
            pl.BlockSpec((tm, D_MODEL), lambda b, t: (b * per_seq + t, 0)),
            pl.BlockSpec((None, SSM_TILES, SUBLANES, S2), lambda b, t: (b, 0, 0, 0)),
        ],
        out_shape=[jax.ShapeDtypeStruct(x.shape, F32),
                   jax.ShapeDtypeStruct((bsz, SSM_TILES, SUBLANES, S2), F32)],
        scratch_shapes=[
            pltpu.VMEM((tm + SSM_LAGS, D_MODEL), F32),
            pltpu.VMEM((2, SSM_LAGS, D_MODEL), F32),
            pltpu.VMEM((2, tm, S2), F32),
            pltpu.VMEM((tm, D_MODEL), F32),
            pltpu.VMEM((SSM_TILES, SUBLANES, S2), F32),
        ],
        compiler_params=_params("arbitrary", "arbitrary"),
        name="ssm_prompt",
    )(x, g_pre, g_post, w_pair, c_pair, lamk_re, lamk_im, d_skip, w_glu, b_glu)
    h_last = h_last[:, :, SUBLANES - 1, :].reshape(bsz, SSM_TILES, SSM_TILE_PAIRS, 2, SSM_PAIR_STATE)
    shape = (bsz, SSM_GROUPS, SSM_STATE)
    return out, h_last[:, :, :, 0, :].reshape(shape), h_last[:, :, :, 1, :].reshape(shape)


def _ssm_sample_kernel(x_ref, sre_ref, sim_ref, gpre_ref, gpost_ref, w0_ref, cp_ref, lamr_ref, lami_ref,
                       d_ref, wglu_ref, bglu_ref, o_ref, nre_ref, nim_ref, y_ref):
    PS, PW = SSM_PAIR_STATE, SSM_PAIR_WIDTH
    x = x_ref[...]
    u = _rms(x, gpre_ref[...])
    for c in range(SSM_TILES):
        cols = slice(c * LANES, (c + 1) * LANES)
        bu = _dot(u[:, cols].astype(BF16), w0_ref[c])
        y = None
        for q in range(SSM_TILE_PAIRS):
            st = slice(c * SSM_TILE_STATE + q * PS, c * SSM_TILE_STATE + (q + 1) * PS)
            lr = lamr_ref[c][:, q * PW:q * PW + PS]
            li = lami_ref[c][:, q * PW:q * PW + PS]
            h0r, h0i = sre_ref[:, st], sim_ref[:, st]
            hr = bu[:, q * PW:q * PW + PS] + (lr * h0r - li * h0i)
            hi = bu[:, q * PW + PS:(q + 1) * PW] + (lr * h0i + li * h0r)
            nre_ref[:, st] = hr
            nim_ref[:, st] = hi
            yq = _dot(jnp.concatenate([hr, -hi], axis=1).astype(BF16), cp_ref[c, q])
            y = yq if y is None else y + yq
        y_ref[:, cols] = y
    o_ref[...] = _glu_tail(x, u, y_ref[...], d_ref, wglu_ref, bglu_ref, gpost_ref)


def _ssm_sample(x, state_re, state_im, g_pre, g_post, tables, d_skip, w_glu, b_glu, layer):
    _, w0, c_pair, lam1_re, lam1_im, _, _ = tables
    n = x.shape[0]
    flat = (n, SSM_GROUPS * SSM_STATE)
    st = jax.ShapeDtypeStruct(flat, F32)
    S2 = 2 * SSM_TILE_STATE
    out, nre, nim = pl.pallas_call(
        _ssm_sample_kernel,
        grid=(1,),
        in_specs=[
            _const_spec((n, D_MODEL)),
            _const_spec(flat),
            _const_spec(flat),
            _const_spec((1, D_MODEL)),
            _const_spec((1, D_MODEL)),
            _const_spec(w0.shape),
            _const_spec(c_pair.shape),
            _const_spec(lam1_re.shape),
            _const_spec(lam1_im.shape),
            _const_spec((1, D_MODEL)),
            pl.BlockSpec((None, D_MODEL, 2 * D_MODEL), lambda i: (layer, 0, 0)),
            _const_spec((1, 2 * D_MODEL)),
        ],
        out_specs=[_const_spec((n, D_MODEL)), _const_spec(flat), _const_spec(flat)],
        out_shape=[jax.ShapeDtypeStruct(x.shape, F32), st, st],
        scratch_shapes=[pltpu.VMEM((n, D_MODEL), F32)],
        compiler_params=_params("arbitrary"),
        name="ssm_sample",
    )(x, state_re.reshape(flat), state_im.reshape(flat), g_pre, g_post, w0, c_pair, lam1_re, lam1_im,
      d_skip, w_glu, b_glu)
    shape = (n, SSM_GROUPS, SSM_STATE)
    return out, nre.reshape(shape), nim.reshape(shape)


def kernel(x_prompt, x_sample, mem_prompt, state_ssm_re, state_ssm_im, cache_win_k, cache_win_v, cache_mem_k, cache_mem_v, norm_g, mem_norm_g, ffn_w_in, ffn_w_out, ssm_a_re, ssm_a_im, ssm_log_dt, ssm_b_re, ssm_b_im, ssm_c_re, ssm_c_im, ssm_d, ssm_w_glu, ssm_b_glu, attn_w_qkv, attn_b_qkv, attn_w_o, attn_sinks, ca_w_q, ca_w_kv, ca_w_o):
    bp, seq, _ = x_prompt.shape
    bs = x_sample.shape[0]
    xp = x_prompt.reshape(bp * seq, D_MODEL)
    xs = x_sample.reshape(bs, D_MODEL)

    gain = lambda i, r: norm_g[i, r].astype(F32).reshape(1, D_MODEL)
    ffn_w_in_b, ffn_w_out_b = ffn_w_in.astype(BF16), ffn_w_out.astype(BF16)
    ssm_w_glu_b = ssm_w_glu.astype(BF16)
    attn_w_qkv_b, attn_w_o_b = attn_w_qkv.astype(BF16), attn_w_o.astype(BF16)
    ca_w_q_b, ca_w_o_b = ca_w_q.astype(BF16), ca_w_o.astype(BF16)

    mem_k, mem_v = _mem_kv(mem_prompt, mem_norm_g.astype(F32), ca_w_kv.astype(BF16))
    cache_win_kt = cache_win_k.transpose(0, 1, 3, 4, 2)
    cache_win_vt = cache_win_v.transpose(0, 1, 3, 4, 2)
    rows8 = lambda a: a.reshape(DEPTH, bs, N_MEM, CA_HEADS, CA_DIM_TILES, LANES).transpose(0, 1, 2, 4, 3, 5).reshape(
        DEPTH, bs, N_MEM, CA_ROWS, LANES)
    cache_mem_k8, cache_mem_v8 = rows8(cache_mem_k), rows8(cache_mem_v)

    ssm_re_p, ssm_im_p, ssm_re_s, ssm_im_s = [], [], [], []
    wk_p, wv_p, wk_s, wv_s = [], [], [], []
    for i in range(DEPTH):
        li = i // N_MIXERS
        xp = _half_ffn(xp, gain(i, 0), gain(i, 1), ffn_w_in_b, ffn_w_out_b, i, 0, FFN_TOKENS)
        xs = _half_ffn(xs, gain(i, 0), gain(i, 1), ffn_w_in_b, ffn_w_out_b, i, 0, FFN_TOKENS)
        if i % N_MIXERS == 0:
            tables = _ssm_tables(ssm_a_re[li], ssm_a_im[li], ssm_log_dt[li], ssm_b_re[li], ssm_b_im[li],
                                 ssm_c_re[li], ssm_c_im[li])
            d_skip = ssm_d[li].astype(F32).reshape(1, D_MODEL)
            b_glu = ssm_b_glu[li].astype(F32).reshape(1, 2 * D_MODEL)
            xp, hr_p, hi_p = _ssm_prompt(xp, gain(i, 2), gain(i, 3), tables, d_skip, ssm_w_glu_b, b_glu, li,
                                         bp, seq)
            xs, hr_s, hi_s = _ssm_sample(xs, state_ssm_re[li], state_ssm_im[li], gain(i, 2), gain(i, 3), tables,
                                         d_skip, ssm_w_glu_b, b_glu, li)
            ssm_re_p.append(hr_p); ssm_im_p.append(hi_p)
            ssm_re_s.append(hr_s); ssm_im_s.append(hi_s)
        else:
            b_qkv = attn_b_qkv[li].astype(F32).reshape(1, QKV_WIDTH)
            sinks = attn_sinks[li].astype(F32)
            xp, bk_p, bv_p = _window_attention_prompt(xp, gain(i, 2), gain(i, 3), attn_w_qkv_b, b_qkv,
                                                      attn_w_o_b, sinks, li, bp, seq)
            xs, bk_s, bv_s = _window_attention_sample(xs, gain(i, 2), gain(i, 3), attn_w_qkv_b, b_qkv, attn_w_o_b,
                                                      sinks, cache_win_kt, cache_win_vt, li)
            wk_p.append(bk_p.reshape(bp, WINDOW, N_KV_HEADS, HEAD_DIM))
            wv_p.append(bv_p.reshape(bp, WINDOW, N_KV_HEADS, HEAD_DIM))
            wk_s.append(bk_s); wv_s.append(bv_s)
        xp = _cross_attention_prompt(xp, gain(i, 4), gain(i, 5), ca_w_q_b, ca_w_o_b, mem_k, mem_v, i, seq)
        xs = _cross_attention_sample(xs, gain(i, 4), gain(i, 5), ca_w_q_b, ca_w_o_b, cache_mem_k8, cache_mem_v8, i)
        xp = _half_ffn(xp, gain(i, 6), gain(i, 7), ffn_w_in_b, ffn_w_out_b, i, 1, FFN_TOKENS)
        xs = _half_ffn(xs, gain(i, 6), gain(i, 7), ffn_w_in_b, ffn_w_out_b, i, 1, FFN_TOKENS)

    mem_shape = (DEPTH, bp, N_MEM, CA_HEADS, CA_HEAD_DIM)
    return (xp.reshape(bp, seq, D_MODEL), xs.reshape(bs, 1, D_MODEL),
            jnp.stack(ssm_re_p), jnp.stack(ssm_im_p), jnp.stack(wk_p), jnp.stack(wv_p),
            mem_k.reshape(mem_shape), mem_v.reshape(mem_shape),
            jnp.stack(ssm_re_s), jnp.stack(ssm_im_s),
            jnp.stack(wk_s).transpose(0, 1, 4, 2, 3), jnp.stack(wv_s).transpose(0, 1, 4, 2, 3))
```

```python
import jax
import jax.numpy as jnp
from jax import lax
from jax.experimental import pallas as pl
from jax.experimental.pallas import tpu as pltpu

F32 = jnp.float32
BF16 = jnp.bfloat16

D_MODEL = 1024
DEPTH = 4
N_MIXERS = 2
SSM_GROUP = 16
SSM_GROUPS = D_MODEL // SSM_GROUP
SSM_STATE = 64
HEAD_DIM = 64
N_HEADS = D_MODEL // HEAD_DIM
N_KV_HEADS = 4
GQA = N_HEADS // N_KV_HEADS
WINDOW = 128
KV_WIDTH = N_KV_HEADS * HEAD_DIM
QKV_WIDTH = (N_HEADS + 2 * N_KV_HEADS) * HEAD_DIM
N_MEM = 256
CA_HEADS = 4
CA_HEAD_DIM = D_MODEL // CA_HEADS
CA_DIM_TILES = CA_HEAD_DIM // 128
CA_ROWS = CA_HEADS * CA_DIM_TILES
D_FF = ((8 * D_MODEL // 3 + 127) // 128) * 128
FFN_RES = 0.5
EPS = 1e-6
NEG = -1e30

SUBLANES = 8
LANES = 128
VMEM_LIMIT_BYTES = 56 * 1024 * 1024

SSM_LAGS = SUBLANES
SSM_TILE_GROUPS = LANES // SSM_GROUP
SSM_TILES = SSM_GROUPS // SSM_TILE_GROUPS
SSM_TILE_STATE = SSM_TILE_GROUPS * SSM_STATE
SSM_PAIR_GROUPS = 4
SSM_TILE_PAIRS = SSM_TILE_GROUPS // SSM_PAIR_GROUPS
SSM_PAIR_CH = SSM_PAIR_GROUPS * SSM_GROUP
SSM_PAIR_STATE = SSM_PAIR_GROUPS * SSM_STATE
SSM_PAIR_WIDTH = 2 * SSM_PAIR_STATE

FFN_CHUNK = 256
FFN_TOKENS = 1024
CA_TOKENS = 1024
SWA_TOKENS = 1024
SSM_TOKENS = 512
SAMPLE_CA_BLOCK = 4
SAMPLE_SWA_BLOCK = 16


def _params(*sem):
    return pltpu.CompilerParams(dimension_semantics=sem, vmem_limit_bytes=VMEM_LIMIT_BYTES)


def _rms(x, g):
    r = lax.rsqrt(jnp.mean(x * x, axis=-1, keepdims=True) + EPS)
    return x * r * g


def _dot(a, b):
    return jnp.dot(a, b, preferred_element_type=F32)


def _dot_nt(a, b):
    return lax.dot_general(a, b, (((1,), (1,)), ((), ())), preferred_element_type=F32)


def _const_spec(shape):
    zeros = (0,) * len(shape)
    return pl.BlockSpec(shape, lambda *_: zeros)


def _ffn_kernel(x_ref, gpre_ref, gpost_ref, wg_ref, wu_ref, wo_ref, o_ref, xn_ref, acc_ref):
    c = pl.program_id(1)

    @pl.when(c == 0)
    def _():
        xn_ref[...] = _rms(x_ref[...], gpre_ref[...]).astype(BF16)
        acc_ref[...] = jnp.zeros_like(acc_ref)

    xn = xn_ref[...]
    gate = _dot(xn, wg_ref[...])
    up = _dot(xn, wu_ref[...])
    h = (gate * jax.nn.sigmoid(gate) * up).astype(BF16)
    acc_ref[...] += _dot(h, wo_ref[...])

    @pl.when(c == pl.num_programs(1) - 1)
    def _():
        o_ref[...] = x_ref[...] + FFN_RES * _rms(acc_ref[...], gpost_ref[...])


def _half_ffn(x, g_pre, g_post, w_in, w_out, layer, half, tokens):
    n = x.shape[0]
    tm = min(tokens, n)
    n_chunks = D_FF // FFN_CHUNK
    return pl.pallas_call(
        _ffn_kernel,
        grid=(n // tm, n_chunks),
        in_specs=[
            pl.BlockSpec((tm, D_MODEL), lambda i, c: (i, 0)),
            _const_spec((1, D_MODEL)),
            _const_spec((1, D_MODEL)),
            pl.BlockSpec((None, None, D_MODEL, FFN_CHUNK), lambda i, c: (layer, half, 0, c)),
            pl.BlockSpec((None, None, D_MODEL, FFN_CHUNK), lambda i, c: (layer, half, 0, c + n_chunks)),
            pl.BlockSpec((None, None, FFN_CHUNK, D_MODEL), lambda i, c: (layer, half, c, 0)),
        ],
        out_specs=pl.BlockSpec((tm, D_MODEL), lambda i, c: (i, 0)),
        out_shape=jax.ShapeDtypeStruct((n, D_MODEL), F32),
        scratch_shapes=[pltpu.VMEM((tm, D_MODEL), BF16), pltpu.VMEM((tm, D_MODEL), F32)],
        compiler_params=_params("parallel", "arbitrary"),
        name="half_ffn",
    )(x, g_pre, g_post, w_in, w_in, w_out)


def _mem_kv_kernel(mem_ref, g_ref, w_ref, k_ref, v_ref):
    mn = _rms(mem_ref[...], g_ref[...]).astype(BF16)
    kv = _dot(mn, w_ref[...])
    k_ref[...] = kv[:, :D_MODEL]
    v_ref[...] = kv[:, D_MODEL:]


def _mem_kv(mem, g_mem, w_kv):
    bsz = mem.shape[0]
    out = jax.ShapeDtypeStruct((DEPTH, bsz, N_MEM, D_MODEL), F32)
    out_spec = pl.BlockSpec((None, None, N_MEM, D_MODEL), lambda l, b: (l, b, 0, 0))
    return pl.pallas_call(
        _mem_kv_kernel,
        grid=(DEPTH, bsz),
        in_specs=[
            pl.BlockSpec((None, N_MEM, D_MODEL), lambda l, b: (b, 0, 0)),
            pl.BlockSpec((None, 1, D_MODEL), lambda l, b: (l, 0, 0)),
            pl.BlockSpec((None, D_MODEL, 2 * D_MODEL), lambda l, b: (l, 0, 0)),
        ],
        out_specs=[out_spec, out_spec],
        out_shape=[out, out],
        compiler_params=_params("parallel", "parallel"),
        name="mem_kv",
    )(mem, g_mem.reshape(DEPTH, 1, D_MODEL), w_kv)


def _ca_kernel(x_ref, gpre_ref, gpost_ref, wq_ref, wo_ref, mk_ref, mv_ref, o_ref):
    x = x_ref[...]
    xn = _rms(x, gpre_ref[...]).astype(BF16)
    q = (_dot(xn, wq_ref[...]) * (CA_HEAD_DIM ** -0.5)).astype(BF16)
    cols = [slice(h * CA_HEAD_DIM, (h + 1) * CA_HEAD_DIM) for h in range(CA_HEADS)]
    scores = [_dot_nt(q[:, c], mk_ref[:, c].astype(BF16)) for c in cols]
    probs = [jnp.exp(s - jnp.max(s, axis=-1, keepdims=True)) for s in scores]
    dens = [jnp.sum(p, axis=-1, keepdims=True) for p in probs]
    outs = [_dot(p.astype(BF16), mv_ref[:, c].astype(BF16)) for p, c in zip(probs, cols)]
    o = jnp.concatenate([a / d for a, d in zip(outs, dens)], axis=-1).astype(BF16)
    o_ref[...] = x + _rms(_dot(o, wo_ref[...]), gpost_ref[...])


def _cross_attention_prompt(x, g_pre, g_post, w_q, w_o, mk, mv, layer, seq):
    n = x.shape[0]
    tm = CA_TOKENS
    per_seq = seq // tm
    mem_spec = pl.BlockSpec((None, None, N_MEM, D_MODEL), lambda i: (layer, i // per_seq, 0, 0))
    w_spec = pl.BlockSpec((None, D_MODEL, D_MODEL), lambda i: (layer, 0, 0))
    return pl.pallas_call(
        _ca_kernel,
        grid=(n // tm,),
        in_specs=[
            pl.BlockSpec((tm, D_MODEL), lambda i: (i, 0)),
            _const_spec((1, D_MODEL)),
            _const_spec((1, D_MODEL)),
            w_spec,
            w_spec,
            mem_spec,
            mem_spec,
        ],
        out_specs=pl.BlockSpec((tm, D_MODEL), lambda i: (i, 0)),
        out_shape=jax.ShapeDtypeStruct((n, D_MODEL), F32),
        compiler_params=_params("parallel"),
        name="cross_attn_prompt",
    )(x, g_pre, g_post, w_q, w_o, mk, mv)


def _stack_rows(pieces):
    n, w = len(pieces), pieces[0].shape[1]
    sub = lax.broadcasted_iota(jnp.int32, (n, w), 0)
    out = jnp.broadcast_to(pieces[0], (n, w))
    for r in range(1, n):
        out = jnp.where(sub == r, jnp.broadcast_to(pieces[r], (n, w)), out)
    return out


def _ca_sample_kernel(x_ref, gpre_ref, gpost_ref, wq_ref, wo_ref, mk_ref, mv_ref, o_ref, q_ref, att_ref):
    i = pl.program_id(0)

    @pl.when(i == 0)
    def _():
        xn = _rms(x_ref[...], gpre_ref[...]).astype(BF16)
        q_ref[...] = _dot(xn, wq_ref[...]) * (CA_HEAD_DIM ** -0.5)

    for b in range(SAMPLE_CA_BLOCK):
        row = i * SAMPLE_CA_BLOCK + b
        q = q_ref[pl.ds(row, 1), :]
        piece = lambda a, r: a[:, (r % CA_HEADS) * CA_HEAD_DIM + (r // CA_HEADS) * LANES:][:, :LANES]
        q8 = _stack_rows([piece(q, r) for r in range(CA_ROWS)])
        prod = (mk_ref[b] * q8[None]).reshape(N_MEM * CA_ROWS, LANES).astype(BF16)
        part = _dot(prod, jnp.ones((LANES, LANES), BF16)).reshape(N_MEM, CA_ROWS, LANES)
        s = part
        for j in range(1, CA_DIM_TILES):
            s = s + pltpu.roll(part, j * CA_HEADS, axis=1)
        p = jnp.exp(s - jnp.max(s, axis=0, keepdims=True))
        den = jnp.sum(p, axis=0)
        o8 = jnp.sum(p * mv_ref[b], axis=0) / den
        att_ref[pl.ds(row, 1), :] = jnp.concatenate(
            [o8[j * CA_HEADS + h:j * CA_HEADS + h + 1, :] for h in range(CA_HEADS) for j in range(CA_DIM_TILES)],
            axis=-1)

    @pl.when(i == pl.num_programs(0) - 1)
    def _():
        y = _dot(att_ref[...].astype(BF16), wo_ref[...])
        o_ref[...] = x_ref[...] + _rms(y, gpost_ref[...])


def _cross_attention_sample(x, g_pre, g_post, w_q, w_o, mk, mv, layer):
    n = x.shape[0]
    mem_spec = pl.BlockSpec((None, SAMPLE_CA_BLOCK, N_MEM, CA_ROWS, LANES), lambda i: (layer, i, 0, 0, 0))
    w_spec = pl.BlockSpec((None, D_MODEL, D_MODEL), lambda i: (layer, 0, 0))
    return pl.pallas_call(
        _ca_sample_kernel,
        grid=(n // SAMPLE_CA_BLOCK,),
        in_specs=[
            _const_spec((n, D_MODEL)),
            _const_spec((1, D_MODEL)),
            _const_spec((1, D_MODEL)),
            w_spec,
            w_spec,
            mem_spec,
            mem_spec,
        ],
        out_specs=_const_spec((n, D_MODEL)),
        out_shape=jax.ShapeDtypeStruct((n, D_MODEL), F32),
        scratch_shapes=[pltpu.VMEM((n, D_MODEL), F32), pltpu.VMEM((n, D_MODEL), F32)],
        compiler_params=_params("arbitrary"),
        name="cross_attn_sample",
    )(x, g_pre, g_post, w_q, w_o, mk, mv)


def _swa_kernel(sink_ref, x_ref, gpre_ref, gpost_ref, wqkv_ref, bqkv_ref, wo_ref,
                o_ref, kout_ref, vout_ref, kprev_ref, vprev_ref, q_ref):
    t = pl.program_id(1)
    tm = x_ref.shape[0]
    x = x_ref[...]
    xn = _rms(x, gpre_ref[...]).astype(BF16)
    qkv = _dot(xn, wqkv_ref[...]) + bqkv_ref[...]
    q_ref[...] = (qkv[:, :N_HEADS * HEAD_DIM] * (HEAD_DIM ** -0.5)).astype(BF16)
    k = qkv[:, N_HEADS * HEAD_DIM:N_HEADS * HEAD_DIM + KV_WIDTH]
    v = qkv[:, N_HEADS * HEAD_DIM + KV_WIDTH:]

    cur, nxt = t % 2, (t + 1) % 2

    @pl.when(t == 0)
    def _():
        kprev_ref[0] = jnp.zeros((WINDOW, KV_WIDTH), BF16)
        vprev_ref[0] = jnp.zeros((WINDOW, KV_WIDTH), BF16)

    kall = jnp.concatenate([kprev_ref[cur], k.astype(BF16)], axis=0)
    vall = jnp.concatenate([vprev_ref[cur], v.astype(BF16)], axis=0)

    row = lax.broadcasted_iota(jnp.int32, (GQA * WINDOW, 2 * WINDOW), 0)
    qi = row & (WINDOW - 1)
    kj = lax.broadcasted_iota(jnp.int32, (GQA * WINDOW, 2 * WINDOW), 1)
    band = (kj >= qi) & (kj <= qi + WINDOW)
    row_g = lax.broadcasted_iota(jnp.int32, (GQA * WINDOW, 1), 0) // WINDOW

    for n in range(tm // WINDOW):
        rows = slice(n * WINDOW, (n + 1) * WINDOW)
        kk = kall[n * WINDOW:(n + 2) * WINDOW, :]
        vv = vall[n * WINDOW:(n + 2) * WINDOW, :]
        valid = band & (kj >= WINDOW - (t * tm + n * WINDOW)) if n == 0 else band
        kv_cols = [slice(kh * HEAD_DIM, (kh + 1) * HEAD_DIM) for kh in range(N_KV_HEADS)]
        scores, sinks = [], []
        for kh in range(N_KV_HEADS):
            q4 = jnp.concatenate(
                [q_ref[rows, (kh * GQA + g) * HEAD_DIM:(kh * GQA + g + 1) * HEAD_DIM] for g in range(GQA)],
                axis=0)
            sink = jnp.full((GQA * WINDOW, 1), sink_ref[kh * GQA], F32)
            for g in range(1, GQA):
                sink = jnp.where(row_g == g, sink_ref[kh * GQA + g], sink)
            sinks.append(sink)
            scores.append(jnp.where(valid, _dot_nt(q4, kk[:, kv_cols[kh]]), NEG))
        maxes = [jnp.maximum(jnp.max(s, axis=-1, keepdims=True), sink) for s, sink in zip(scores, sinks)]
        probs = [jnp.exp(s - m).astype(BF16) for s, m in zip(scores, maxes)]
        ones = jnp.ones((2 * WINDOW, HEAD_DIM), BF16)
        outs = [_dot(p, jnp.concatenate([vv[:, kv_cols[kh]], ones], axis=1)) for kh, p in enumerate(probs)]
        heads = []
        for kh in range(N_KV_HEADS):
            den = outs[kh][:, HEAD_DIM:HEAD_DIM + 1] + jnp.exp(sinks[kh] - maxes[kh])
            o4 = outs[kh][:, :HEAD_DIM] / den
            heads += [o4[g * WINDOW:(g + 1) * WINDOW] for g in range(GQA)]
        att = jnp.concatenate(heads, axis=1).astype(BF16)
        o_ref[rows, :] = x_ref[rows, :] + _rms(_dot(att, wo_ref[...]), gpost_ref[...])

    kprev_ref[nxt] = k[tm - WINDOW:, :].astype(BF16)
    vprev_ref[nxt] = v[tm - WINDOW:, :].astype(BF16)

    kout_ref[...] = k[tm - WINDOW:, :]
    vout_ref[...] = v[tm - WINDOW:, :]


def _window_attention_prompt(x, g_pre, g_post, w_qkv, b_qkv, w_o, sinks, layer, bsz, seq):
    tm = SWA_TOKENS
    per_seq = seq // tm
    win_spec = pl.BlockSpec((None, WINDOW, KV_WIDTH), lambda b, t, *_: (b, 0, 0))
    win_shape = jax.ShapeDtypeStruct((bsz, WINDOW, KV_WIDTH), F32)
    grid_spec = pltpu.PrefetchScalarGridSpec(
        num_scalar_prefetch=1,
        grid=(bsz, per_seq),
        in_specs=[
            pl.BlockSpec((tm, D_MODEL), lambda b, t, *_: (b * per_seq + t, 0)),
            _const_spec((1, D_MODEL)),
            _const_spec((1, D_MODEL)),
            pl.BlockSpec((None, D_MODEL, QKV_WIDTH), lambda b, t, *_: (layer, 0, 0)),
            _const_spec((1, QKV_WIDTH)),
            pl.BlockSpec((None, N_HEADS * HEAD_DIM, D_MODEL), lambda b, t, *_: (layer, 0, 0)),
        ],
        out_specs=[pl.BlockSpec((tm, D_MODEL), lambda b, t, *_: (b * per_seq + t, 0)), win_spec, win_spec],
        scratch_shapes=[
            pltpu.VMEM((2, WINDOW, KV_WIDTH), BF16),
            pltpu.VMEM((2, WINDOW, KV_WIDTH), BF16),
            pltpu.VMEM((tm, N_HEADS * HEAD_DIM), BF16),
        ],
    )
    return pl.pallas_call(
        _swa_kernel,
        grid_spec=grid_spec,
        out_shape=[jax.ShapeDtypeStruct(x.shape, F32), win_shape, win_shape],
        compiler_params=_params("arbitrary", "arbitrary"),
        name="window_attn_prompt",
    )(sinks, x, g_pre, g_post, w_qkv, b_qkv, w_o)


def _swa_sample_kernel(sink_ref, x_ref, gpre_ref, gpost_ref, wqkv_ref, bqkv_ref, wo_ref, ck_ref, cv_ref,
                       o_ref, nk_ref, nv_ref, qkvt_ref, attt_ref, blkt_ref):
    i = pl.program_id(0)
    n = x_ref.shape[0]
    k0 = N_HEADS * HEAD_DIM
    v0 = k0 + KV_WIDTH

    @pl.when(i == 0)
    def _():
        xn = _rms(x_ref[...], gpre_ref[...]).astype(BF16)
        qkvt_ref[...] = (_dot(xn, wqkv_ref[...]) + bqkv_ref[...]).T
        attt_ref[...] = jnp.zeros_like(attt_ref)
        blkt_ref[...] = jnp.zeros_like(blkt_ref)

    base = i * SAMPLE_SWA_BLOCK
    qkvt = pltpu.roll(qkvt_ref[...], (n - base) % n, axis=1)
    newest = lax.broadcasted_iota(jnp.int32, (KV_WIDTH, WINDOW), 1) == WINDOW - 1
    heads3 = lambda a: a.reshape(N_HEADS, HEAD_DIM, a.shape[-1])
    per_q_head = lambda a: jnp.concatenate(
        [a[(h // GQA) * HEAD_DIM:(h // GQA + 1) * HEAD_DIM] for h in range(N_HEADS)], axis=0)
    sink = sink_ref[...]
    q_all = (qkvt[0:k0] * (HEAD_DIM ** -0.5)).astype(BF16)
    src_lane = lax.broadcasted_iota(jnp.int32, (LANES, LANES), 0)
    ones = jnp.ones((WINDOW, LANES), BF16)
    for b in range(SAMPLE_SWA_BLOCK):
        lane_b = slice(b, b + 1)
        q = _dot(q_all, jnp.where(src_lane == b, 1.0, 0.0).astype(BF16))
        k_new, v_new = qkvt[k0:v0, lane_b], qkvt[v0:, lane_b]
        kt = ck_ref[b].reshape(KV_WIDTH, WINDOW)
        vt = cv_ref[b].reshape(KV_WIDTH, WINDOW)
        s = jnp.sum(heads3(per_q_head(kt) * q), axis=1, keepdims=True)
        s_new = jnp.sum(heads3(per_q_head(k_new) * q[:, lane_b]), axis=1, keepdims=True)
        m = jnp.maximum(jnp.maximum(jnp.max(s, axis=2, keepdims=True), s_new), sink)
        p = jnp.exp(s - m)
        p_new = jnp.exp(s_new - m)
        den = jnp.sum(p, axis=2, keepdims=True) + p_new + jnp.exp(sink - m)
        pv = (heads3(per_q_head(vt)) * p).reshape(N_HEADS * HEAD_DIM, WINDOW).astype(BF16)
        o = heads3(_dot(pv, ones)[:, lane_b])
        o = (o + p_new * heads3(per_q_head(v_new))) / den
        blkt_ref[:, lane_b] = o.reshape(N_HEADS * HEAD_DIM, 1)
        nk_ref[b] = jnp.where(newest, k_new, pltpu.roll(kt, WINDOW - 1, axis=1)).reshape(nk_ref.shape[1:])
        nv_ref[b] = jnp.where(newest, v_new, pltpu.roll(vt, WINDOW - 1, axis=1)).reshape(nv_ref.shape[1:])

    lane = lax.broadcasted_iota(jnp.int32, attt_ref.shape, 1)
    mine = (lane >= base) & (lane < base + SAMPLE_SWA_BLOCK)
    attt_ref[...] = jnp.where(mine, pltpu.roll(blkt_ref[...], base, axis=1), attt_ref[...])

    @pl.when(i == pl.num_programs(0) - 1)
    def _():
        y = _dot(attt_ref[...].T.astype(BF16), wo_ref[...])
        o_ref[...] = x_ref[...] + _rms(y, gpost_ref[...])


def _window_attention_sample(x, g_pre, g_post, w_qkv, b_qkv, w_o, sinks, cache_k, cache_v, layer):
    n = x.shape[0]
    assert n == LANES, "the sample kernel keeps one sample per lane"
    blk = SAMPLE_SWA_BLOCK
    cache_spec = pl.BlockSpec((None, blk, N_KV_HEADS, HEAD_DIM, WINDOW), lambda i, *_: (layer, i, 0, 0, 0))
    win_spec = pl.BlockSpec((blk, N_KV_HEADS, HEAD_DIM, WINDOW), lambda i, *_: (i, 0, 0, 0))
    win_shape = jax.ShapeDtypeStruct((n, N_KV_HEADS, HEAD_DIM, WINDOW), F32)
    return pl.pallas_call(
        _swa_sample_kernel,
        grid=(n // blk,),
        in_specs=[
            _const_spec((N_HEADS, 1, 1)),
            _const_spec((n, D_MODEL)),
            _const_spec((1, D_MODEL)),
            _const_spec((1, D_MODEL)),
            pl.BlockSpec((None, D_MODEL, QKV_WIDTH), lambda i, *_: (layer, 0, 0)),
            _const_spec((1, QKV_WIDTH)),
            pl.BlockSpec((None, N_HEADS * HEAD_DIM, D_MODEL), lambda i, *_: (layer, 0, 0)),
            cache_spec,
            cache_spec,
        ],
        out_specs=[_const_spec((n, D_MODEL)), win_spec, win_spec],
        out_shape=[jax.ShapeDtypeStruct(x.shape, F32), win_shape, win_shape],
        scratch_shapes=[pltpu.VMEM((QKV_WIDTH, n), F32), pltpu.VMEM((N_HEADS * HEAD_DIM, n), F32),
                        pltpu.VMEM((N_HEADS * HEAD_DIM, n), F32)],
        compiler_params=_params("arbitrary"),
        name="window_attn_sample",
    )(sinks.reshape(N_HEADS, 1, 1), x, g_pre, g_post, w_qkv, b_qkv, w_o, cache_k, cache_v)


def _ssm_table_kernel(ar_ref, ai_ref, ldt_ref, br_ref, bi_ref, w_ref, lam_ref):
    P, PG, GS = SSM_STATE, SSM_PAIR_GROUPS, SSM_GROUP
    ar, ai, dt = ar_ref[...], ai_ref[...], jnp.exp(ldt_ref[...])
    mag = jnp.exp(ar * dt)
    lr, li = mag * jnp.cos(ai * dt), mag * jnp.sin(ai * dt)
    den = ar * ar + ai * ai
    nr, ni = lr - 1.0, li
    zr = (nr * ar + ni * ai) / den
    zi = (ni * ar - nr * ai) / den
    br, bi = br_ref[...], bi_ref[...]
    wr = zr * br - zi * bi
    wi = zr * bi + zi * br
    pr, pi = lr, li
    rows = PG * GS
    own = (lax.broadcasted_iota(jnp.int32, (rows, PG * P), 0) // GS
           == lax.broadcasted_iota(jnp.int32, (rows, PG * P), 1) // P)
    for k in range(SSM_LAGS):
        parts = [jnp.where(own, jnp.concatenate([w] * PG, axis=1), 0.0) for w in (wr, wi)]
        w_ref[k * rows:(k + 1) * rows, :] = jnp.concatenate(parts, axis=1).astype(BF16)
        wr, wi = lr * wr - li * wi, lr * wi + li * wr
        if k > 0:
            pr, pi = lr * pr - li * pi, lr * pi + li * pr
    lam_ref[0] = lr
    lam_ref[1] = li
    lam_ref[2] = pr
    lam_ref[3] = pi


def _ssm_tables(a_re, a_im, log_dt, b_re, b_im, c_re, c_im):
    G, P, GS, R, NT = SSM_GROUPS, SSM_STATE, SSM_GROUP, SSM_LAGS, SSM_TILES
    NQ, PG = SSM_TILE_PAIRS, SSM_PAIR_GROUPS
    iota = lambda n: jnp.arange(n, dtype=jnp.int32)
    clusters = NT * NQ
    per_chan = lambda a: jnp.broadcast_to(a.astype(F32).reshape(clusters, PG, 1, P), (clusters, PG, GS, P)).reshape(
        clusters, SSM_PAIR_CH, P)
    b_t = lambda b: b.astype(F32).transpose(0, 2, 1).reshape(clusters, SSM_PAIR_CH, P)
    in_spec = pl.BlockSpec((None, SSM_PAIR_CH, P), lambda i: (i, 0, 0))
    w_pair, lam = pl.pallas_call(
        _ssm_table_kernel,
        grid=(clusters,),
        in_specs=[in_spec] * 5,
        out_specs=[pl.BlockSpec((None, R * SSM_PAIR_CH, SSM_PAIR_WIDTH), lambda i: (i, 0, 0)),
                   pl.BlockSpec((None, 4, SSM_PAIR_CH, P), lambda i: (i, 0, 0, 0))],
        out_shape=[jax.ShapeDtypeStruct((clusters, R * SSM_PAIR_CH, SSM_PAIR_WIDTH), BF16),
                   jax.ShapeDtypeStruct((clusters, 4, SSM_PAIR_CH, P), F32)],
        compiler_params=_params("parallel"),
        name="ssm_tables",
    )(per_chan(a_re), per_chan(a_im), per_chan(jnp.broadcast_to(log_dt[:, None], (G, P))), b_t(b_re), b_t(b_im))
    w_pair = w_pair.reshape(NT, NQ, R * SSM_PAIR_CH, SSM_PAIR_WIDTH)
    lam = lam[:, :, ::GS, :].transpose(1, 0, 2, 3).reshape(4, G * P)
    w0 = jnp.tile(w_pair[:, :, :SSM_PAIR_CH, None, :], (1, 1, 1, NQ, 1))
    w0 = jnp.where((iota(NQ)[:, None, None, None] == iota(NQ)[None, None, :, None])[None], w0, 0).reshape(
        NT, LANES, NQ * SSM_PAIR_WIDTH)
    ct = jnp.stack([c_re, c_im]).astype(BF16).reshape(2, NT, NQ, PG, GS, P).transpose(1, 2, 0, 3, 5, 4)
    ct = jnp.tile(ct.reshape(NT, NQ, SSM_PAIR_WIDTH, GS), (1, 1, 1, LANES // GS))
    want_slot = iota(NQ)[:, None] * PG + ((iota(SSM_PAIR_WIDTH) // P) % PG)[None, :]
    c_pair = jnp.where(want_slot[:, :, None] == (iota(LANES) // GS)[None, None, :], ct, 0)

    def lam_rows(v):
        v = v.reshape(NT, NQ, SSM_PAIR_STATE)
        return jnp.concatenate([v, v], axis=-1).reshape(NT, 1, NQ * SSM_PAIR_WIDTH)

    return (w_pair, w0, c_pair, lam_rows(lam[0]), lam_rows(lam[1]), lam_rows(lam[2]), lam_rows(lam[3]))


def _glu_tail(x, u, y, d_ref, wglu_ref, bglu_ref, gpost_ref):
    y = y + d_ref[...] * u
    y = 0.5 * y * (1.0 + lax.erf(y * (2.0 ** -0.5)))
    z = _dot(y.astype(BF16), wglu_ref[...]) + bglu_ref[...]
    out = z[:, :D_MODEL] * jax.nn.sigmoid(z[:, D_MODEL:])
    return x + _rms(out, gpost_ref[...])


def _swap_re_im(a):
    tiles = a.shape[-1] // SSM_PAIR_STATE
    return jnp.concatenate(
        [a[:, (j ^ 1) * SSM_PAIR_STATE:((j ^ 1) + 1) * SSM_PAIR_STATE] for j in range(tiles)], axis=1)


def _ssm_kernel(x_ref, gpre_ref, gpost_ref, wp_ref, cp_ref, lamr_ref, lami_ref, d_ref, wglu_ref, bglu_ref,
                o_ref, hout_ref, ubuf_ref, uprev_ref, h2_ref, y_ref, carry_ref):
    t = pl.program_id(1)
    tm = x_ref.shape[0]
    NQ, PS, PW = SSM_TILE_PAIRS, SSM_PAIR_STATE, SSM_PAIR_WIDTH
    S2 = NQ * PW
    cur, nxt = t % 2, (t + 1) % 2

    @pl.when(t == 0)
    def _():
        uprev_ref[0] = jnp.zeros((SSM_LAGS, D_MODEL), F32)
        carry_ref[...] = jnp.zeros_like(carry_ref)

    x = x_ref[...]
    u = _rms(x, gpre_ref[...])
    ubuf_ref[0:SSM_LAGS, :] = uprev_ref[cur]
    ubuf_ref[SSM_LAGS:, :] = u

    slot = lax.broadcasted_iota(jnp.int32, (tm, LANES), 1) // SSM_PAIR_CH
    im_lane = (lax.broadcasted_iota(jnp.int32, (1, S2), 1) // SSM_PAIR_STATE) % 2 == 1

    for c in range(SSM_TILES):
        cols = slice(c * LANES, (c + 1) * LANES)
        h_ref = h2_ref.at[c % 2]
        lagged = [ubuf_ref[SSM_LAGS - k:SSM_LAGS - k + tm, cols] for k in range(SSM_LAGS)]
        for q in range(NQ):
            halves = []
            for half in range(SSM_LAGS // NQ):
                acc = None
                for m in range(NQ):
                    piece = lagged[half * NQ + m]
                    if m != q:
                        piece = pltpu.roll(piece, (SSM_PAIR_CH * (m - q)) % LANES, axis=1)
                    acc = piece if acc is None else jnp.where(slot == m, piece, acc)
                halves.append(acc)
            lhs = jnp.concatenate(halves, axis=1).astype(BF16)
            h_ref[:, q * PW:(q + 1) * PW] = _dot(lhs, wp_ref[c, q])
        lr = jnp.broadcast_to(lamr_ref[c], (SUBLANES, S2))
        li = jnp.broadcast_to(jnp.where(im_lane, lami_ref[c], -lami_ref[c]), (SUBLANES, S2))

        def slab(m, carry):
            r0 = pl.multiple_of(m * SUBLANES, SUBLANES)
            new = h_ref[pl.ds(r0, SUBLANES), :] + (lr * carry + li * _swap_re_im(carry))
            h_ref[pl.ds(r0, SUBLANES), :] = new
            return new

        carry_ref[c] = lax.fori_loop(0, tm // SUBLANES, slab, carry_ref[c], unroll=True)
        y = None
        for q in range(NQ):
            h_re = h_ref[:, q * PW:q * PW + PS].astype(BF16)
            h_im = h_ref[:, q * PW + PS:(q + 1) * PW].astype(BF16)
            yq = _dot(h_re, cp_ref[c, q, 0:PS, :]) - _dot(h_im, cp_ref[c, q, PS:PW, :])
            y = yq if y is None else y + yq
        y_ref[:, cols] = y

    o_ref[...] = _glu_tail(x, u, y_ref[...], d_ref, wglu_ref, bglu_ref, gpost_ref)
    uprev_ref[nxt] = u[tm - SSM_LAGS:, :]

    @pl.when(t == pl.num_programs(1) - 1)
    def _():
        hout_ref[...] = carry_ref[...]


def _ssm_prompt(x, g_pre, g_post, tables, d_skip, w_glu, b_glu, layer, bsz, seq):
    w_pair, _, c_pair, _, _, lamk_re, lamk_im = tables
    tm = SSM_TOKENS
    per_seq = seq // tm
    S2 = 2 * SSM_TILE_STATE
    once = pl.Buffered(1)
    out, h_last = pl.pallas_call(
        _ssm_kernel,
        grid=(bsz, per_seq),
        in_specs=[
            pl.BlockSpec((tm, D_MODEL), lambda b, t: (b * per_seq + t, 0)),
            _const_spec((1, D_MODEL)),
            _const_spec((1, D_MODEL)),
            pl.BlockSpec(w_pair.shape, lambda b, t: (0, 0, 0, 0), pipeline_mode=once),
            pl.BlockSpec(c_pair.shape, lambda b, t: (0, 0, 0, 0), pipeline_mode=once),
            _const_spec(lamk_re.shape),
            _const_spec(lamk_im.shape),
            _const_spec((1, D_MODEL)),
            pl.BlockSpec((None, D_MODEL, 2 * D_MODEL), lambda b, t: (layer, 0, 0), pipeline_mode=once),
            _const_spec((1, 2 * D_MODEL)),
        ],
        out_specs=[
            pl.BlockSpec((tm, D_MODEL), lambda b, t: (b * per_seq + t, 0)),
            pl.BlockSpec((None, SSM_TILES, SUBLANES, S2), lambda b, t: (b, 0, 0, 0)),
        ],
        out_shape=[jax.ShapeDtypeStruct(x.shape, F32),
                   jax.ShapeDtypeStruct((bsz, SSM_TILES, SUBLANES, S2), F32)],
        scratch_shapes=[
            pltpu.VMEM((tm + SSM_LAGS, D_MODEL), F32),
            pltpu.VMEM((2, SSM_LAGS, D_MODEL), F32),
            pltpu.VMEM((2, tm, S2), F32),
            pltpu.VMEM((tm, D_MODEL), F32),
            pltpu.VMEM((SSM_TILES, SUBLANES, S2), F32),
        ],
        compiler_params=_params("arbitrary", "arbitrary"),
        name="ssm_prompt",
    )(x, g_pre, g_post, w_pair, c_pair, lamk_re, lamk_im, d_skip, w_glu, b_glu)
    h_last = h_last[:, :, SUBLANES - 1, :].reshape(bsz, SSM_TILES, SSM_TILE_PAIRS, 2, SSM_PAIR_STATE)
    shape = (bsz, SSM_GROUPS, SSM_STATE)
    return out, h_last[:, :, :, 0, :].reshape(shape), h_last[:, :, :, 1, :].reshape(shape)


def _ssm_sample_kernel(x_ref, sre_ref, sim_ref, gpre_ref, gpost_ref, w0_ref, cp_ref, lamr_ref, lami_ref,
                       d_ref, wglu_ref, bglu_ref, o_ref, nre_ref, nim_ref, y_ref):
    PS, PW = SSM_PAIR_STATE, SSM_PAIR_WIDTH
    x = x_ref[...]
    u = _rms(x, gpre_ref[...])
    for c in range(SSM_TILES):
        cols = slice(c * LANES, (c + 1) * LANES)
        bu = _dot(u[:, cols].astype(BF16), w0_ref[c])
        y = None
        for q in range(SSM_TILE_PAIRS):
            st = slice(c * SSM_TILE_STATE + q * PS, c * SSM_TILE_STATE + (q + 1) * PS)
            lr = lamr_ref[c][:, q * PW:q * PW + PS]
            li = lami_ref[c][:, q * PW:q * PW + PS]
            h0r, h0i = sre_ref[:, st], sim_ref[:, st]
            hr = bu[:, q * PW:q * PW + PS] + (lr * h0r - li * h0i)
            hi = bu[:, q * PW + PS:(q + 1) * PW] + (lr * h0i + li * h0r)
            nre_ref[:, st] = hr
            nim_ref[:, st] = hi
            yq = _dot(jnp.concatenate([hr, -hi], axis=1).astype(BF16), cp_ref[c, q])
            y = yq if y is None else y + yq
        y_ref[:, cols] = y
    o_ref[...] = _glu_tail(x, u, y_ref[...], d_ref, wglu_ref, bglu_ref, gpost_ref)


def _ssm_sample(x, state_re, state_im, g_pre, g_post, tables, d_skip, w_glu, b_glu, layer):
    _, w0, c_pair, lam1_re, lam1_im, _, _ = tables
    n = x.shape[0]
    flat = (n, SSM_GROUPS * SSM_STATE)
    st = jax.ShapeDtypeStruct(flat, F32)
    S2 = 2 * SSM_TILE_STATE
    out, nre, nim = pl.pallas_call(
        _ssm_sample_kernel,
        grid=(1,),
        in_specs=[
            _const_spec((n, D_MODEL)),
            _const_spec(flat),
            _const_spec(flat),
            _const_spec((1, D_MODEL)),
            _const_spec((1, D_MODEL)),
            _const_spec(w0.shape),
            _const_spec(c_pair.shape),
            _const_spec(lam1_re.shape),
            _const_spec(lam1_im.shape),
            _const_spec((1, D_MODEL)),
            pl.BlockSpec((None, D_MODEL, 2 * D_MODEL), lambda i: (layer, 0, 0)),
            _const_spec((1, 2 * D_MODEL)),
        ],
        out_specs=[_const_spec((n, D_MODEL)), _const_spec(flat), _const_spec(flat)],
        out_shape=[jax.ShapeDtypeStruct(x.shape, F32), st, st],
        scratch_shapes=[pltpu.VMEM((n, D_MODEL), F32)],
        compiler_params=_params("arbitrary"),
        name="ssm_sample",
    )(x, state_re.reshape(flat), state_im.reshape(flat), g_pre, g_post, w0, c_pair, lam1_re, lam1_im,
      d_skip, w_glu, b_glu)
    shape = (n, SSM_GROUPS, SSM_STATE)
    return out, nre.reshape(shape), nim.reshape(shape)


def kernel(x_prompt, x_sample, mem_prompt, state_ssm_re, state_ssm_im, cache_win_k, cache_win_v, cache_mem_k, cache_mem_v, norm_g, mem_norm_g, ffn_w_in, ffn_w_out, ssm_a_re, ssm_a_im, ssm_log_dt, ssm_b_re, ssm_b_im, ssm_c_re, ssm_c_im, ssm_d, ssm_w_glu, ssm_b_glu, attn_w_qkv, attn_b_qkv, attn_w_o, attn_sinks, ca_w_q, ca_w_kv, ca_w_o):
    bp, seq, _ = x_prompt.shape
    bs = x_sample.shape[0]
    xp = x_prompt.reshape(bp * seq, D_MODEL)
    xs = x_sample.reshape(bs, D_MODEL)

    gain = lambda i, r: norm_g[i, r].astype(F32).reshape(1, D_MODEL)
    ffn_w_in_b, ffn_w_out_b = ffn_w_in.astype(BF16), ffn_w_out.astype(BF16)
    ssm_w_glu_b = ssm_w_glu.astype(BF16)
    attn_w_qkv_b, attn_w_o_b = attn_w_qkv.astype(BF16), attn_w_o.astype(BF16)
    ca_w_q_b, ca_w_o_b = ca_w_q.astype(BF16), ca_w_o.astype(BF16)

    mem_k, mem_v = _mem_kv(mem_prompt, mem_norm_g.astype(F32), ca_w_kv.astype(BF16))
    cache_win_kt = cache_win_k.transpose(0, 1, 3, 4, 2)
    cache_win_vt = cache_win_v.transpose(0, 1, 3, 4, 2)
    rows8 = lambda a: a.reshape(DEPTH, bs, N_MEM, CA_HEADS, CA_DIM_TILES, LANES).transpose(0, 1, 2, 4, 3, 5).reshape(
        DEPTH, bs, N_MEM, CA_ROWS, LANES)
    cache_mem_k8, cache_mem_v8 = rows8(cache_mem_k), rows8(cache_mem_v)

    ssm_re_p, ssm_im_p, ssm_re_s, ssm_im_s = [], [], [], []
    wk_p, wv_p, wk_s, wv_s = [], [], [], []
    for i in range(DEPTH):
        li = i // N_MIXERS
        xp = _half_ffn(xp, gain(i, 0), gain(i, 1), ffn_w_in_b, ffn_w_out_b, i, 0, FFN_TOKENS)
        xs = _half_ffn(xs, gain(i, 0), gain(i, 1), ffn_w_in_b, ffn_w_out_b, i, 0, FFN_TOKENS)
        if i % N_MIXERS == 0:
            tables = _ssm_tables(ssm_a_re[li], ssm_a_im[li], ssm_log_dt[li], ssm_b_re[li], ssm_b_im[li],
                                 ssm_c_re[li], ssm_c_im[li])
            d_skip = ssm_d[li].astype(F32).reshape(1, D_MODEL)
            b_glu = ssm_b_glu[li].astype(F32).reshape(1, 2 * D_MODEL)
            xp, hr_p, hi_p = _ssm_prompt(xp, gain(i, 2), gain(i, 3), tables, d_skip, ssm_w_glu_b, b_glu, li,
                                         bp, seq)
            xs, hr_s, hi_s = _ssm_sample(xs, state_ssm_re[li], state_ssm_im[li], gain(i, 2), gain(i, 3), tables,
                                         d_skip, ssm_w_glu_b, b_glu, li)
            ssm_re_p.append(hr_p); ssm_im_p.append(hi_p)
            ssm_re_s.append(hr_s); ssm_im_s.append(hi_s)
        else:
            b_qkv = attn_b_qkv[li].astype(F32).reshape(1, QKV_WIDTH)
            sinks = attn_sinks[li].astype(F32)
            xp, bk_p, bv_p = _window_attention_prompt(xp, gain(i, 2), gain(i, 3), attn_w_qkv_b, b_qkv,
                                                      attn_w_o_b, sinks, li, bp, seq)
            xs, bk_s, bv_s = _window_attention_sample(xs, gain(i, 2), gain(i, 3), attn_w_qkv_b, b_qkv, attn_w_o_b,
                                                      sinks, cache_win_kt, cache_win_vt, li)
            wk_p.append(bk_p.reshape(bp, WINDOW, N_KV_HEADS, HEAD_DIM))
            wv_p.append(bv_p.reshape(bp, WINDOW, N_KV_HEADS, HEAD_DIM))
            wk_s.append(bk_s); wv_s.append(bv_s)
        xp = _cross_attention_prompt(xp, gain(i, 4), gain(i, 5), ca_w_q_b, ca_w_o_b, mem_k, mem_v, i, seq)
        xs = _cross_attention_sample(xs, gain(i, 4), gain(i, 5), ca_w_q_b, ca_w_o_b, cache_mem_k8, cache_mem_v8, i)
        xp = _half_ffn(xp, gain(i, 6), gain(i, 7), ffn_w_in_b, ffn_w_out_b, i, 1, FFN_TOKENS)
        xs = _half_ffn(xs, gain(i, 6), gain(i, 7), ffn_w_in_b, ffn_w_out_b, i, 1, FFN_TOKENS)

    mem_shape = (DEPTH, bp, N_MEM, CA_HEADS, CA_HEAD_DIM)
    return (xp.reshape(bp, seq, D_MODEL), xs.reshape(bs, 1, D_MODEL),
            jnp.stack(ssm_re_p), jnp.stack(ssm_im_p), jnp.stack(wk_p), jnp.stack(wv_p),
            mem_k.reshape(mem_shape), mem_v.reshape(mem_shape),
            jnp.stack(ssm_re_s), jnp.stack(ssm_im_s),
            jnp.stack(wk_s).transpose(0, 1, 4, 2, 3), jnp.stack(wv_s).transpose(0, 1, 4, 2, 3))
```

```python
import jax
import jax.numpy as jnp
from jax import lax
from jax.experimental import pallas as pl
from jax.experimental.pallas import tpu as pltpu

F32 = jnp.float32
BF16 = jnp.bfloat16

D_MODEL = 1024
DEPTH = 4
N_MIXERS = 2
SSM_GROUP = 16
SSM_GROUPS = D_MODEL // SSM_GROUP
SSM_STATE = 64
HEAD_DIM = 64
N_HEADS = D_MODEL // HEAD_DIM
N_KV_HEADS = 4
GQA = N_HEADS // N_KV_HEADS
WINDOW = 128
KV_WIDTH = N_KV_HEADS * HEAD_DIM
QKV_WIDTH = (N_HEADS + 2 * N_KV_HEADS) * HEAD_DIM
N_MEM = 256
CA_HEADS = 4
CA_HEAD_DIM = D_MODEL // CA_HEADS
CA_DIM_TILES = CA_HEAD_DIM // 128
CA_ROWS = CA_HEADS * CA_DIM_TILES
D_FF = ((8 * D_MODEL // 3 + 127) // 128) * 128
FFN_RES = 0.5
EPS = 1e-6
NEG = -1e30

SUBLANES = 8
LANES = 128
VMEM_LIMIT_BYTES = 56 * 1024 * 1024

SSM_LAGS = SUBLANES
SSM_TILE_GROUPS = LANES // SSM_GROUP
SSM_TILES = SSM_GROUPS // SSM_TILE_GROUPS
SSM_TILE_STATE = SSM_TILE_GROUPS * SSM_STATE
SSM_PAIR_GROUPS = 4
SSM_TILE_PAIRS = SSM_TILE_GROUPS // SSM_PAIR_GROUPS
SSM_PAIR_CH = SSM_PAIR_GROUPS * SSM_GROUP
SSM_PAIR_STATE = SSM_PAIR_GROUPS * SSM_STATE
SSM_PAIR_WIDTH = 2 * SSM_PAIR_STATE

FFN_CHUNK = 256
FFN_TOKENS = 2048
CA_TOKENS = 1024
SWA_TOKENS = 1024
SSM_TOKENS = 512
SAMPLE_CA_BLOCK = 4
SAMPLE_SWA_BLOCK = 16


def _params(*sem):
    return pltpu.CompilerParams(dimension_semantics=sem, vmem_limit_bytes=VMEM_LIMIT_BYTES)


def _rms(x, g):
    r = lax.rsqrt(jnp.mean(x * x, axis=-1, keepdims=True) + EPS)
    return x * r * g


def _dot(a, b):
    return jnp.dot(a, b, preferred_element_type=F32)


def _dot_nt(a, b):
    return lax.dot_general(a, b, (((1,), (1,)), ((), ())), preferred_element_type=F32)


def _const_spec(shape):
    zeros = (0,) * len(shape)
    return pl.BlockSpec(shape, lambda *_: zeros)


def _ffn_kernel(x_ref, gpre_ref, gpost_ref, wg_ref, wu_ref, wo_ref, o_ref, xn_ref, acc_ref):
    c = pl.program_id(1)

    @pl.when(c == 0)
    def _():
        xn_ref[...] = _rms(x_ref[...], gpre_ref[...]).astype(BF16)
        acc_ref[...] = jnp.zeros_like(acc_ref)

    xn = xn_ref[...]
    gate = _dot(xn, wg_ref[...])
    up = _dot(xn, wu_ref[...])
    h = (gate * jax.nn.sigmoid(gate) * up).astype(BF16)
    acc_ref[...] += _dot(h, wo_ref[...])

    @pl.when(c == pl.num_programs(1) - 1)
    def _():
        o_ref[...] = x_ref[...] + FFN_RES * _rms(acc_ref[...], gpost_ref[...])


def _half_ffn(x, g_pre, g_post, w_in, w_out, layer, half, tokens):
    n = x.shape[0]
    tm = min(tokens, n)
    n_chunks = D_FF // FFN_CHUNK
    return pl.pallas_call(
        _ffn_kernel,
        grid=(n // tm, n_chunks),
        in_specs=[
            pl.BlockSpec((tm, D_MODEL), lambda i, c: (i, 0)),
            _const_spec((1, D_MODEL)),
            _const_spec((1, D_MODEL)),
            pl.BlockSpec((None, None, D_MODEL, FFN_CHUNK), lambda i, c: (layer, half, 0, c)),
            pl.BlockSpec((None, None, D_MODEL, FFN_CHUNK), lambda i, c: (layer, half, 0, c + n_chunks)),
            pl.BlockSpec((None, None, FFN_CHUNK, D_MODEL), lambda i, c: (layer, half, c, 0)),
        ],
        out_specs=pl.BlockSpec((tm, D_MODEL), lambda i, c: (i, 0)),
        out_shape=jax.ShapeDtypeStruct((n, D_MODEL), F32),
        scratch_shapes=[pltpu.VMEM((tm, D_MODEL), BF16), pltpu.VMEM((tm, D_MODEL), F32)],
        compiler_params=_params("parallel", "arbitrary"),
        name="half_ffn",
    )(x, g_pre, g_post, w_in, w_in, w_out)


def _mem_kv_kernel(mem_ref, g_ref, w_ref, k_ref, v_ref):
    mn = _rms(mem_ref[...], g_ref[...]).astype(BF16)
    kv = _dot(mn, w_ref[...])
    k_ref[...] = kv[:, :D_MODEL]
    v_ref[...] = kv[:, D_MODEL:]


def _mem_kv(mem, g_mem, w_kv):
    bsz = mem.shape[0]
    out = jax.ShapeDtypeStruct((DEPTH, bsz, N_MEM, D_MODEL), F32)
    out_spec = pl.BlockSpec((None, None, N_MEM, D_MODEL), lambda l, b: (l, b, 0, 0))
    return pl.pallas_call(
        _mem_kv_kernel,
        grid=(DEPTH, bsz),
        in_specs=[
            pl.BlockSpec((None, N_MEM, D_MODEL), lambda l, b: (b, 0, 0)),
            pl.BlockSpec((None, 1, D_MODEL), lambda l, b: (l, 0, 0)),
            pl.BlockSpec((None, D_MODEL, 2 * D_MODEL), lambda l, b: (l, 0, 0)),
        ],
        out_specs=[out_spec, out_spec],
        out_shape=[out, out],
        compiler_params=_params("parallel", "parallel"),
        name="mem_kv",
    )(mem, g_mem.reshape(DEPTH, 1, D_MODEL), w_kv)


def _ca_kernel(x_ref, gpre_ref, gpost_ref, wq_ref, wo_ref, mk_ref, mv_ref, o_ref):
    x = x_ref[...]
    xn = _rms(x, gpre_ref[...]).astype(BF16)
    q = (_dot(xn, wq_ref[...]) * (CA_HEAD_DIM ** -0.5)).astype(BF16)
    cols = [slice(h * CA_HEAD_DIM, (h + 1) * CA_HEAD_DIM) for h in range(CA_HEADS)]
    scores = [_dot_nt(q[:, c], mk_ref[:, c].astype(BF16)) for c in cols]
    probs = [jnp.exp(s - jnp.max(s, axis=-1, keepdims=True)) for s in scores]
    dens = [jnp.sum(p, axis=-1, keepdims=True) for p in probs]
    outs = [_dot(p.astype(BF16), mv_ref[:, c].astype(BF16)) for p, c in zip(probs, cols)]
    o = jnp.concatenate([a / d for a, d in zip(outs, dens)], axis=-1).astype(BF16)
    o_ref[...] = x + _rms(_dot(o, wo_ref[...]), gpost_ref[...])


def _cross_attention_prompt(x, g_pre, g_post, w_q, w_o, mk, mv, layer, seq):
    n = x.shape[0]
    tm = CA_TOKENS
    per_seq = seq // tm
    mem_spec = pl.BlockSpec((None, None, N_MEM, D_MODEL), lambda i: (layer, i // per_seq, 0, 0))
    w_spec = pl.BlockSpec((None, D_MODEL, D_MODEL), lambda i: (layer, 0, 0))
    return pl.pallas_call(
        _ca_kernel,
        grid=(n // tm,),
        in_specs=[
            pl.BlockSpec((tm, D_MODEL), lambda i: (i, 0)),
            _const_spec((1, D_MODEL)),
            _const_spec((1, D_MODEL)),
            w_spec,
            w_spec,
            mem_spec,
            mem_spec,
        ],
        out_specs=pl.BlockSpec((tm, D_MODEL), lambda i: (i, 0)),
        out_shape=jax.ShapeDtypeStruct((n, D_MODEL), F32),
        compiler_params=_params("parallel"),
        name="cross_attn_prompt",
    )(x, g_pre, g_post, w_q, w_o, mk, mv)


def _stack_rows(pieces):
    n, w = len(pieces), pieces[0].shape[1]
    sub = lax.broadcasted_iota(jnp.int32, (n, w), 0)
    out = jnp.broadcast_to(pieces[0], (n, w))
    for r in range(1, n):
        out = jnp.where(sub == r, jnp.broadcast_to(pieces[r], (n, w)), out)
    return out


def _ca_sample_kernel(x_ref, gpre_ref, gpost_ref, wq_ref, wo_ref, mk_ref, mv_ref, o_ref, q_ref, att_ref):
    i = pl.program_id(0)

    @pl.when(i == 0)
    def _():
        xn = _rms(x_ref[...], gpre_ref[...]).astype(BF16)
        q_ref[...] = _dot(xn, wq_ref[...]) * (CA_HEAD_DIM ** -0.5)

    for b in range(SAMPLE_CA_BLOCK):
        row = i * SAMPLE_CA_BLOCK + b
        q = q_ref[pl.ds(row, 1), :]
        piece = lambda a, r: a[:, (r % CA_HEADS) * CA_HEAD_DIM + (r // CA_HEADS) * LANES:][:, :LANES]
        q8 = _stack_rows([piece(q, r) for r in range(CA_ROWS)])
        prod = (mk_ref[b] * q8[None]).reshape(N_MEM * CA_ROWS, LANES).astype(BF16)
        part = _dot(prod, jnp.ones((LANES, LANES), BF16)).reshape(N_MEM, CA_ROWS, LANES)
        s = part
        for j in range(1, CA_DIM_TILES):
            s = s + pltpu.roll(part, j * CA_HEADS, axis=1)
        p = jnp.exp(s - jnp.max(s, axis=0, keepdims=True))
        den = jnp.sum(p, axis=0)
        o8 = jnp.sum(p * mv_ref[b], axis=0) / den
        att_ref[pl.ds(row, 1), :] = jnp.concatenate(
            [o8[j * CA_HEADS + h:j * CA_HEADS + h + 1, :] for h in range(CA_HEADS) for j in range(CA_DIM_TILES)],
            axis=-1)

    @pl.when(i == pl.num_programs(0) - 1)
    def _():
        y = _dot(att_ref[...].astype(BF16), wo_ref[...])
        o_ref[...] = x_ref[...] + _rms(y, gpost_ref[...])


def _cross_attention_sample(x, g_pre, g_post, w_q, w_o, mk, mv, layer):
    n = x.shape[0]
    mem_spec = pl.BlockSpec((None, SAMPLE_CA_BLOCK, N_MEM, CA_ROWS, LANES), lambda i: (layer, i, 0, 0, 0))
    w_spec = pl.BlockSpec((None, D_MODEL, D_MODEL), lambda i: (layer, 0, 0))
    return pl.pallas_call(
        _ca_sample_kernel,
        grid=(n // SAMPLE_CA_BLOCK,),
        in_specs=[
            _const_spec((n, D_MODEL)),
            _const_spec((1, D_MODEL)),
            _const_spec((1, D_MODEL)),
            w_spec,
            w_spec,
            mem_spec,
            mem_spec,
        ],
        out_specs=_const_spec((n, D_MODEL)),
        out_shape=jax.ShapeDtypeStruct((n, D_MODEL), F32),
        scratch_shapes=[pltpu.VMEM((n, D_MODEL), F32), pltpu.VMEM((n, D_MODEL), F32)],
        compiler_params=_params("arbitrary"),
        name="cross_attn_sample",
    )(x, g_pre, g_post, w_q, w_o, mk, mv)


def _swa_kernel(sink_ref, x_ref, gpre_ref, gpost_ref, wqkv_ref, bqkv_ref, wo_ref,
                o_ref, kout_ref, vout_ref, kprev_ref, vprev_ref, q_ref):
    t = pl.program_id(1)
    tm = x_ref.shape[0]
    x = x_ref[...]
    xn = _rms(x, gpre_ref[...]).astype(BF16)
    qkv = _dot(xn, wqkv_ref[...]) + bqkv_ref[...]
    q_ref[...] = (qkv[:, :N_HEADS * HEAD_DIM] * (HEAD_DIM ** -0.5)).astype(BF16)
    k = qkv[:, N_HEADS * HEAD_DIM:N_HEADS * HEAD_DIM + KV_WIDTH]
    v = qkv[:, N_HEADS * HEAD_DIM + KV_WIDTH:]

    cur, nxt = t % 2, (t + 1) % 2

    @pl.when(t == 0)
    def _():
        kprev_ref[0] = jnp.zeros((WINDOW, KV_WIDTH), BF16)
        vprev_ref[0] = jnp.zeros((WINDOW, KV_WIDTH), BF16)

    kall = jnp.concatenate([kprev_ref[cur], k.astype(BF16)], axis=0)
    vall = jnp.concatenate([vprev_ref[cur], v.astype(BF16)], axis=0)

    row = lax.broadcasted_iota(jnp.int32, (GQA * WINDOW, 2 * WINDOW), 0)
    qi = row & (WINDOW - 1)
    kj = lax.broadcasted_iota(jnp.int32, (GQA * WINDOW, 2 * WINDOW), 1)
    band = (kj >= qi) & (kj <= qi + WINDOW)
    row_g = lax.broadcasted_iota(jnp.int32, (GQA * WINDOW, 1), 0) // WINDOW

    for n in range(tm // WINDOW):
        rows = slice(n * WINDOW, (n + 1) * WINDOW)
        kk = kall[n * WINDOW:(n + 2) * WINDOW, :]
        vv = vall[n * WINDOW:(n + 2) * WINDOW, :]
        valid = band & (kj >= WINDOW - (t * tm + n * WINDOW)) if n == 0 else band
        kv_cols = [slice(kh * HEAD_DIM, (kh + 1) * HEAD_DIM) for kh in range(N_KV_HEADS)]
        scores, sinks = [], []
        for kh in range(N_KV_HEADS):
            q4 = jnp.concatenate(
                [q_ref[rows, (kh * GQA + g) * HEAD_DIM:(kh * GQA + g + 1) * HEAD_DIM] for g in range(GQA)],
                axis=0)
            sink = jnp.full((GQA * WINDOW, 1), sink_ref[kh * GQA], F32)
            for g in range(1, GQA):
                sink = jnp.where(row_g == g, sink_ref[kh * GQA + g], sink)
            sinks.append(sink)
            scores.append(jnp.where(valid, _dot_nt(q4, kk[:, kv_cols[kh]]), NEG))
        maxes = [jnp.maximum(jnp.max(s, axis=-1, keepdims=True), sink) for s, sink in zip(scores, sinks)]
        probs = [jnp.exp(s - m).astype(BF16) for s, m in zip(scores, maxes)]
        ones = jnp.ones((2 * WINDOW, HEAD_DIM), BF16)
        outs = [_dot(p, jnp.concatenate([vv[:, kv_cols[kh]], ones], axis=1)) for kh, p in enumerate(probs)]
        heads = []
        for kh in range(N_KV_HEADS):
            den = outs[kh][:, HEAD_DIM:HEAD_DIM + 1] + jnp.exp(sinks[kh] - maxes[kh])
            o4 = outs[kh][:, :HEAD_DIM] / den
            heads += [o4[g * WINDOW:(g + 1) * WINDOW] for g in range(GQA)]
        att = jnp.concatenate(heads, axis=1).astype(BF16)
        o_ref[rows, :] = x_ref[rows, :] + _rms(_dot(att, wo_ref[...]), gpost_ref[...])

    kprev_ref[nxt] = k[tm - WINDOW:, :].astype(BF16)
    vprev_ref[nxt] = v[tm - WINDOW:, :].astype(BF16)

    kout_ref[...] = k[tm - WINDOW:, :]
    vout_ref[...] = v[tm - WINDOW:, :]


def _window_attention_prompt(x, g_pre, g_post, w_qkv, b_qkv, w_o, sinks, layer, bsz, seq):
    tm = SWA_TOKENS
    per_seq = seq // tm
    win_spec = pl.BlockSpec((None, WINDOW, KV_WIDTH), lambda b, t, *_: (b, 0, 0))
    win_shape = jax.ShapeDtypeStruct((bsz, WINDOW, KV_WIDTH), F32)
    grid_spec = pltpu.PrefetchScalarGridSpec(
        num_scalar_prefetch=1,
        grid=(bsz, per_seq),
        in_specs=[
            pl.BlockSpec((tm, D_MODEL), lambda b, t, *_: (b * per_seq + t, 0)),
            _const_spec((1, D_MODEL)),
            _const_spec((1, D_MODEL)),
            pl.BlockSpec((None, D_MODEL, QKV_WIDTH), lambda b, t, *_: (layer, 0, 0)),
            _const_spec((1, QKV_WIDTH)),
            pl.BlockSpec((None, N_HEADS * HEAD_DIM, D_MODEL), lambda b, t, *_: (layer, 0, 0)),
        ],
        out_specs=[pl.BlockSpec((tm, D_MODEL), lambda b, t, *_: (b * per_seq + t, 0)), win_spec, win_spec],
        scratch_shapes=[
            pltpu.VMEM((2, WINDOW, KV_WIDTH), BF16),
            pltpu.VMEM((2, WINDOW, KV_WIDTH), BF16),
            pltpu.VMEM((tm, N_HEADS * HEAD_DIM), BF16),
        ],
    )
    return pl.pallas_call(
        _swa_kernel,
        grid_spec=grid_spec,
        out_shape=[jax.ShapeDtypeStruct(x.shape, F32), win_shape, win_shape],
        compiler_params=_params("arbitrary", "arbitrary"),
        name="window_attn_prompt",
    )(sinks, x, g_pre, g_post, w_qkv, b_qkv, w_o)


def _swa_sample_kernel(sink_ref, x_ref, gpre_ref, gpost_ref, wqkv_ref, bqkv_ref, wo_ref, ck_ref, cv_ref,
                       o_ref, nk_ref, nv_ref, qkvt_ref, attt_ref, blkt_ref):
    i = pl.program_id(0)
    n = x_ref.shape[0]
    k0 = N_HEADS * HEAD_DIM
    v0 = k0 + KV_WIDTH

    @pl.when(i == 0)
    def _():
        xn = _rms(x_ref[...], gpre_ref[...]).astype(BF16)
        qkvt_ref[...] = (_dot(xn, wqkv_ref[...]) + bqkv_ref[...]).T
        attt_ref[...] = jnp.zeros_like(attt_ref)
        blkt_ref[...] = jnp.zeros_like(blkt_ref)

    base = i * SAMPLE_SWA_BLOCK
    qkvt = pltpu.roll(qkvt_ref[...], (n - base) % n, axis=1)
    newest = lax.broadcasted_iota(jnp.int32, (KV_WIDTH, WINDOW), 1) == WINDOW - 1
    heads3 = lambda a: a.reshape(N_HEADS, HEAD_DIM, a.shape[-1])
    per_q_head = lambda a: jnp.concatenate(
        [a[(h // GQA) * HEAD_DIM:(h // GQA + 1) * HEAD_DIM] for h in range(N_HEADS)], axis=0)
    sink = sink_ref[...]
    q_all = (qkvt[0:k0] * (HEAD_DIM ** -0.5)).astype(BF16)
    src_lane = lax.broadcasted_iota(jnp.int32, (LANES, LANES), 0)
    ones = jnp.ones((WINDOW, LANES), BF16)
    for b in range(SAMPLE_SWA_BLOCK):
        lane_b = slice(b, b + 1)
        q = _dot(q_all, jnp.where(src_lane == b, 1.0, 0.0).astype(BF16))
        k_new, v_new = qkvt[k0:v0, lane_b], qkvt[v0:, lane_b]
        kt = ck_ref[b].reshape(KV_WIDTH, WINDOW)
        vt = cv_ref[b].reshape(KV_WIDTH, WINDOW)
        s = jnp.sum(heads3(per_q_head(kt) * q), axis=1, keepdims=True)
        s_new = jnp.sum(heads3(per_q_head(k_new) * q[:, lane_b]), axis=1, keepdims=True)
        m = jnp.maximum(jnp.maximum(jnp.max(s, axis=2, keepdims=True), s_new), sink)
        p = jnp.exp(s - m)
        p_new = jnp.exp(s_new - m)
        den = jnp.sum(p, axis=2, keepdims=True) + p_new + jnp.exp(sink - m)
        pv = (heads3(per_q_head(vt)) * p).reshape(N_HEADS * HEAD_DIM, WINDOW).astype(BF16)
        o = heads3(_dot(pv, ones)[:, lane_b])
        o = (o + p_new * heads3(per_q_head(v_new))) / den
        blkt_ref[:, lane_b] = o.reshape(N_HEADS * HEAD_DIM, 1)
        nk_ref[b] = jnp.where(newest, k_new, pltpu.roll(kt, WINDOW - 1, axis=1)).reshape(nk_ref.shape[1:])
        nv_ref[b] = jnp.where(newest, v_new, pltpu.roll(vt, WINDOW - 1, axis=1)).reshape(nv_ref.shape[1:])

    lane = lax.broadcasted_iota(jnp.int32, attt_ref.shape, 1)
    mine = (lane >= base) & (lane < base + SAMPLE_SWA_BLOCK)
    attt_ref[...] = jnp.where(mine, pltpu.roll(blkt_ref[...], base, axis=1), attt_ref[...])

    @pl.when(i == pl.num_programs(0) - 1)
    def _():
        y = _dot(attt_ref[...].T.astype(BF16), wo_ref[...])
        o_ref[...] = x_ref[...] + _rms(y, gpost_ref[...])


def _window_attention_sample(x, g_pre, g_post, w_qkv, b_qkv, w_o, sinks, cache_k, cache_v, layer):
    n = x.shape[0]
    assert n == LANES, "the sample kernel keeps one sample per lane"
    blk = SAMPLE_SWA_BLOCK
    cache_spec = pl.BlockSpec((None, blk, N_KV_HEADS, HEAD_DIM, WINDOW), lambda i, *_: (layer, i, 0, 0, 0))
    win_spec = pl.BlockSpec((blk, N_KV_HEADS, HEAD_DIM, WINDOW), lambda i, *_: (i, 0, 0, 0))
    win_shape = jax.ShapeDtypeStruct((n, N_KV_HEADS, HEAD_DIM, WINDOW), F32)
    return pl.pallas_call(
        _swa_sample_kernel,
        grid=(n // blk,),
        in_specs=[
            _const_spec((N_HEADS, 1, 1)),
            _const_spec((n, D_MODEL)),
            _const_spec((1, D_MODEL)),
            _const_spec((1, D_MODEL)),
            pl.BlockSpec((None, D_MODEL, QKV_WIDTH), lambda i, *_: (layer, 0, 0)),
            _const_spec((1, QKV_WIDTH)),
            pl.BlockSpec((None, N_HEADS * HEAD_DIM, D_MODEL), lambda i, *_: (layer, 0, 0)),
            cache_spec,
            cache_spec,
        ],
        out_specs=[_const_spec((n, D_MODEL)), win_spec, win_spec],
        out_shape=[jax.ShapeDtypeStruct(x.shape, F32), win_shape, win_shape],
        scratch_shapes=[pltpu.VMEM((QKV_WIDTH, n), F32), pltpu.VMEM((N_HEADS * HEAD_DIM, n), F32),
                        pltpu.VMEM((N_HEADS * HEAD_DIM, n), F32)],
        compiler_params=_params("arbitrary"),
        name="window_attn_sample",
    )(sinks.reshape(N_HEADS, 1, 1), x, g_pre, g_post, w_qkv, b_qkv, w_o, cache_k, cache_v)


def _ssm_table_kernel(ar_ref, ai_ref, ldt_ref, br_ref, bi_ref, w_ref, lam_ref):
    P, PG, GS = SSM_STATE, SSM_PAIR_GROUPS, SSM_GROUP
    ar, ai, dt = ar_ref[...], ai_ref[...], jnp.exp(ldt_ref[...])
    mag = jnp.exp(ar * dt)
    lr, li = mag * jnp.cos(ai * dt), mag * jnp.sin(ai * dt)
    den = ar * ar + ai * ai
    nr, ni = lr - 1.0, li
    zr = (nr * ar + ni * ai) / den
    zi = (ni * ar - nr * ai) / den
    br, bi = br_ref[...], bi_ref[...]
    wr = zr * br - zi * bi
    wi = zr * bi + zi * br
    pr, pi = lr, li
    rows = PG * GS
    own = (lax.broadcasted_iota(jnp.int32, (rows, PG * P), 0) // GS
           == lax.broadcasted_iota(jnp.int32, (rows, PG * P), 1) // P)
    for k in range(SSM_LAGS):
        parts = [jnp.where(own, jnp.concatenate([w] * PG, axis=1), 0.0) for w in (wr, wi)]
        w_ref[k * rows:(k + 1) * rows, :] = jnp.concatenate(parts, axis=1).astype(BF16)
        wr, wi = lr * wr - li * wi, lr * wi + li * wr
        if k > 0:
            pr, pi = lr * pr - li * pi, lr * pi + li * pr
    lam_ref[0] = lr
    lam_ref[1] = li
    lam_ref[2] = pr
    lam_ref[3] = pi


def _ssm_tables(a_re, a_im, log_dt, b_re, b_im, c_re, c_im):
    G, P, GS, R, NT = SSM_GROUPS, SSM_STATE, SSM_GROUP, SSM_LAGS, SSM_TILES
    NQ, PG = SSM_TILE_PAIRS, SSM_PAIR_GROUPS
    iota = lambda n: jnp.arange(n, dtype=jnp.int32)
    clusters = NT * NQ
    per_chan = lambda a: jnp.broadcast_to(a.astype(F32).reshape(clusters, PG, 1, P), (clusters, PG, GS, P)).reshape(
        clusters, SSM_PAIR_CH, P)
    b_t = lambda b: b.astype(F32).transpose(0, 2, 1).reshape(clusters, SSM_PAIR_CH, P)
    in_spec = pl.BlockSpec((None, SSM_PAIR_CH, P), lambda i: (i, 0, 0))
    w_pair, lam = pl.pallas_call(
        _ssm_table_kernel,
        grid=(clusters,),
        in_specs=[in_spec] * 5,
        out_specs=[pl.BlockSpec((None, R * SSM_PAIR_CH, SSM_PAIR_WIDTH), lambda i: (i, 0, 0)),
                   pl.BlockSpec((None, 4, SSM_PAIR_CH, P), lambda i: (i, 0, 0, 0))],
        out_shape=[jax.ShapeDtypeStruct((clusters, R * SSM_PAIR_CH, SSM_PAIR_WIDTH), BF16),
                   jax.ShapeDtypeStruct((clusters, 4, SSM_PAIR_CH, P), F32)],
        compiler_params=_params("parallel"),
        name="ssm_tables",
    )(per_chan(a_re), per_chan(a_im), per_chan(jnp.broadcast_to(log_dt[:, None], (G, P))), b_t(b_re), b_t(b_im))
    w_pair = w_pair.reshape(NT, NQ, R * SSM_PAIR_CH, SSM_PAIR_WIDTH)
    lam = lam[:, :, ::GS, :].transpose(1, 0, 2, 3).reshape(4, G * P)
    w0 = jnp.tile(w_pair[:, :, :SSM_PAIR_CH, None, :], (1, 1, 1, NQ, 1))
    w0 = jnp.where((iota(NQ)[:, None, None, None] == iota(NQ)[None, None, :, None])[None], w0, 0).reshape(
        NT, LANES, NQ * SSM_PAIR_WIDTH)
    ct = jnp.stack([c_re, c_im]).astype(BF16).reshape(2, NT, NQ, PG, GS, P).transpose(1, 2, 0, 3, 5, 4)
    ct = jnp.tile(ct.reshape(NT, NQ, SSM_PAIR_WIDTH, GS), (1, 1, 1, LANES // GS))
    want_slot = iota(NQ)[:, None] * PG + ((iota(SSM_PAIR_WIDTH) // P) % PG)[None, :]
    c_pair = jnp.where(want_slot[:, :, None] == (iota(LANES) // GS)[None, None, :], ct, 0)

    def lam_rows(v):
        v = v.reshape(NT, NQ, SSM_PAIR_STATE)
        return jnp.concatenate([v, v], axis=-1).reshape(NT, 1, NQ * SSM_PAIR_WIDTH)

    return (w_pair, w0, c_pair, lam_rows(lam[0]), lam_rows(lam[1]), lam_rows(lam[2]), lam_rows(lam[3]))


def _glu_tail(x, u, y, d_ref, wglu_ref, bglu_ref, gpost_ref):
    y = y + d_ref[...] * u
    y = 0.5 * y * (1.0 + lax.erf(y * (2.0 ** -0.5)))
    z = _dot(y.astype(BF16), wglu_ref[...]) + bglu_ref[...]
    out = z[:, :D_MODEL] * jax.nn.sigmoid(z[:, D_MODEL:])
    return x + _rms(out, gpost_ref[...])


def _swap_re_im(a):
    tiles = a.shape[-1] // SSM_PAIR_STATE
    return jnp.concatenate(
        [a[:, (j ^ 1) * SSM_PAIR_STATE:((j ^ 1) + 1) * SSM_PAIR_STATE] for j in range(tiles)], axis=1)


def _ssm_kernel(x_ref, gpre_ref, gpost_ref, wp_ref, cp_ref, lamr_ref, lami_ref, d_ref, wglu_ref, bglu_ref,
                o_ref, hout_ref, ubuf_ref, uprev_ref, h2_ref, y_ref, carry_ref):
    t = pl.program_id(1)
    tm = x_ref.shape[0]
    NQ, PS, PW = SSM_TILE_PAIRS, SSM_PAIR_STATE, SSM_PAIR_WIDTH
    S2 = NQ * PW
    cur, nxt = t % 2, (t + 1) % 2

    @pl.when(t == 0)
    def _():
        uprev_ref[0] = jnp.zeros((SSM_LAGS, D_MODEL), F32)
        carry_ref[...] = jnp.zeros_like(carry_ref)

    x = x_ref[...]
    u = _rms(x, gpre_ref[...])
    ubuf_ref[0:SSM_LAGS, :] = uprev_ref[cur]
    ubuf_ref[SSM_LAGS:, :] = u

    slot = lax.broadcasted_iota(jnp.int32, (tm, LANES), 1) // SSM_PAIR_CH
    im_lane = (lax.broadcasted_iota(jnp.int32, (1, S2), 1) // SSM_PAIR_STATE) % 2 == 1

    for c in range(SSM_TILES):
        cols = slice(c * LANES, (c + 1) * LANES)
        h_ref = h2_ref.at[c % 2]
        lagged = [ubuf_ref[SSM_LAGS - k:SSM_LAGS - k + tm, cols] for k in range(SSM_LAGS)]
        for q in range(NQ):
            halves = []
            for half in range(SSM_LAGS // NQ):
                acc = None
                for m in range(NQ):
                    piece = lagged[half * NQ + m]
                    if m != q:
                        piece = pltpu.roll(piece, (SSM_PAIR_CH * (m - q)) % LANES, axis=1)
                    acc = piece if acc is None else jnp.where(slot == m, piece, acc)
                halves.append(acc)
            lhs = jnp.concatenate(halves, axis=1).astype(BF16)
            h_ref[:, q * PW:(q + 1) * PW] = _dot(lhs, wp_ref[c, q])
        lr = jnp.broadcast_to(lamr_ref[c], (SUBLANES, S2))
        li = jnp.broadcast_to(jnp.where(im_lane, lami_ref[c], -lami_ref[c]), (SUBLANES, S2))

        def slab(m, carry):
            r0 = pl.multiple_of(m * SUBLANES, SUBLANES)
            new = h_ref[pl.ds(r0, SUBLANES), :] + (lr * carry + li * _swap_re_im(carry))
            h_ref[pl.ds(r0, SUBLANES), :] = new
            return new

        carry_ref[c] = lax.fori_loop(0, tm // SUBLANES, slab, carry_ref[c], unroll=True)
        y = None
        for q in range(NQ):
            h_re = h_ref[:, q * PW:q * PW + PS].astype(BF16)
            h_im = h_ref[:, q * PW + PS:(q + 1) * PW].astype(BF16)
            yq = _dot(h_re, cp_ref[c, q, 0:PS, :]) - _dot(h_im, cp_ref[c, q, PS:PW, :])
            y = yq if y is None else y + yq
        y_ref[:, cols] = y

    o_ref[...] = _glu_tail(x, u, y_ref[...], d_ref, wglu_ref, bglu_ref, gpost_ref)
    uprev_ref[nxt] = u[tm - SSM_LAGS:, :]

    @pl.when(t == pl.num_programs(1) - 1)
    def _():
        hout_ref[...] = carry_ref[...]


def _ssm_prompt(x, g_pre, g_post, tables, d_skip, w_glu, b_glu, layer, bsz, seq):
    w_pair, _, c_pair, _, _, lamk_re, lamk_im = tables
    tm = SSM_TOKENS
    per_seq = seq // tm
    S2 = 2 * SSM_TILE_STATE
    once = pl.Buffered(1)
    out, h_last = pl.pallas_call(
        _ssm_kernel,
        grid=(bsz, per_seq),
        in_specs=[
            pl.BlockSpec((tm, D_MODEL), lambda b, t: (b * per_seq + t, 0)),
            _const_spec((1, D_MODEL)),
            _const_spec((1, D_MODEL)),
            pl.BlockSpec(w_pair.shape, lambda b, t: (0, 0, 0, 0), pipeline_mode=once),
            pl.BlockSpec(c_pair.shape, lambda b, t: (0, 0, 0, 0), pipeline_mode=once),
            _const_spec(lamk_re.shape),
            _const_spec(lamk_im.shape),
            _const_spec((1, D_MODEL)),
            pl.BlockSpec((None, D_MODEL, 2 * D_MODEL), lambda b, t: (layer, 0, 0), pipeline_mode=once),
            _const_spec((1, 2 * D_MODEL)),
        ],
        out_specs=[
            pl.BlockSpec((tm, D_MODEL), lambda b, t: (b * per_seq + t, 0)),
            pl.BlockSpec((None, SSM_TILES, SUBLANES, S2), lambda b, t: (b, 0, 0, 0)),
        ],
        out_shape=[jax.ShapeDtypeStruct(x.shape, F32),
                   jax.ShapeDtypeStruct((bsz, SSM_TILES, SUBLANES, S2), F32)],
        scratch_shapes=[
            pltpu.VMEM((tm + SSM_LAGS, D_MODEL), F32),
            pltpu.VMEM((2, SSM_LAGS, D_MODEL), F32),
            pltpu.VMEM((2, tm, S2), F32),
            pltpu.VMEM((tm, D_MODEL), F32),
            pltpu.VMEM((SSM_TILES, SUBLANES, S2), F32),
        ],
        compiler_params=_params("arbitrary", "arbitrary"),
        name="ssm_prompt",
    )(x, g_pre, g_post, w_pair, c_pair, lamk_re, lamk_im, d_skip, w_glu, b_glu)
    h_last = h_last[:, :, SUBLANES - 1, :].reshape(bsz, SSM_TILES, SSM_TILE_PAIRS, 2, SSM_PAIR_STATE)
    shape = (bsz, SSM_GROUPS, SSM_STATE)
    return out, h_last[:, :, :, 0, :].reshape(shape), h_last[:, :, :, 1, :].reshape(shape)


def _ssm_sample_kernel(x_ref, sre_ref, sim_ref, gpre_ref, gpost_ref, w0_ref, cp_ref, lamr_ref, lami_ref,
                       d_ref, wglu_ref, bglu_ref, o_ref, nre_ref, nim_ref, y_ref):
    PS, PW = SSM_PAIR_STATE, SSM_PAIR_WIDTH
    x = x_ref[...]
    u = _rms(x, gpre_ref[...])
    for c in range(SSM_TILES):
        cols = slice(c * LANES, (c + 1) * LANES)
        bu = _dot(u[:, cols].astype(BF16), w0_ref[c])
        y = None
        for q in range(SSM_TILE_PAIRS):
            st = slice(c * SSM_TILE_STATE + q * PS, c * SSM_TILE_STATE + (q + 1) * PS)
            lr = lamr_ref[c][:, q * PW:q * PW + PS]
            li = lami_ref[c][:, q * PW:q * PW + PS]
            h0r, h0i = sre_ref[:, st], sim_ref[:, st]
            hr = bu[:, q * PW:q * PW + PS] + (lr * h0r - li * h0i)
            hi = bu[:, q * PW + PS:(q + 1) * PW] + (lr * h0i + li * h0r)
            nre_ref[:, st] = hr
            nim_ref[:, st] = hi
            yq = _dot(jnp.concatenate([hr, -hi], axis=1).astype(BF16), cp_ref[c, q])
            y = yq if y is None else y + yq
        y_ref[:, cols] = y
    o_ref[...] = _glu_tail(x, u, y_ref[...], d_ref, wglu_ref, bglu_ref, gpost_ref)


def _ssm_sample(x, state_re, state_im, g_pre, g_post, tables, d_skip, w_glu, b_glu, layer):
    _, w0, c_pair, lam1_re, lam1_im, _, _ = tables
    n = x.shape[0]
    flat = (n, SSM_GROUPS * SSM_STATE)
    st = jax.ShapeDtypeStruct(flat, F32)
    S2 = 2 * SSM_TILE_STATE
    out, nre, nim = pl.pallas_call(
        _ssm_sample_kernel,
        grid=(1,),
        in_specs=[
            _const_spec((n, D_MODEL)),
            _const_spec(flat),
            _const_spec(flat),
            _const_spec((1, D_MODEL)),
            _const_spec((1, D_MODEL)),
            _const_spec(w0.shape),
            _const_spec(c_pair.shape),
            _const_spec(lam1_re.shape),
            _const_spec(lam1_im.shape),
            _const_spec((1, D_MODEL)),
            pl.BlockSpec((None, D_MODEL, 2 * D_MODEL), lambda i: (layer, 0, 0)),
            _const_spec((1, 2 * D_MODEL)),
        ],
        out_specs=[_const_spec((n, D_MODEL)), _const_spec(flat), _const_spec(flat)],
        out_shape=[jax.ShapeDtypeStruct(x.shape, F32), st, st],
        scratch_shapes=[pltpu.VMEM((n, D_MODEL), F32)],
        compiler_params=_params("arbitrary"),
        name="ssm_sample",
    )(x, state_re.reshape(flat), state_im.reshape(flat), g_pre, g_post, w0, c_pair, lam1_re, lam1_im,
      d_skip, w_glu, b_glu)
    shape = (n, SSM_GROUPS, SSM_STATE)
    return out, nre.reshape(shape), nim.reshape(shape)


def kernel(x_prompt, x_sample, mem_prompt, state_ssm_re, state_ssm_im, cache_win_k, cache_win_v, cache_mem_k, cache_mem_v, norm_g, mem_norm_g, ffn_w_in, ffn_w_out, ssm_a_re, ssm_a_im, ssm_log_dt, ssm_b_re, ssm_b_im, ssm_c_re, ssm_c_im, ssm_d, ssm_w_glu, ssm_b_glu, attn_w_qkv, attn_b_qkv, attn_w_o, attn_sinks, ca_w_q, ca_w_kv, ca_w_o):
    bp, seq, _ = x_prompt.shape
    bs = x_sample.shape[0]
    xp = x_prompt.reshape(bp * seq, D_MODEL)
    xs = x_sample.reshape(bs, D_MODEL)

    gain = lambda i, r: norm_g[i, r].astype(F32).reshape(1, D_MODEL)
    ffn_w_in_b, ffn_w_out_b = ffn_w_in.astype(BF16), ffn_w_out.astype(BF16)
    ssm_w_glu_b = ssm_w_glu.astype(BF16)
    attn_w_qkv_b, attn_w_o_b = attn_w_qkv.astype(BF16), attn_w_o.astype(BF16)
    ca_w_q_b, ca_w_o_b = ca_w_q.astype(BF16), ca_w_o.astype(BF16)

    mem_k, mem_v = _mem_kv(mem_prompt, mem_norm_g.astype(F32), ca_w_kv.astype(BF16))
    cache_win_kt = cache_win_k.transpose(0, 1, 3, 4, 2)
    cache_win_vt = cache_win_v.transpose(0, 1, 3, 4, 2)
    rows8 = lambda a: a.reshape(DEPTH, bs, N_MEM, CA_HEADS, CA_DIM_TILES, LANES).transpose(0, 1, 2, 4, 3, 5).reshape(
        DEPTH, bs, N_MEM, CA_ROWS, LANES)
    cache_mem_k8, cache_mem_v8 = rows8(cache_mem_k), rows8(cache_mem_v)

    ssm_re_p, ssm_im_p, ssm_re_s, ssm_im_s = [], [], [], []
    wk_p, wv_p, wk_s, wv_s = [], [], [], []
    for i in range(DEPTH):
        li = i // N_MIXERS
        xp = _half_ffn(xp, gain(i, 0), gain(i, 1), ffn_w_in_b, ffn_w_out_b, i, 0, FFN_TOKENS)
        xs = _half_ffn(xs, gain(i, 0), gain(i, 1), ffn_w_in_b, ffn_w_out_b, i, 0, FFN_TOKENS)
        if i % N_MIXERS == 0:
            tables = _ssm_tables(ssm_a_re[li], ssm_a_im[li], ssm_log_dt[li], ssm_b_re[li], ssm_b_im[li],
                                 ssm_c_re[li], ssm_c_im[li])
            d_skip = ssm_d[li].astype(F32).reshape(1, D_MODEL)
            b_glu = ssm_b_glu[li].astype(F32).reshape(1, 2 * D_MODEL)
            xp, hr_p, hi_p = _ssm_prompt(xp, gain(i, 2), gain(i, 3), tables, d_skip, ssm_w_glu_b, b_glu, li,
                                         bp, seq)
            xs, hr_s, hi_s = _ssm_sample(xs, state_ssm_re[li], state_ssm_im[li], gain(i, 2), gain(i, 3), tables,
                                         d_skip, ssm_w_glu_b, b_glu, li)
            ssm_re_p.append(hr_p); ssm_im_p.append(hi_p)
            ssm_re_s.append(hr_s); ssm_im_s.append(hi_s)
        else:
            b_qkv = attn_b_qkv[li].astype(F32).reshape(1, QKV_WIDTH)
            sinks = attn_sinks[li].astype(F32)
            xp, bk_p, bv_p = _window_attention_prompt(xp, gain(i, 2), gain(i, 3), attn_w_qkv_b, b_qkv,
                                                      attn_w_o_b, sinks, li, bp, seq)
            xs, bk_s, bv_s = _window_attention_sample(xs, gain(i, 2), gain(i, 3), attn_w_qkv_b, b_qkv, attn_w_o_b,
                                                      sinks, cache_win_kt, cache_win_vt, li)
            wk_p.append(bk_p.reshape(bp, WINDOW, N_KV_HEADS, HEAD_DIM))
            wv_p.append(bv_p.reshape(bp, WINDOW, N_KV_HEADS, HEAD_DIM))
            wk_s.append(bk_s); wv_s.append(bv_s)
        xp = _cross_attention_prompt(xp, gain(i, 4), gain(i, 5), ca_w_q_b, ca_w_o_b, mem_k, mem_v, i, seq)
        xs = _cross_attention_sample(xs, gain(i, 4), gain(i, 5), ca_w_q_b, ca_w_o_b, cache_mem_k8, cache_mem_v8, i)
        xp = _half_ffn(xp, gain(i, 6), gain(i, 7), ffn_w_in_b, ffn_w_out_b, i, 1, FFN_TOKENS)
        xs = _half_ffn(xs, gain(i, 6), gain(i, 7), ffn_w_in_b, ffn_w_out_b, i, 1, FFN_TOKENS)

    mem_shape = (DEPTH, bp, N_MEM, CA_HEADS, CA_HEAD_DIM)
    return (xp.reshape(bp, seq, D_MODEL), xs.reshape(bs, 1, D_MODEL),
            jnp.stack(ssm_re_p), jnp.stack(ssm_im_p), jnp.stack(wk_p), jnp.stack(wv_p),
            mem_k.reshape(mem_shape), mem_v.reshape(mem_shape),
            jnp.stack(ssm_re_s), jnp.stack(ssm_im_s),
            jnp.stack(wk_s).transpose(0, 1, 4, 2, 3), jnp.stack(wv_s).transpose(0, 1, 4, 2, 3))
```
